```python
import jax, jax.numpy as jnp
from jax import lax
import numpy as np

D_MODEL = 2048
BATCH = 4
SEQ = 2048
DEPTH = 1
DEC_BATCH = 16
DEC_SEQ = 64
PAST_LEN = 2048

CHUNK = 64
GMLP_CHUNK = 128
GMLP_WIDTH = D_MODEL
GMLP_GROUPS = 8
GMLP_GROUP_DIM = GMLP_WIDTH // GMLP_GROUPS
N_HEADS = 32
N_KV_HEADS = 4
HEAD_DIM = 64
Q_REP = N_HEADS // N_KV_HEADS
WINDOW = 128
WIN_CHUNKS = WINDOW // CHUNK
ROPE_THETA = 500000.0
ROT_DIM = HEAD_DIM // 4
D_FF = ((8 * D_MODEL // 3 + 255) // 256) * 256
Q_WIDTH = N_HEADS * HEAD_DIM
KV_WIDTH = N_KV_HEADS * HEAD_DIM
IN_COLS = 2 * GMLP_WIDTH + Q_WIDTH + 2 * KV_WIDTH
EPS = 1e-6
NEG = -1e30

kernel_name = "hybrid_gmlp_swa_streaming_step"


def _rmsnorm(x, g):
    xf = x.astype(jnp.float32)
    y = xf * lax.rsqrt(jnp.mean(xf * xf, axis=-1, keepdims=True) + EPS)
    return (y * g.astype(jnp.float32)).astype(x.dtype)


def _layernorm(x, g, b):
    xf = x.astype(jnp.float32)
    mu = jnp.mean(xf, axis=-1, keepdims=True)
    var = jnp.mean(jnp.square(xf - mu), axis=-1, keepdims=True)
    y = (xf - mu) * lax.rsqrt(var + EPS)
    return (y * g.astype(jnp.float32) + b.astype(jnp.float32)).astype(x.dtype)


def _partial_rope(x, pos):
    half = ROT_DIM // 2
    inv_freq = jnp.float32(ROPE_THETA) ** (-(jnp.arange(half, dtype=jnp.float32) * 2.0 / ROT_DIM))
    ang = pos[:, None] * inv_freq[None, :]
    cos = jnp.cos(ang)[:, None, :]
    sin = jnp.sin(ang)[:, None, :]
    xr = x[..., :ROT_DIM].astype(jnp.float32)
    x1, x2 = xr[..., :half], xr[..., half:]
    rot = jnp.concatenate([x1 * cos - x2 * sin, x2 * cos + x1 * sin], axis=-1)
    return jnp.concatenate([rot.astype(x.dtype), x[..., ROT_DIM:]], axis=-1)


def _project(x, norm_g, w_in):
    B, S, _ = x.shape
    xn = _rmsnorm(x, norm_g)
    h = xn @ w_in
    z = jax.nn.gelu(h[..., :2 * GMLP_WIDTH])
    u, v = z[..., :GMLP_WIDTH], z[..., GMLP_WIDTH:]
    o = 2 * GMLP_WIDTH
    q = h[..., o:o + Q_WIDTH].reshape(B, S, N_HEADS, HEAD_DIM)
    k = h[..., o + Q_WIDTH:o + Q_WIDTH + KV_WIDTH].reshape(B, S, N_KV_HEADS, HEAD_DIM)
    va = h[..., o + Q_WIDTH + KV_WIDTH:].reshape(B, S, N_KV_HEADS, HEAD_DIM)
    return xn, u, v, q, k, va


def _gmlp_mix(u, vn, ws, bs):
    B, N, L, _ = vn.shape
    i = jnp.arange(GMLP_CHUNK)
    mask = (i[None, :] // CHUNK) <= (i[:, None] // CHUNK)
    wm = (ws * mask.astype(ws.dtype))[:, :L, :L]
    vg = vn.reshape(B, N, L, GMLP_GROUPS, GMLP_GROUP_DIM)
    s = jnp.einsum('gij,bnjgc->bnigc', wm, vg) + bs[:, :L].T[None, None, :, :, None]
    return u * s.reshape(B, N, L, GMLP_WIDTH)


def _sink_softmax(s, sink):
    sk = sink.astype(jnp.float32)[:, :, None]
    m = jnp.maximum(jnp.max(s, axis=-1), sk)
    p = jnp.exp(s - m[..., None])
    den = jnp.sum(p, axis=-1) + jnp.exp(sk - m)
    return p / den[..., None]


def _swa_prompt(q, k, v, sinks):
    B, S, _, _ = q.shape
    nC = S // CHUNK
    padw = ((0, 0), (WINDOW, 0), (0, 0), (0, 0))
    kp = jnp.pad(k, padw).reshape(B, nC + WIN_CHUNKS, CHUNK, N_KV_HEADS, HEAD_DIM)
    vp = jnp.pad(v, padw).reshape(B, nC + WIN_CHUNKS, CHUNK, N_KV_HEADS, HEAD_DIM)
    kb = jnp.concatenate([kp[:, i:i + nC] for i in range(WIN_CHUNKS + 1)], axis=2)
    vb = jnp.concatenate([vp[:, i:i + nC] for i in range(WIN_CHUNKS + 1)], axis=2)
    key_chunk = jnp.arange(nC)[:, None] + jnp.repeat(jnp.arange(WIN_CHUNKS + 1), CHUNK)[None, :] - WIN_CHUNKS
    valid = key_chunk >= 0
    qg = q.reshape(B, nC, CHUNK, N_KV_HEADS, Q_REP, HEAD_DIM)
    s = jnp.einsum('bnqhrd,bnkhd->bnhrqk', qg, kb).astype(jnp.float32) * (HEAD_DIM ** -0.5)
    s = jnp.where(valid[None, :, None, None, None, :], s, NEG)
    p = _sink_softmax(s, sinks.reshape(N_KV_HEADS, Q_REP)).astype(v.dtype)
    o = jnp.einsum('bnhrqk,bnkhd->bnqhrd', p, vb)
    return o.reshape(B, S, Q_WIDTH)


def _swa_sample(q, k_all, v_all, sinks):
    B, T, _, _ = q.shape
    qg = q.reshape(B, T, N_KV_HEADS, Q_REP, HEAD_DIM)
    s = jnp.einsum('bqhrd,bkhd->bhrqk', qg, k_all).astype(jnp.float32) * (HEAD_DIM ** -0.5)
    p = _sink_softmax(s, sinks.reshape(N_KV_HEADS, Q_REP)).astype(v_all.dtype)
    o = jnp.einsum('bhrqk,bkhd->bqhrd', p, v_all)
    return o.reshape(B, T, Q_WIDTH)


def _merge_and_ffn(x, xn, a, o, w_gate, b_gate, w_branch_a, w_branch_b, w_out,
                   norm_ffn_g, w_ffn_gate, w_ffn_up, w_ffn_down):
    g = jax.nn.sigmoid(xn @ w_gate + b_gate)
    g_a, g_b = g[..., :D_MODEL], g[..., D_MODEL:]
    x = x + (g_a * (a @ w_branch_a) + g_b * (o @ w_branch_b)) @ w_out
    h = _rmsnorm(x, norm_ffn_g)
    return x + (jax.nn.silu(h @ w_ffn_gate) * (h @ w_ffn_up)) @ w_ffn_down


def setup_inputs(seed: int = 0) -> dict:
    key = jax.random.key(seed)
    ks = jax.random.split(key, 24)
    f32 = jnp.float32
    swa_len = min(WINDOW, PAST_LEN)

    def nrm(k, shape, scale):
        return jax.random.normal(k, shape, f32) * scale

    return {
        "x_prompt": nrm(ks[0], (BATCH, SEQ, D_MODEL), 1.0),
        "x_sample": nrm(ks[1], (DEC_BATCH, DEC_SEQ, D_MODEL), 1.0),
        "cache_swa_k": nrm(ks[2], (DEPTH, DEC_BATCH, swa_len, N_KV_HEADS, HEAD_DIM), 1.0),
        "cache_swa_v": nrm(ks[3], (DEPTH, DEC_BATCH, swa_len, N_KV_HEADS, HEAD_DIM), 1.0),
        "norm_mix_g": 1.0 + nrm(ks[4], (DEPTH, D_MODEL), 0.02),
        "w_in": nrm(ks[5], (DEPTH, D_MODEL, IN_COLS), D_MODEL ** -0.5),
        "gmlp_ln_g": 1.0 + nrm(ks[6], (DEPTH, GMLP_WIDTH), 0.02),
        "gmlp_ln_b": nrm(ks[7], (DEPTH, GMLP_WIDTH), 0.02),
        "gmlp_ws": nrm(ks[8], (DEPTH, GMLP_GROUPS, GMLP_CHUNK, GMLP_CHUNK), GMLP_CHUNK ** -0.5),
        "gmlp_bs": 1.0 + nrm(ks[9], (DEPTH, GMLP_GROUPS, GMLP_CHUNK), 0.1),
        "attn_sinks": nrm(ks[10], (DEPTH, N_HEADS), 0.5),
        "w_gate": nrm(ks[11], (DEPTH, D_MODEL, 2 * D_MODEL), D_MODEL ** -0.5),
        "b_gate": nrm(ks[12], (DEPTH, 2 * D_MODEL), 0.02),
        "w_branch_a": nrm(ks[13], (DEPTH, GMLP_WIDTH, D_MODEL), GMLP_WIDTH ** -0.5),
        "w_branch_b": nrm(ks[14], (DEPTH, Q_WIDTH, D_MODEL), Q_WIDTH ** -0.5),
        "w_out": nrm(ks[15], (DEPTH, D_MODEL, D_MODEL), D_MODEL ** -0.5),
        "norm_ffn_g": 1.0 + nrm(ks[16], (DEPTH, D_MODEL), 0.02),
        "w_ffn_gate": nrm(ks[17], (DEPTH, D_MODEL, D_FF), D_MODEL ** -0.5),
        "w_ffn_up": nrm(ks[18], (DEPTH, D_MODEL, D_FF), D_MODEL ** -0.5),
        "w_ffn_down": nrm(ks[19], (DEPTH, D_FF, D_MODEL), D_FF ** -0.5),
        "final_norm_g": 1.0 + nrm(ks[20], (D_MODEL,), 0.02),
    }


def reference(x_prompt, x_sample, cache_swa_k, cache_swa_v, norm_mix_g, w_in, gmlp_ln_g, gmlp_ln_b,
              gmlp_ws, gmlp_bs, attn_sinks, w_gate, b_gate, w_branch_a, w_branch_b, w_out,
              norm_ffn_g, w_ffn_gate, w_ffn_up, w_ffn_down, final_norm_g):
    Bp, S, _ = x_prompt.shape
    Bs, T, _ = x_sample.shape
    pos_p = jnp.arange(S, dtype=jnp.float32)
    pos_s = PAST_LEN + jnp.arange(T, dtype=jnp.float32)
    keep_p = min(WINDOW, S)
    xp, xs = x_prompt, x_sample
    kp_list, vp_list, ks_list, vs_list, gv_list = [], [], [], [], []
    for l in range(DEPTH):
        xn, u, v, q, k, va = _project(xp, norm_mix_g[l], w_in[l])
        q, k = _partial_rope(q, pos_p), _partial_rope(k, pos_p)
        vn = _layernorm(v, gmlp_ln_g[l], gmlp_ln_b[l])
        nb = S // GMLP_CHUNK
        a = _gmlp_mix(u.reshape(Bp, nb, GMLP_CHUNK, GMLP_WIDTH),
                      vn.reshape(Bp, nb, GMLP_CHUNK, GMLP_WIDTH), gmlp_ws[l], gmlp_bs[l]).reshape(Bp, S, GMLP_WIDTH)
        o = _swa_prompt(q, k, va, attn_sinks[l])
        xp = _merge_and_ffn(xp, xn, a, o, w_gate[l], b_gate[l], w_branch_a[l], w_branch_b[l], w_out[l],
                            norm_ffn_g[l], w_ffn_gate[l], w_ffn_up[l], w_ffn_down[l])
        kp_list.append(k[:, S - keep_p:])
        vp_list.append(va[:, S - keep_p:])
        xn_s, u_s, v_s, q_s, k_s, va_s = _project(xs, norm_mix_g[l], w_in[l])
        q_s, k_s = _partial_rope(q_s, pos_s), _partial_rope(k_s, pos_s)
        vn_s = _layernorm(v_s, gmlp_ln_g[l], gmlp_ln_b[l])
        a_s = _gmlp_mix(u_s[:, None], vn_s[:, None], gmlp_ws[l], gmlp_bs[l])[:, 0]
        k_all = jnp.concatenate([cache_swa_k[l].astype(k_s.dtype), k_s], axis=1)
        v_all = jnp.concatenate([cache_swa_v[l].astype(va_s.dtype), va_s], axis=1)
        o_s = _swa_sample(q_s, k_all, v_all, attn_sinks[l])
        xs = _merge_and_ffn(xs, xn_s, a_s, o_s, w_gate[l], b_gate[l], w_branch_a[l], w_branch_b[l], w_out[l],
                            norm_ffn_g[l], w_ffn_gate[l], w_ffn_up[l], w_ffn_down[l])
        ks_list.append(k_s)
        vs_list.append(va_s)
        gv_list.append(vn_s)
    y_prompt = _rmsnorm(xp, final_norm_g)
    y_sample = _rmsnorm(xs, final_norm_g)
    swa_k_prompt = jnp.stack(kp_list, axis=0)
    swa_v_prompt = jnp.stack(vp_list, axis=0)
    swa_k_sample = jnp.stack(ks_list, axis=0)
    swa_v_sample = jnp.stack(vs_list, axis=0)
    gmlp_v_sample = jnp.stack(gv_list, axis=0)
    return (y_prompt, y_sample, swa_k_prompt, swa_v_prompt, swa_k_sample, swa_v_sample, gmlp_v_sample)
```

```python
import functools

import jax
import jax.numpy as jnp
from jax import lax
from jax.experimental import pallas as pl
from jax.experimental.pallas import tpu as pltpu

D_MODEL = 2048
CHUNK = 64
GMLP_CHUNK = 128
GMLP_GROUPS = 8
GMLP_GROUP_DIM = D_MODEL // GMLP_GROUPS
N_HEADS = 32
N_KV_HEADS = 4
HEAD_DIM = 64
Q_REP = N_HEADS // N_KV_HEADS
WINDOW = 128
ROPE_THETA = 500000.0
ROT_DIM = HEAD_DIM // 4
ROT_HALF = ROT_DIM // 2
Q_WIDTH = N_HEADS * HEAD_DIM
KV_WIDTH = N_KV_HEADS * HEAD_DIM
EPS = 1e-6
NEG = -1e30
LANES = 128

F32 = jnp.float32
BF16 = jnp.bfloat16

VMEM_LIMIT = 56 * 1024 * 1024


def _params(semantics):
    return pltpu.CompilerParams(dimension_semantics=semantics, vmem_limit_bytes=VMEM_LIMIT)


def _rms(x, g):
    return x * lax.rsqrt(jnp.mean(x * x, axis=-1, keepdims=True) + EPS) * g


def _norm_kernel(n_prompt_blocks, xp_ref, xs_ref, g_ref, o_ref):
    i = pl.program_id(0)

    @pl.when(i < n_prompt_blocks)
    def _():
        o_ref[...] = _rms(xp_ref[...], g_ref[...]).astype(BF16)

    @pl.when(i >= n_prompt_blocks)
    def _():
        o_ref[...] = _rms(xs_ref[...], g_ref[...]).astype(BF16)


def _norm(xp, xs, g, tr=512):
    mp, ms = xp.shape[0], xs.shape[0]
    npb, nsb = mp // tr, ms // tr
    return pl.pallas_call(
        functools.partial(_norm_kernel, npb),
        grid=(npb + nsb,),
        in_specs=[
            pl.BlockSpec((tr, D_MODEL), lambda i: (jnp.minimum(i, npb - 1), 0)),
            pl.BlockSpec((tr, D_MODEL), lambda i: (jnp.maximum(i - npb, 0), 0)),
            pl.BlockSpec((1, D_MODEL), lambda i: (0, 0)),
        ],
        out_specs=pl.BlockSpec((tr, D_MODEL), lambda i: (i, 0)),
        out_shape=jax.ShapeDtypeStruct((mp + ms, D_MODEL), BF16),
        compiler_params=_params(("arbitrary",)),
        name="norm",
    )(xp, xs, g)


def _rope(h, cos, sa, sb):
    return h * cos + pltpu.roll(h, ROT_HALF, 1) * sa + pltpu.roll(h, LANES - ROT_HALF, 1) * sb


def _proj_kernel(nu, nq, tn, xn_ref, win_ref, wg_ref, bg_ref, cos_ref, sa_ref, sb_ref,
                 u_ref, v_ref, q_ref, k_ref, va_ref, g_ref):
    j = pl.program_id(1)
    xn = xn_ref[...]

    def mm(w_ref):
        return jnp.dot(xn, w_ref[...], preferred_element_type=F32)

    def rope_cols(h, scale):
        cos, sa, sb = cos_ref[...], sa_ref[...], sb_ref[...]
        parts = [_rope(h[:, c:c + LANES], cos, sa, sb) * scale for c in range(0, h.shape[1], LANES)]
        return jnp.concatenate(parts, axis=1)

    @pl.when(j < nu)
    def _():
        u_ref[...] = jax.nn.gelu(mm(win_ref)).astype(u_ref.dtype)

    @pl.when((j >= nu) & (j < 2 * nu))
    def _():
        v_ref[...] = jax.nn.gelu(mm(win_ref)).astype(v_ref.dtype)

    @pl.when((j >= 2 * nu) & (j < 2 * nu + nq))
    def _():
        q_ref[...] = rope_cols(mm(win_ref), HEAD_DIM ** -0.5).astype(q_ref.dtype)

    @pl.when(j == 2 * nu + nq)
    def _():
        h = mm(win_ref)
        k_ref[...] = rope_cols(h[:, :KV_WIDTH], 1.0)
        va_ref[...] = h[:, KV_WIDTH:]

    @pl.when(j > 2 * nu + nq)
    def _():
        g_ref[...] = jax.nn.sigmoid(mm(wg_ref) + bg_ref[...]).astype(g_ref.dtype)


def _proj(xn, w_in, w_gate, b_gate, cos_t, sa_t, sb_t, n_prompt_rows, seq, tm=1024, tn=512):
    m = xn.shape[0]
    nu = D_MODEL // tn
    nq = Q_WIDTH // tn
    n_in = 2 * nu + nq + 1
    ng = (2 * D_MODEL) // tn
    assert 2 * KV_WIDTH == tn
    npb = n_prompt_rows // tm
    blocks_per_seq = seq // tm

    def tab_idx(i, j):
        return (jnp.where(i < npb, i % blocks_per_seq, blocks_per_seq), 0)

    clip = lambda x, lo, hi: jnp.minimum(jnp.maximum(x, lo), hi)
    tab_spec = pl.BlockSpec((tm, LANES), tab_idx)
    out_shapes = (
        jax.ShapeDtypeStruct((m, D_MODEL), BF16),
        jax.ShapeDtypeStruct((m, D_MODEL), F32),
        jax.ShapeDtypeStruct((m, Q_WIDTH), BF16),
        jax.ShapeDtypeStruct((m, KV_WIDTH), F32),
        jax.ShapeDtypeStruct((m, KV_WIDTH), F32),
        jax.ShapeDtypeStruct((m, 2 * D_MODEL), BF16),
    )
    return pl.pallas_call(
        functools.partial(_proj_kernel, nu, nq, tn),
        grid=(m // tm, n_in + ng),
        in_specs=[
            pl.BlockSpec((tm, D_MODEL), lambda i, j: (i, 0)),
            pl.BlockSpec((D_MODEL, tn), lambda i, j: (0, jnp.minimum(j, n_in - 1))),
            pl.BlockSpec((D_MODEL, tn), lambda i, j: (0, clip(j - n_in, 0, ng - 1))),
            pl.BlockSpec((1, tn), lambda i, j: (0, clip(j - n_in, 0, ng - 1))),
            tab_spec, tab_spec, tab_spec,
        ],
        out_specs=(
            pl.BlockSpec((tm, tn), lambda i, j: (i, jnp.minimum(j, nu - 1))),
            pl.BlockSpec((tm, tn), lambda i, j: (i, clip(j - nu, 0, nu - 1))),
            pl.BlockSpec((tm, tn), lambda i, j: (i, clip(j - 2 * nu, 0, nq - 1))),
            pl.BlockSpec((tm, KV_WIDTH), lambda i, j: (i, 0)),
            pl.BlockSpec((tm, KV_WIDTH), lambda i, j: (i, 0)),
            pl.BlockSpec((tm, tn), lambda i, j: (i, clip(j - n_in, 0, ng - 1))),
        ),
        out_shape=out_shapes,
        compiler_params=_params(("arbitrary", "arbitrary")),
        name="proj",
    )(xn, w_in, w_gate, b_gate, cos_t, sa_t, sb_t)


def _gmlp_kernel(n_prompt_blocks, tr, u_ref, v_ref, lng_ref, lnb_ref, ws_ref, bst_ref,
                 a_ref, vn_ref):
    i = pl.program_id(0)
    v = v_ref[...]
    mu = jnp.mean(v, axis=-1, keepdims=True)
    vc = v - mu
    var = jnp.mean(vc * vc, axis=-1, keepdims=True)
    vn = vc * lax.rsqrt(var + EPS) * lng_ref[...] + lnb_ref[...]
    vnb = vn.astype(BF16)

    def mix(length):
        r = lax.broadcasted_iota(jnp.int32, (length, length), 0) // CHUNK
        c = lax.broadcasted_iota(jnp.int32, (length, length), 1) // CHUNK
        for g in range(GMLP_GROUPS):
            w = jnp.where(c <= r, ws_ref[g, :length, :length], 0.0).astype(BF16)
            bias = bst_ref[:length, g:g + 1]
            cols = slice(g * GMLP_GROUP_DIM, (g + 1) * GMLP_GROUP_DIM)
            for b in range(tr // length):
                rows = slice(b * length, (b + 1) * length)
                s = jnp.dot(w, vnb[rows, cols], preferred_element_type=F32) + bias
                a_ref[rows, cols] = (u_ref[rows, cols].astype(F32) * s).astype(a_ref.dtype)

    @pl.when(i < n_prompt_blocks)
    def _():
        mix(GMLP_CHUNK)

    @pl.when(i >= n_prompt_blocks)
    def _():
        vn_ref[...] = vn
        mix(CHUNK)


def _gmlp(u, v, ln_g, ln_b, ws, bs_t, n_prompt_rows, tr=256):
    m = u.shape[0]
    npb = n_prompt_rows // tr
    return pl.pallas_call(
        functools.partial(_gmlp_kernel, npb, tr),
        grid=(m // tr,),
        in_specs=[
            pl.BlockSpec((tr, D_MODEL), lambda i: (i, 0)),
            pl.BlockSpec((tr, D_MODEL), lambda i: (i, 0)),
            pl.BlockSpec((1, D_MODEL), lambda i: (0, 0)),
            pl.BlockSpec((1, D_MODEL), lambda i: (0, 0)),
            pl.BlockSpec((GMLP_GROUPS, GMLP_CHUNK, GMLP_CHUNK), lambda i: (0, 0, 0)),
            pl.BlockSpec((GMLP_CHUNK, GMLP_GROUPS), lambda i: (0, 0)),
        ],
        out_specs=(
            pl.BlockSpec((tr, D_MODEL), lambda i: (i, 0)),
            pl.BlockSpec((tr, D_MODEL), lambda i: (jnp.maximum(i - npb, 0), 0)),
        ),
        out_shape=(
            jax.ShapeDtypeStruct((m, D_MODEL), BF16),
            jax.ShapeDtypeStruct((m - n_prompt_rows, D_MODEL), F32),
        ),
        compiler_params=_params(("arbitrary",)),
        name="gmlp",
    )(u, v, ln_g, ln_b, ws, bs_t)


def _attend(q_ref, kwin, vwin, valid, sink_ref, o_ref):
    for hk in range(N_KV_HEADS):
        cols = slice(hk * HEAD_DIM, (hk + 1) * HEAD_DIM)
        kh, vh = kwin[:, cols], vwin[:, cols]
        for r in range(Q_REP):
            h = hk * Q_REP + r
            hcols = slice(h * HEAD_DIM, (h + 1) * HEAD_DIM)
            s = lax.dot_general(q_ref[:, hcols], kh, (((1,), (1,)), ((), ())),
                                preferred_element_type=F32)
            if valid is not None:
                s = jnp.where(valid, s, NEG)
            sink = sink_ref[h]
            mx = jnp.maximum(jnp.max(s, axis=-1, keepdims=True), sink)
            p = jnp.exp(s - mx)
            den = jnp.sum(p, axis=-1, keepdims=True) + jnp.exp(sink - mx)
            o = jnp.dot(p.astype(BF16), vh, preferred_element_type=F32) / den
            o_ref[:, hcols] = o.astype(o_ref.dtype)


def _attn_prompt_kernel(tq, sink_ref, q_ref, kp_ref, kc_ref, vp_ref, vc_ref, o_ref):
    t = pl.program_id(1)
    kwin = jnp.concatenate([kp_ref[...], kc_ref[...]], axis=0).astype(BF16)
    vwin = jnp.concatenate([vp_ref[...], vc_ref[...]], axis=0).astype(BF16)
    qc = lax.broadcasted_iota(jnp.int32, (tq, 2 * tq), 0) // CHUNK
    kc = lax.broadcasted_iota(jnp.int32, (tq, 2 * tq), 1) // CHUNK - WINDOW // CHUNK
    valid = (kc <= qc) & (kc >= qc - WINDOW // CHUNK) & (kc + 2 * t >= 0)
    _attend(q_ref, kwin, vwin, valid, sink_ref, o_ref)


def _attn_prompt(sinks, q, k, va, batch, seq, tq=128):
    m = q.shape[0]
    nt = seq // tq
    assert tq == WINDOW
    cur = lambda b, t: (b * nt + t, 0)
    prev = lambda b, t: (b * nt + jnp.maximum(t - 1, 0), 0)
    return pl.pallas_call(
        functools.partial(_attn_prompt_kernel, tq),
        grid=(batch, nt),
        in_specs=[
            pl.BlockSpec(memory_space=pltpu.SMEM),
            pl.BlockSpec((tq, Q_WIDTH), cur),
            pl.BlockSpec((tq, KV_WIDTH), prev),
            pl.BlockSpec((tq, KV_WIDTH), cur),
            pl.BlockSpec((tq, KV_WIDTH), prev),
            pl.BlockSpec((tq, KV_WIDTH), cur),
        ],
        out_specs=pl.BlockSpec((tq, Q_WIDTH), cur),
        out_shape=jax.ShapeDtypeStruct((m, Q_WIDTH), BF16),
        compiler_params=_params(("arbitrary", "arbitrary")),
        name="attn_prompt",
    )(sinks, q, k, k, va, va)


def _attn_sample_kernel(sink_ref, q_ref, ck_ref, kn_ref, cv_ref, vn_ref, o_in_ref, o_ref):
    del o_in_ref
    kwin = jnp.concatenate([ck_ref[...], kn_ref[...]], axis=0).astype(BF16)
    vwin = jnp.concatenate([cv_ref[...], vn_ref[...]], axis=0).astype(BF16)
    _attend(q_ref, kwin, vwin, None, sink_ref, o_ref)


def _attn_sample(sinks, q, k, va, cache_k, cache_v, o, n_prompt_rows, dec_batch, dec_seq):
    first = n_prompt_rows // dec_seq
    new = lambda b: (first + b, 0)
    cached = lambda b: (b, 0, 0)
    cache_len = cache_k.shape[1]
    return pl.pallas_call(
        _attn_sample_kernel,
        grid=(dec_batch,),
        in_specs=[
            pl.BlockSpec(memory_space=pltpu.SMEM),
            pl.BlockSpec((dec_seq, Q_WIDTH), new),
            pl.BlockSpec((None, cache_len, KV_WIDTH), cached),
            pl.BlockSpec((dec_seq, KV_WIDTH), new),
            pl.BlockSpec((None, cache_len, KV_WIDTH), cached),
            pl.BlockSpec((dec_seq, KV_WIDTH), new),
            pl.BlockSpec(memory_space=pl.ANY),
        ],
        out_specs=pl.BlockSpec((dec_seq, Q_WIDTH), new),
        out_shape=jax.ShapeDtypeStruct(o.shape, o.dtype),
        input_output_aliases={6: 0},
        compiler_params=_params(("arbitrary",)),
        name="attn_sample",
    )(sinks, q, cache_k, k, cache_v, va, o)


def _branch_kernel(a_ref, o_ref, wa_ref, wb_ref, ga_ref, gb_ref, t_ref):
    ya = jnp.dot(a_ref[...], wa_ref[...], preferred_element_type=F32)
    yb = jnp.dot(o_ref[...], wb_ref[...], preferred_element_type=F32)
    t = ga_ref[...].astype(F32) * ya + gb_ref[...].astype(F32) * yb
    t_ref[...] = t.astype(t_ref.dtype)


def _branch(a, o, wa, wb, g, tm=1024, tn=512):
    m = a.shape[0]
    nj = D_MODEL // tn
    return pl.pallas_call(
        _branch_kernel,
        grid=(m // tm, nj),
        in_specs=[
            pl.BlockSpec((tm, D_MODEL), lambda i, j: (i, 0)),
            pl.BlockSpec((tm, Q_WIDTH), lambda i, j: (i, 0)),
            pl.BlockSpec((D_MODEL, tn), lambda i, j: (0, j)),
            pl.BlockSpec((Q_WIDTH, tn), lambda i, j: (0, j)),
            pl.BlockSpec((tm, tn), lambda i, j: (i, j)),
            pl.BlockSpec((tm, tn), lambda i, j: (i, j + nj)),
        ],
        out_specs=pl.BlockSpec((tm, tn), lambda i, j: (i, j)),
        out_shape=jax.ShapeDtypeStruct((m, D_MODEL), BF16),
        compiler_params=_params(("arbitrary", "arbitrary")),
        name="branch",
    )(a, o, wa, wb, g, g)


def _out_kernel(n_prompt_blocks, t_ref, w_ref, xp_ref, xs_ref, x1_ref):
    i = pl.program_id(0)
    y = jnp.dot(t_ref[...], w_ref[...], preferred_element_type=F32)

    @pl.when(i < n_prompt_blocks)
    def _():
        x1_ref[...] = xp_ref[...] + y

    @pl.when(i >= n_prompt_blocks)
    def _():
        x1_ref[...] = xs_ref[...] + y


def _out(t, w_out, xp, xs, tm=1024, tn=512):
    m = t.shape[0]
    npb = xp.shape[0] // tm
    return pl.pallas_call(
        functools.partial(_out_kernel, npb),
        grid=(m // tm, D_MODEL // tn),
        in_specs=[
            pl.BlockSpec((tm, D_MODEL), lambda i, j: (i, 0)),
            pl.BlockSpec((D_MODEL, tn), lambda i, j: (0, j)),
            pl.BlockSpec((tm, tn), lambda i, j: (jnp.minimum(i, npb - 1), j)),
            pl.BlockSpec((tm, tn), lambda i, j: (jnp.maximum(i - npb, 0), j)),
        ],
        out_specs=pl.BlockSpec((tm, tn), lambda i, j: (i, j)),
        out_shape=jax.ShapeDtypeStruct((m, D_MODEL), F32),
        compiler_params=_params(("arbitrary", "arbitrary")),
        name="out_proj",
    )(t, w_out, xp, xs)


def _ffn_kernel(x1_ref, ng_ref, wg_ref, wu_ref, wd_ref, fg_ref, y_ref, h_ref):
    f = pl.program_id(1)

    @pl.when(f == 0)
    def _():
        x1 = x1_ref[...]
        h_ref[...] = _rms(x1, ng_ref[...]).astype(h_ref.dtype)
        y_ref[...] = x1

    h = h_ref[...]
    gate = jnp.dot(h, wg_ref[...], preferred_element_type=F32)
    up = jnp.dot(h, wu_ref[...], preferred_element_type=F32)
    hid = (jax.nn.silu(gate) * up).astype(BF16)
    y_ref[...] += jnp.dot(hid, wd_ref[...], preferred_element_type=F32)

    @pl.when(f == pl.num_programs(1) - 1)
    def _():
        y_ref[...] = _rms(y_ref[...], fg_ref[...])


def _ffn(x1, row_block_offset, n_row_blocks, norm_g, wg, wu, wd, final_g, tm=512, tf=512):
    d_ff = wg.shape[1]
    return pl.pallas_call(
        _ffn_kernel,
        grid=(n_row_blocks, d_ff // tf),
        in_specs=[
            pl.BlockSpec((tm, D_MODEL), lambda i, f: (i + row_block_offset, 0)),
            pl.BlockSpec((1, D_MODEL), lambda i, f: (0, 0)),
            pl.BlockSpec((D_MODEL, tf), lambda i, f: (0, f)),
            pl.BlockSpec((D_MODEL, tf), lambda i, f: (0, f)),
            pl.BlockSpec((tf, D_MODEL), lambda i, f: (f, 0)),
            pl.BlockSpec((1, D_MODEL), lambda i, f: (0, 0)),
        ],
        out_specs=pl.BlockSpec((tm, D_MODEL), lambda i, f: (i, 0)),
        out_shape=jax.ShapeDtypeStruct((n_row_blocks * tm, D_MODEL), F32),
        scratch_shapes=[pltpu.VMEM((tm, D_MODEL), BF16)],
        compiler_params=_params(("arbitrary", "arbitrary")),
        name="ffn",
    )(x1, norm_g, wg, wu, wd, final_g)


def _rope_tables(seq, past_len, dec_batch, dec_seq):
    pos = jnp.concatenate([jnp.arange(seq, dtype=F32),
                           jnp.tile(past_len + jnp.arange(dec_seq, dtype=F32), dec_batch)])
    inv_freq = jnp.float32(ROPE_THETA) ** (-(jnp.arange(ROT_HALF, dtype=F32) * 2.0 / ROT_DIM))
    ang = pos[:, None] * inv_freq[None, :]
    cos, sin = jnp.cos(ang), jnp.sin(ang)
    n = pos.shape[0]
    pad = jnp.zeros((n, HEAD_DIM - ROT_DIM), F32)
    zero = jnp.zeros((n, ROT_HALF), F32)
    cos_h = jnp.concatenate([cos, cos, pad + 1.0], axis=1)
    sa_h = jnp.concatenate([zero, sin, pad], axis=1)
    sb_h = jnp.concatenate([-sin, zero, pad], axis=1)
    rep = LANES // HEAD_DIM
    return jnp.tile(cos_h, (1, rep)), jnp.tile(sa_h, (1, rep)), jnp.tile(sb_h, (1, rep))


def kernel(x_prompt, x_sample, cache_swa_k, cache_swa_v, norm_mix_g, w_in, gmlp_ln_g, gmlp_ln_b,
           gmlp_ws, gmlp_bs, attn_sinks, w_gate, b_gate, w_branch_a, w_branch_b, w_out,
           norm_ffn_g, w_ffn_gate, w_ffn_up, w_ffn_down, final_norm_g):
    batch, seq, _ = x_prompt.shape
    dec_batch, dec_seq, _ = x_sample.shape
    depth = w_in.shape[0]
    past_len = 2048
    assert depth == 1 and cache_swa_k.shape[2] == WINDOW and dec_seq == CHUNK
    mp, ms = batch * seq, dec_batch * dec_seq

    xp = x_prompt.reshape(mp, D_MODEL)
    xs = x_sample.reshape(ms, D_MODEL)
    cos_t, sa_t, sb_t = _rope_tables(seq, past_len, dec_batch, dec_seq)
    row = lambda p: p.reshape(1, -1)

    l = 0
    xn = _norm(xp, xs, row(norm_mix_g[l]))
    u, v, q, k, va, g = _proj(xn, w_in[l].astype(BF16), w_gate[l].astype(BF16), row(b_gate[l]),
                              cos_t, sa_t, sb_t, mp, seq)
    a, vn_s = _gmlp(u, v, row(gmlp_ln_g[l]), row(gmlp_ln_b[l]), gmlp_ws[l], gmlp_bs[l].T, mp)
    o = _attn_prompt(attn_sinks[l], q, k, va, batch, seq)
    o = _attn_sample(attn_sinks[l], q, k, va,
                     cache_swa_k[l].reshape(dec_batch, WINDOW, KV_WIDTH),
                     cache_swa_v[l].reshape(dec_batch, WINDOW, KV_WIDTH),
                     o, mp, dec_batch, dec_seq)
    t = _branch(a, o, w_branch_a[l].astype(BF16), w_branch_b[l].astype(BF16), g)
    x1 = _out(t, w_out[l].astype(BF16), xp, xs)
    wg, wu, wd = w_ffn_gate[l].astype(BF16), w_ffn_up[l].astype(BF16), w_ffn_down[l].astype(BF16)
    ffn_tm = 512
    ffn = functools.partial(_ffn, norm_g=row(norm_ffn_g[l]), wg=wg, wu=wu, wd=wd,
                            final_g=row(final_norm_g), tm=ffn_tm)
    y_prompt = ffn(x1, 0, mp // ffn_tm)
    y_sample = ffn(x1, mp // ffn_tm, ms // ffn_tm)

    keep = min(WINDOW, seq)
    kp = k[:mp].reshape(batch, seq, N_KV_HEADS, HEAD_DIM)[:, seq - keep:]
    vp = va[:mp].reshape(batch, seq, N_KV_HEADS, HEAD_DIM)[:, seq - keep:]
    return (
        y_prompt.reshape(batch, seq, D_MODEL),
        y_sample.reshape(dec_batch, dec_seq, D_MODEL),
        kp[None],
        vp[None],
        k[mp:].reshape(1, dec_batch, dec_seq, N_KV_HEADS, HEAD_DIM),
        va[mp:].reshape(1, dec_batch, dec_seq, N_KV_HEADS, HEAD_DIM),
        vn_s.reshape(1, dec_batch, dec_seq, D_MODEL),
    )
```

```python
import functools

import jax
import jax.numpy as jnp
from jax import lax
from jax.experimental import pallas as pl
from jax.experimental.pallas import tpu as pltpu

D_MODEL = 2048
CHUNK = 64
GMLP_CHUNK = 128
GMLP_GROUPS = 8
GMLP_GROUP_DIM = D_MODEL // GMLP_GROUPS
N_HEADS = 32
N_KV_HEADS = 4
HEAD_DIM = 64
Q_REP = N_HEADS // N_KV_HEADS
WINDOW = 128
ROPE_THETA = 500000.0
ROT_DIM = HEAD_DIM // 4
ROT_HALF = ROT_DIM // 2
Q_WIDTH = N_HEADS * HEAD_DIM
KV_WIDTH = N_KV_HEADS * HEAD_DIM
EPS = 1e-6
NEG = -1e30
LANES = 128

F32 = jnp.float32
BF16 = jnp.bfloat16

VMEM_LIMIT = 56 * 1024 * 1024


def _params(semantics):
    return pltpu.CompilerParams(dimension_semantics=semantics, vmem_limit_bytes=VMEM_LIMIT)


def _rms(x, g):
    return x * lax.rsqrt(jnp.mean(x * x, axis=-1, keepdims=True) + EPS) * g


def _norm_kernel(n_prompt_blocks, xp_ref, xs_ref, g_ref, o_ref):
    i = pl.program_id(0)

    @pl.when(i < n_prompt_blocks)
    def _():
        o_ref[...] = _rms(xp_ref[...], g_ref[...]).astype(BF16)

    @pl.when(i >= n_prompt_blocks)
    def _():
        o_ref[...] = _rms(xs_ref[...], g_ref[...]).astype(BF16)


def _norm(xp, xs, g, tr=512):
    mp, ms = xp.shape[0], xs.shape[0]
    npb, nsb = mp // tr, ms // tr
    return pl.pallas_call(
        functools.partial(_norm_kernel, npb),
        grid=(npb + nsb,),
        in_specs=[
            pl.BlockSpec((tr, D_MODEL), lambda i: (jnp.minimum(i, npb - 1), 0)),
            pl.BlockSpec((tr, D_MODEL), lambda i: (jnp.maximum(i - npb, 0), 0)),
            pl.BlockSpec((1, D_MODEL), lambda i: (0, 0)),
        ],
        out_specs=pl.BlockSpec((tr, D_MODEL), lambda i: (i, 0)),
        out_shape=jax.ShapeDtypeStruct((mp + ms, D_MODEL), BF16),
        compiler_params=_params(("arbitrary",)),
        name="norm",
    )(xp, xs, g)


def _rope(h, cos, sa, sb):
    return h * cos + pltpu.roll(h, ROT_HALF, 1) * sa + pltpu.roll(h, LANES - ROT_HALF, 1) * sb


def _proj_in_kernel(nu, nq, xn_ref, win_ref, wkv_ref, cos_ref, sa_ref, sb_ref,
                    u_ref, v_ref, q_ref, k_ref, va_ref):
    j = pl.program_id(1)
    xn = xn_ref[...]

    def mm(w_ref):
        return jnp.dot(xn, w_ref[...], preferred_element_type=F32)

    def rope_cols(h, scale):
        cos, sa, sb = cos_ref[...], sa_ref[...], sb_ref[...]
        parts = [_rope(h[:, c:c + LANES], cos, sa, sb) * scale for c in range(0, h.shape[1], LANES)]
        return jnp.concatenate(parts, axis=1)

    @pl.when(j < nu)
    def _():
        u_ref[...] = jax.nn.gelu(mm(win_ref)).astype(u_ref.dtype)

    @pl.when((j >= nu) & (j < 2 * nu))
    def _():
        v_ref[...] = jax.nn.gelu(mm(win_ref)).astype(v_ref.dtype)

    @pl.when((j >= 2 * nu) & (j < 2 * nu + nq))
    def _():
        q_ref[...] = rope_cols(mm(win_ref), HEAD_DIM ** -0.5).astype(q_ref.dtype)

    @pl.when(j == 2 * nu + nq)
    def _():
        h = mm(wkv_ref)
        k_ref[...] = rope_cols(h[:, :KV_WIDTH], 1.0)
        va_ref[...] = h[:, KV_WIDTH:]


def _proj_in(xn, w_in, cos_t, sa_t, sb_t, n_prompt_rows, seq, tm=1024, tn=1024):
    m = xn.shape[0]
    nu = D_MODEL // tn
    nq = Q_WIDTH // tn
    n_wide = 2 * nu + nq
    kv_block = (2 * D_MODEL + Q_WIDTH) // (2 * KV_WIDTH)
    npb = n_prompt_rows // tm
    blocks_per_seq = seq // tm

    def tab_idx(i, j):
        return (jnp.where(i < npb, i % blocks_per_seq, blocks_per_seq), 0)

    clip = lambda x, lo, hi: jnp.minimum(jnp.maximum(x, lo), hi)
    tab_spec = pl.BlockSpec((tm, LANES), tab_idx)
    out_shapes = (
        jax.ShapeDtypeStruct((m, D_MODEL), BF16),
        jax.ShapeDtypeStruct((m, D_MODEL), F32),
        jax.ShapeDtypeStruct((m, Q_WIDTH), BF16),
        jax.ShapeDtypeStruct((m, KV_WIDTH), F32),
        jax.ShapeDtypeStruct((m, KV_WIDTH), F32),
    )
    return pl.pallas_call(
        functools.partial(_proj_in_kernel, nu, nq),
        grid=(m // tm, n_wide + 1),
        in_specs=[
            pl.BlockSpec((tm, D_MODEL), lambda i, j: (i, 0)),
            pl.BlockSpec((D_MODEL, tn), lambda i, j: (0, jnp.minimum(j, n_wide - 1))),
            pl.BlockSpec((D_MODEL, 2 * KV_WIDTH), lambda i, j: (0, kv_block)),
            tab_spec, tab_spec, tab_spec,
        ],
        out_specs=(
            pl.BlockSpec((tm, tn), lambda i, j: (i, jnp.minimum(j, nu - 1))),
            pl.BlockSpec((tm, tn), lambda i, j: (i, clip(j - nu, 0, nu - 1))),
            pl.BlockSpec((tm, tn), lambda i, j: (i, clip(j - 2 * nu, 0, nq - 1))),
            pl.BlockSpec((tm, KV_WIDTH), lambda i, j: (i, 0)),
            pl.BlockSpec((tm, KV_WIDTH), lambda i, j: (i, 0)),
        ),
        out_shape=out_shapes,
        compiler_params=_params(("arbitrary", "arbitrary")),
        name="proj_in",
    )(xn, w_in, w_in, cos_t, sa_t, sb_t)


def _proj_gate_kernel(xn_ref, w_ref, b_ref, g_ref):
    h = jnp.dot(xn_ref[...], w_ref[...], preferred_element_type=F32)
    g_ref[...] = jax.nn.sigmoid(h + b_ref[...]).astype(g_ref.dtype)


def _proj_gate(xn, w_gate, b_gate, tm=1024, tn=1024):
    m = xn.shape[0]
    n = w_gate.shape[1]
    return pl.pallas_call(
        _proj_gate_kernel,
        grid=(m // tm, n // tn),
        in_specs=[
            pl.BlockSpec((tm, D_MODEL), lambda i, j: (i, 0)),
            pl.BlockSpec((D_MODEL, tn), lambda i, j: (0, j)),
            pl.BlockSpec((1, tn), lambda i, j: (0, j)),
        ],
        out_specs=pl.BlockSpec((tm, tn), lambda i, j: (i, j)),
        out_shape=jax.ShapeDtypeStruct((m, n), BF16),
        compiler_params=_params(("arbitrary", "arbitrary")),
        name="proj_gate",
    )(xn, w_gate, b_gate)


def _gmlp_kernel(n_prompt_blocks, tr, u_ref, v_ref, lng_ref, lnb_ref, ws_ref, bst_ref,
                 a_ref, vn_ref):
    i = pl.program_id(0)
    v = v_ref[...]
    mu = jnp.mean(v, axis=-1, keepdims=True)
    vc = v - mu
    var = jnp.mean(vc * vc, axis=-1, keepdims=True)
    vn = vc * lax.rsqrt(var + EPS) * lng_ref[...] + lnb_ref[...]
    vnb = vn.astype(BF16)

    def mix(length):
        r = lax.broadcasted_iota(jnp.int32, (length, length), 0) // CHUNK
        c = lax.broadcasted_iota(jnp.int32, (length, length), 1) // CHUNK
        for g in range(GMLP_GROUPS):
            w = jnp.where(c <= r, ws_ref[g, :length, :length], 0.0).astype(BF16)
            bias = bst_ref[:length, g:g + 1]
            cols = slice(g * GMLP_GROUP_DIM, (g + 1) * GMLP_GROUP_DIM)
            for b in range(tr // length):
                rows = slice(b * length, (b + 1) * length)
                s = jnp.dot(w, vnb[rows, cols], preferred_element_type=F32) + bias
                a_ref[rows, cols] = (u_ref[rows, cols].astype(F32) * s).astype(a_ref.dtype)

    @pl.when(i < n_prompt_blocks)
    def _():
        mix(GMLP_CHUNK)

    @pl.when(i >= n_prompt_blocks)
    def _():
        vn_ref[...] = vn
        mix(CHUNK)


def _gmlp(u, v, ln_g, ln_b, ws, bs_t, n_prompt_rows, tr=256):
    m = u.shape[0]
    npb = n_prompt_rows // tr
    return pl.pallas_call(
        functools.partial(_gmlp_kernel, npb, tr),
        grid=(m // tr,),
        in_specs=[
            pl.BlockSpec((tr, D_MODEL), lambda i: (i, 0)),
            pl.BlockSpec((tr, D_MODEL), lambda i: (i, 0)),
            pl.BlockSpec((1, D_MODEL), lambda i: (0, 0)),
            pl.BlockSpec((1, D_MODEL), lambda i: (0, 0)),
            pl.BlockSpec((GMLP_GROUPS, GMLP_CHUNK, GMLP_CHUNK), lambda i: (0, 0, 0)),
            pl.BlockSpec((GMLP_CHUNK, GMLP_GROUPS), lambda i: (0, 0)),
        ],
        out_specs=(
            pl.BlockSpec((tr, D_MODEL), lambda i: (i, 0)),
            pl.BlockSpec((tr, D_MODEL), lambda i: (jnp.maximum(i - npb, 0), 0)),
        ),
        out_shape=(
            jax.ShapeDtypeStruct((m, D_MODEL), BF16),
            jax.ShapeDtypeStruct((m - n_prompt_rows, D_MODEL), F32),
        ),
        compiler_params=_params(("arbitrary",)),
        name="gmlp",
    )(u, v, ln_g, ln_b, ws, bs_t)


def _dup_head(pair, pair_swapped, low_half, odd):
    if odd:
        return jnp.where(low_half, pair_swapped, pair).astype(BF16)
    return jnp.where(low_half, pair, pair_swapped).astype(BF16)


def _attend(q_ref, kwin, vwin, valid, sink_ref, o_ref):
    rows, keys = q_ref.shape[0], kwin.shape[0]
    low_q = lax.broadcasted_iota(jnp.int32, (rows, LANES), 1) < HEAD_DIM
    low_k = lax.broadcasted_iota(jnp.int32, (keys, LANES), 1) < HEAD_DIM
    heads_per_pair = LANES // HEAD_DIM
    for c in range(N_KV_HEADS // heads_per_pair):
        kpair = kwin[:, c * LANES:(c + 1) * LANES]
        vpair = vwin[:, c * LANES:(c + 1) * LANES]
        kswap = pltpu.roll(kpair, HEAD_DIM, 1)
        vswap = pltpu.roll(vpair, HEAD_DIM, 1)
        for odd in range(heads_per_pair):
            hk = c * heads_per_pair + odd
            kdup = _dup_head(kpair, kswap, low_k, odd)
            vdup = _dup_head(vpair, vswap, low_k, odd)
            parts = []
            for r in range(Q_REP):
                h = hk * Q_REP + r
                qpair = q_ref[:, (h // 2) * LANES:(h // 2 + 1) * LANES]
                keep = low_q if h % 2 == 0 else jnp.logical_not(low_q)
                parts.append(jnp.where(keep, qpair, jnp.zeros_like(qpair)))
            s_all = lax.dot_general(jnp.concatenate(parts, axis=0), kdup, (((1,), (1,)), ((), ())),
                                    preferred_element_type=F32)
            ps, dens = [], []
            for r in range(Q_REP):
                s = s_all[r * rows:(r + 1) * rows]
                if valid is not None:
                    s = jnp.where(valid, s, NEG)
                sink = sink_ref[hk * Q_REP + r]
                mx = jnp.maximum(jnp.max(s, axis=-1, keepdims=True), sink)
                p = jnp.exp(s - mx)
                dens.append(jnp.sum(p, axis=-1, keepdims=True) + jnp.exp(sink - mx))
                ps.append(p.astype(BF16))
            o_all = jnp.dot(jnp.concatenate(ps, axis=0), vdup, preferred_element_type=F32)
            for r in range(0, Q_REP, 2):
                h = hk * Q_REP + r
                even = o_all[r * rows:(r + 1) * rows] / dens[r]
                oddh = o_all[(r + 1) * rows:(r + 2) * rows] / dens[r + 1]
                o_ref[:, (h // 2) * LANES:(h // 2 + 1) * LANES] = (
                    jnp.where(low_q, even, oddh).astype(o_ref.dtype))


def _attn_prompt_kernel(tq, sink_ref, q_ref, kp_ref, kc_ref, vp_ref, vc_ref, o_ref):
    t = pl.program_id(1)
    kwin = jnp.concatenate([kp_ref[...], kc_ref[...]], axis=0)
    vwin = jnp.concatenate([vp_ref[...], vc_ref[...]], axis=0)
    qc = lax.broadcasted_iota(jnp.int32, (tq, 2 * tq), 0) // CHUNK
    kc = lax.broadcasted_iota(jnp.int32, (tq, 2 * tq), 1) // CHUNK - WINDOW // CHUNK
    valid = (kc <= qc) & (kc >= qc - WINDOW // CHUNK) & (kc + 2 * t >= 0)
    _attend(q_ref, kwin, vwin, valid, sink_ref, o_ref)


def _attn_prompt(sinks, q, k, va, batch, seq, tq=128):
    m = q.shape[0]
    nt = seq // tq
    assert tq == WINDOW
    cur = lambda b, t: (b * nt + t, 0)
    prev = lambda b, t: (b * nt + jnp.maximum(t - 1, 0), 0)
    return pl.pallas_call(
        functools.partial(_attn_prompt_kernel, tq),
        grid=(batch, nt),
        in_specs=[
            pl.BlockSpec(memory_space=pltpu.SMEM),
            pl.BlockSpec((tq, Q_WIDTH), cur),
            pl.BlockSpec((tq, KV_WIDTH), prev),
            pl.BlockSpec((tq, KV_WIDTH), cur),
            pl.BlockSpec((tq, KV_WIDTH), prev),
            pl.BlockSpec((tq, KV_WIDTH), cur),
        ],
        out_specs=pl.BlockSpec((tq, Q_WIDTH), cur),
        out_shape=jax.ShapeDtypeStruct((m, Q_WIDTH), BF16),
        compiler_params=_params(("arbitrary", "arbitrary")),
        name="attn_prompt",
    )(sinks, q, k, k, va, va)


def _attn_sample_kernel(sink_ref, q_ref, ck_ref, kn_ref, cv_ref, vn_ref, o_in_ref, o_ref):
    del o_in_ref
    kwin = jnp.concatenate([ck_ref[...], kn_ref[...]], axis=0)
    vwin = jnp.concatenate([cv_ref[...], vn_ref[...]], axis=0)
    _attend(q_ref, kwin, vwin, None, sink_ref, o_ref)


def _attn_sample(sinks, q, k, va, cache_k, cache_v, o, n_prompt_rows, dec_batch, dec_seq):
    first = n_prompt_rows // dec_seq
    new = lambda b: (first + b, 0)
    cached = lambda b: (b, 0, 0)
    cache_len = cache_k.shape[1]
    return pl.pallas_call(
        _attn_sample_kernel,
        grid=(dec_batch,),
        in_specs=[
            pl.BlockSpec(memory_space=pltpu.SMEM),
            pl.BlockSpec((dec_seq, Q_WIDTH), new),
            pl.BlockSpec((None, cache_len, KV_WIDTH), cached),
            pl.BlockSpec((dec_seq, KV_WIDTH), new),
            pl.BlockSpec((None, cache_len, KV_WIDTH), cached),
            pl.BlockSpec((dec_seq, KV_WIDTH), new),
            pl.BlockSpec(memory_space=pl.ANY),
        ],
        out_specs=pl.BlockSpec((dec_seq, Q_WIDTH), new),
        out_shape=jax.ShapeDtypeStruct(o.shape, o.dtype),
        input_output_aliases={6: 0},
        compiler_params=_params(("arbitrary",)),
        name="attn_sample",
    )(sinks, q, cache_k, k, cache_v, va, o)


def _branch_kernel(a_ref, o_ref, wa_ref, wb_ref, ga_ref, gb_ref, t_ref):
    ya = jnp.dot(a_ref[...], wa_ref[...], preferred_element_type=F32)
    yb = jnp.dot(o_ref[...], wb_ref[...], preferred_element_type=F32)
    t = ga_ref[...].astype(F32) * ya + gb_ref[...].astype(F32) * yb
    t_ref[...] = t.astype(t_ref.dtype)


def _branch(a, o, wa, wb, g, tm=1024, tn=512):
    m = a.shape[0]
    nj = D_MODEL // tn
    return pl.pallas_call(
        _branch_kernel,
        grid=(m // tm, nj),
        in_specs=[
            pl.BlockSpec((tm, D_MODEL), lambda i, j: (i, 0)),
            pl.BlockSpec((tm, Q_WIDTH), lambda i, j: (i, 0)),
            pl.BlockSpec((D_MODEL, tn), lambda i, j: (0, j)),
            pl.BlockSpec((Q_WIDTH, tn), lambda i, j: (0, j)),
            pl.BlockSpec((tm, tn), lambda i, j: (i, j)),
            pl.BlockSpec((tm, tn), lambda i, j: (i, j + nj)),
        ],
        out_specs=pl.BlockSpec((tm, tn), lambda i, j: (i, j)),
        out_shape=jax.ShapeDtypeStruct((m, D_MODEL), BF16),
        compiler_params=_params(("arbitrary", "arbitrary")),
        name="branch",
    )(a, o, wa, wb, g, g)


def _out_kernel(n_prompt_blocks, t_ref, w_ref, xp_ref, xs_ref, x1_ref):
    i = pl.program_id(0)
    y = jnp.dot(t_ref[...], w_ref[...], preferred_element_type=F32)

    @pl.when(i < n_prompt_blocks)
    def _():
        x1_ref[...] = xp_ref[...] + y

    @pl.when(i >= n_prompt_blocks)
    def _():
        x1_ref[...] = xs_ref[...] + y


def _out(t, w_out, xp, xs, tm=1024, tn=1024):
    m = t.shape[0]
    npb = xp.shape[0] // tm
    return pl.pallas_call(
        functools.partial(_out_kernel, npb),
        grid=(m // tm, D_MODEL // tn),
        in_specs=[
            pl.BlockSpec((tm, D_MODEL), lambda i, j: (i, 0)),
            pl.BlockSpec((D_MODEL, tn), lambda i, j: (0, j)),
            pl.BlockSpec((tm, tn), lambda i, j: (jnp.minimum(i, npb - 1), j)),
            pl.BlockSpec((tm, tn), lambda i, j: (jnp.maximum(i - npb, 0), j)),
        ],
        out_specs=pl.BlockSpec((tm, tn), lambda i, j: (i, j)),
        out_shape=jax.ShapeDtypeStruct((m, D_MODEL), F32),
        compiler_params=_params(("arbitrary", "arbitrary")),
        name="out_proj",
    )(t, w_out, xp, xs)


def _ffn_kernel(x1_ref, ng_ref, wg_ref, wu_ref, wd_ref, fg_ref, y_ref, h_ref):
    f = pl.program_id(1)

    @pl.when(f == 0)
    def _():
        x1 = x1_ref[...]
        h_ref[...] = _rms(x1, ng_ref[...]).astype(h_ref.dtype)
        y_ref[...] = x1

    h = h_ref[...]
    gate = jnp.dot(h, wg_ref[...], preferred_element_type=F32)
    up = jnp.dot(h, wu_ref[...], preferred_element_type=F32)
    hid = (jax.nn.silu(gate) * up).astype(BF16)
    y_ref[...] += jnp.dot(hid, wd_ref[...], preferred_element_type=F32)

    @pl.when(f == pl.num_programs(1) - 1)
    def _():
        y_ref[...] = _rms(y_ref[...], fg_ref[...])


def _ffn(x1, row_block_offset, n_row_blocks, norm_g, wg, wu, wd, final_g, tm=512, tf=512):
    d_ff = wg.shape[1]
    return pl.pallas_call(
        _ffn_kernel,
        grid=(n_row_blocks, d_ff // tf),
        in_specs=[
            pl.BlockSpec((tm, D_MODEL), lambda i, f: (i + row_block_offset, 0)),
            pl.BlockSpec((1, D_MODEL), lambda i, f: (0, 0)),
            pl.BlockSpec((D_MODEL, tf), lambda i, f: (0, f)),
            pl.BlockSpec((D_MODEL, tf), lambda i, f: (0, f)),
            pl.BlockSpec((tf, D_MODEL), lambda i, f: (f, 0)),
            pl.BlockSpec((1, D_MODEL), lambda i, f: (0, 0)),
        ],
        out_specs=pl.BlockSpec((tm, D_MODEL), lambda i, f: (i, 0)),
        out_shape=jax.ShapeDtypeStruct((n_row_blocks * tm, D_MODEL), F32),
        scratch_shapes=[pltpu.VMEM((tm, D_MODEL), BF16)],
        compiler_params=_params(("arbitrary", "arbitrary")),
        name="ffn",
    )(x1, norm_g, wg, wu, wd, final_g)


def _rope_tables(seq, past_len, dec_batch, dec_seq):
    pos = jnp.concatenate([jnp.arange(seq, dtype=F32),
                           jnp.tile(past_len + jnp.arange(dec_seq, dtype=F32), dec_batch)])
    inv_freq = jnp.float32(ROPE_THETA) ** (-(jnp.arange(ROT_HALF, dtype=F32) * 2.0 / ROT_DIM))
    ang = pos[:, None] * inv_freq[None, :]
    cos, sin = jnp.cos(ang), jnp.sin(ang)
    n = pos.shape[0]
    pad = jnp.zeros((n, HEAD_DIM - ROT_DIM), F32)
    zero = jnp.zeros((n, ROT_HALF), F32)
    cos_h = jnp.concatenate([cos, cos, pad + 1.0], axis=1)
    sa_h = jnp.concatenate([zero, sin, pad], axis=1)
    sb_h = jnp.concatenate([-sin, zero, pad], axis=1)
    rep = LANES // HEAD_DIM
    return jnp.tile(cos_h, (1, rep)), jnp.tile(sa_h, (1, rep)), jnp.tile(sb_h, (1, rep))


def kernel(x_prompt, x_sample, cache_swa_k, cache_swa_v, norm_mix_g, w_in, gmlp_ln_g, gmlp_ln_b,
           gmlp_ws, gmlp_bs, attn_sinks, w_gate, b_gate, w_branch_a, w_branch_b, w_out,
           norm_ffn_g, w_ffn_gate, w_ffn_up, w_ffn_down, final_norm_g):
    batch, seq, _ = x_prompt.shape
    dec_batch, dec_seq, _ = x_sample.shape
    depth = w_in.shape[0]
    past_len = 2048
    assert depth == 1 and cache_swa_k.shape[2] == WINDOW and dec_seq == CHUNK
    mp, ms = batch * seq, dec_batch * dec_seq

    xp = x_prompt.reshape(mp, D_MODEL)
    xs = x_sample.reshape(ms, D_MODEL)
    cos_t, sa_t, sb_t = _rope_tables(seq, past_len, dec_batch, dec_seq)
    row = lambda p: p.reshape(1, -1)

    l = 0
    xn = _norm(xp, xs, row(norm_mix_g[l]))
    u, v, q, k, va = _proj_in(xn, w_in[l].astype(BF16), cos_t, sa_t, sb_t, mp, seq)
    g = _proj_gate(xn, w_gate[l].astype(BF16), row(b_gate[l]))
    a, vn_s = _gmlp(u, v, row(gmlp_ln_g[l]), row(gmlp_ln_b[l]), gmlp_ws[l], gmlp_bs[l].T, mp)
    o = _attn_prompt(attn_sinks[l], q, k, va, batch, seq)
    o = _attn_sample(attn_sinks[l], q, k, va,
                     cache_swa_k[l].reshape(dec_batch, WINDOW, KV_WIDTH),
                     cache_swa_v[l].reshape(dec_batch, WINDOW, KV_WIDTH),
                     o, mp, dec_batch, dec_seq)
    t = _branch(a, o, w_branch_a[l].astype(BF16), w_branch_b[l].astype(BF16), g)
    x1 = _out(t, w_out[l].astype(BF16), xp, xs)
    wg, wu, wd = w_ffn_gate[l].astype(BF16), w_ffn_up[l].astype(BF16), w_ffn_down[l].astype(BF16)
    ffn_tm = 512
    ffn = functools.partial(_ffn, norm_g=row(norm_ffn_g[l]), wg=wg, wu=wu, wd=wd,
                            final_g=row(final_norm_g), tm=ffn_tm)
    y_prompt = ffn(x1, 0, mp // ffn_tm)
    y_sample = ffn(x1, mp // ffn_tm, ms // ffn_tm)

    keep = min(WINDOW, seq)
    kp = k[:mp].reshape(batch, seq, N_KV_HEADS, HEAD_DIM)[:, seq - keep:]
    vp = va[:mp].reshape(batch, seq, N_KV_HEADS, HEAD_DIM)[:, seq - keep:]
    return (
        y_prompt.reshape(batch, seq, D_MODEL),
        y_sample.reshape(dec_batch, dec_seq, D_MODEL),
        kp[None],
        vp[None],
        k[mp:].reshape(1, dec_batch, dec_seq, N_KV_HEADS, HEAD_DIM),
        va[mp:].reshape(1, dec_batch, dec_seq, N_KV_HEADS, HEAD_DIM),
        vn_s.reshape(1, dec_batch, dec_seq, D_MODEL),
    )
```

```python
import functools

import jax
import jax.numpy as jnp
from jax import lax
from jax.experimental import pallas as pl
from jax.experimental.pallas import tpu as pltpu

D_MODEL = 2048
CHUNK = 64
GMLP_CHUNK = 128
GMLP_GROUPS = 8
GMLP_GROUP_DIM = D_MODEL // GMLP_GROUPS
N_HEADS = 32
N_KV_HEADS = 4
HEAD_DIM = 64
Q_REP = N_HEADS // N_KV_HEADS
WINDOW = 128
ROPE_THETA = 500000.0
ROT_DIM = HEAD_DIM // 4
ROT_HALF = ROT_DIM // 2
Q_WIDTH = N_HEADS * HEAD_DIM
KV_WIDTH = N_KV_HEADS * HEAD_DIM
EPS = 1e-6
NEG = -1e30
LANES = 128

F32 = jnp.float32
BF16 = jnp.bfloat16

VMEM_LIMIT = 56 * 1024 * 1024


def _params(semantics):
    return pltpu.CompilerParams(dimension_semantics=semantics, vmem_limit_bytes=VMEM_LIMIT)


def _rms(x, g):
    return x * lax.rsqrt(jnp.mean(x * x, axis=-1, keepdims=True) + EPS) * g


BF16_SUBLANES = 16


def _rider_specs(weights, n_steps, step_of):
    counts, in_specs, out_specs, out_shapes = [], [], [], []
    for w in weights:
        rows, cols = w.shape
        nb = max(n for n in range(1, n_steps + 1)
                 if rows % n == 0 and (rows // n) % BF16_SUBLANES == 0)
        idx = lambda *g, nb=nb: (jnp.minimum(step_of(*g), nb - 1), 0)
        counts.append(nb)
        in_specs.append(pl.BlockSpec((rows // nb, cols), idx))
        out_specs.append(pl.BlockSpec((rows // nb, cols), idx))
        out_shapes.append(jax.ShapeDtypeStruct(w.shape, BF16))
    return counts, in_specs, out_specs, out_shapes


def _rider_cast(step, counts, src_refs, dst_refs):
    for nb, src, dst in zip(counts, src_refs, dst_refs):
        @pl.when(step < nb)
        def _():
            dst[...] = src[...].astype(dst.dtype)


def _norm_kernel(n_prompt_blocks, xp_ref, xs_ref, g_ref, o_ref):
    i = pl.program_id(0)

    @pl.when(i < n_prompt_blocks)
    def _():
        o_ref[...] = _rms(xp_ref[...], g_ref[...]).astype(BF16)

    @pl.when(i >= n_prompt_blocks)
    def _():
        o_ref[...] = _rms(xs_ref[...], g_ref[...]).astype(BF16)


def _norm(xp, xs, g, tr=512):
    mp, ms = xp.shape[0], xs.shape[0]
    npb, nsb = mp // tr, ms // tr
    return pl.pallas_call(
        functools.partial(_norm_kernel, npb),
        grid=(npb + nsb,),
        in_specs=[
            pl.BlockSpec((tr, D_MODEL), lambda i: (jnp.minimum(i, npb - 1), 0)),
            pl.BlockSpec((tr, D_MODEL), lambda i: (jnp.maximum(i - npb, 0), 0)),
            pl.BlockSpec((1, D_MODEL), lambda i: (0, 0)),
        ],
        out_specs=pl.BlockSpec((tr, D_MODEL), lambda i: (i, 0)),
        out_shape=jax.ShapeDtypeStruct((mp + ms, D_MODEL), BF16),
        compiler_params=_params(("arbitrary",)),
        name="norm",
    )(xp, xs, g)


def _rope(h, cos, sa, sb):
    return h * cos + pltpu.roll(h, ROT_HALF, 1) * sa + pltpu.roll(h, LANES - ROT_HALF, 1) * sb


def _proj_in_kernel(nu, nq, xn_ref, win_ref, wkv_ref, cos_ref, sa_ref, sb_ref,
                    u_ref, v_ref, q_ref, k_ref, va_ref):
    j = pl.program_id(1)
    xn = xn_ref[...]

    def mm(w_ref):
        return jnp.dot(xn, w_ref[...], preferred_element_type=F32)

    def rope_cols(h, scale):
        cos, sa, sb = cos_ref[...], sa_ref[...], sb_ref[...]
        parts = [_rope(h[:, c:c + LANES], cos, sa, sb) * scale for c in range(0, h.shape[1], LANES)]
        return jnp.concatenate(parts, axis=1)

    @pl.when(j < nu)
    def _():
        u_ref[...] = jax.nn.gelu(mm(win_ref)).astype(u_ref.dtype)

    @pl.when((j >= nu) & (j < 2 * nu))
    def _():
        v_ref[...] = jax.nn.gelu(mm(win_ref)).astype(v_ref.dtype)

    @pl.when((j >= 2 * nu) & (j < 2 * nu + nq))
    def _():
        q_ref[...] = rope_cols(mm(win_ref), HEAD_DIM ** -0.5).astype(q_ref.dtype)

    @pl.when(j == 2 * nu + nq)
    def _():
        h = mm(wkv_ref)
        k_ref[...] = rope_cols(h[:, :KV_WIDTH], 1.0)
        va_ref[...] = h[:, KV_WIDTH:]


def _proj_in(xn, w_in, cos_t, sa_t, sb_t, n_prompt_rows, seq, tm=1024, tn=1024):
    m = xn.shape[0]
    nu = D_MODEL // tn
    nq = Q_WIDTH // tn
    n_wide = 2 * nu + nq
    kv_block = (2 * D_MODEL + Q_WIDTH) // (2 * KV_WIDTH)
    npb = n_prompt_rows // tm
    blocks_per_seq = seq // tm

    def tab_idx(i, j):
        return (jnp.where(i < npb, i % blocks_per_seq, blocks_per_seq), 0)

    clip = lambda x, lo, hi: jnp.minimum(jnp.maximum(x, lo), hi)
    tab_spec = pl.BlockSpec((tm, LANES), tab_idx)
    out_shapes = (
        jax.ShapeDtypeStruct((m, D_MODEL), BF16),
        jax.ShapeDtypeStruct((m, D_MODEL), F32),
        jax.ShapeDtypeStruct((m, Q_WIDTH), BF16),
        jax.ShapeDtypeStruct((m, KV_WIDTH), F32),
        jax.ShapeDtypeStruct((m, KV_WIDTH), F32),
    )
    return pl.pallas_call(
        functools.partial(_proj_in_kernel, nu, nq),
        grid=(m // tm, n_wide + 1),
        in_specs=[
            pl.BlockSpec((tm, D_MODEL), lambda i, j: (i, 0)),
            pl.BlockSpec((D_MODEL, tn), lambda i, j: (0, jnp.minimum(j, n_wide - 1))),
            pl.BlockSpec((D_MODEL, 2 * KV_WIDTH), lambda i, j: (0, kv_block)),
            tab_spec, tab_spec, tab_spec,
        ],
        out_specs=(
            pl.BlockSpec((tm, tn), lambda i, j: (i, jnp.minimum(j, nu - 1))),
            pl.BlockSpec((tm, tn), lambda i, j: (i, clip(j - nu, 0, nu - 1))),
            pl.BlockSpec((tm, tn), lambda i, j: (i, clip(j - 2 * nu, 0, nq - 1))),
            pl.BlockSpec((tm, KV_WIDTH), lambda i, j: (i, 0)),
            pl.BlockSpec((tm, KV_WIDTH), lambda i, j: (i, 0)),
        ),
        out_shape=out_shapes,
        compiler_params=_params(("arbitrary", "arbitrary")),
        name="proj_in",
    )(xn, w_in, w_in, cos_t, sa_t, sb_t)


def _proj_gate_kernel(counts, xn_ref, w_ref, b_ref, *refs):
    n = len(counts)
    riders, g_ref, cast = refs[:n], refs[n], refs[n + 1:]
    h = jnp.dot(xn_ref[...], w_ref[...].astype(BF16), preferred_element_type=F32)
    g_ref[...] = jax.nn.sigmoid(h + b_ref[...]).astype(g_ref.dtype)
    _rider_cast(pl.program_id(0) * pl.num_programs(1) + pl.program_id(1), counts, riders, cast)


def _proj_gate(xn, w_gate, b_gate, riders, tm=1024, tn=1024):
    m = xn.shape[0]
    n = w_gate.shape[1]
    nj = n // tn
    counts, r_in, r_out, r_shapes = _rider_specs(riders, (m // tm) * nj, lambda i, j: i * nj + j)
    return pl.pallas_call(
        functools.partial(_proj_gate_kernel, counts),
        grid=(m // tm, nj),
        in_specs=[
            pl.BlockSpec((tm, D_MODEL), lambda i, j: (i, 0)),
            pl.BlockSpec((D_MODEL, tn), lambda i, j: (0, j)),
            pl.BlockSpec((1, tn), lambda i, j: (0, j)),
            *r_in,
        ],
        out_specs=(pl.BlockSpec((tm, tn), lambda i, j: (i, j)), *r_out),
        out_shape=(jax.ShapeDtypeStruct((m, n), BF16), *r_shapes),
        compiler_params=_params(("arbitrary", "arbitrary")),
        name="proj_gate",
    )(xn, w_gate, b_gate, *riders)


def _gmlp_kernel(n_prompt_blocks, tr, u_ref, v_ref, lng_ref, lnb_ref, ws_ref, bst_ref,
                 a_ref, vn_ref):
    i = pl.program_id(0)
    v = v_ref[...]
    mu = jnp.mean(v, axis=-1, keepdims=True)
    vc = v - mu
    var = jnp.mean(vc * vc, axis=-1, keepdims=True)
    vn = vc * lax.rsqrt(var + EPS) * lng_ref[...] + lnb_ref[...]
    vnb = vn.astype(BF16)

    def mix(length):
        r = lax.broadcasted_iota(jnp.int32, (length, length), 0) // CHUNK
        c = lax.broadcasted_iota(jnp.int32, (length, length), 1) // CHUNK
        for g in range(GMLP_GROUPS):
            w = jnp.where(c <= r, ws_ref[g, :length, :length], 0.0).astype(BF16)
            bias = bst_ref[:length, g:g + 1]
            cols = slice(g * GMLP_GROUP_DIM, (g + 1) * GMLP_GROUP_DIM)
            for b in range(tr // length):
                rows = slice(b * length, (b + 1) * length)
                s = jnp.dot(w, vnb[rows, cols], preferred_element_type=F32) + bias
                a_ref[rows, cols] = (u_ref[rows, cols].astype(F32) * s).astype(a_ref.dtype)

    @pl.when(i < n_prompt_blocks)
    def _():
        mix(GMLP_CHUNK)

    @pl.when(i >= n_prompt_blocks)
    def _():
        vn_ref[...] = vn
        mix(CHUNK)


def _gmlp(u, v, ln_g, ln_b, ws, bs_t, n_prompt_rows, tr=256):
    m = u.shape[0]
    npb = n_prompt_rows // tr
    return pl.pallas_call(
        functools.partial(_gmlp_kernel, npb, tr),
        grid=(m // tr,),
        in_specs=[
            pl.BlockSpec((tr, D_MODEL), lambda i: (i, 0)),
            pl.BlockSpec((tr, D_MODEL), lambda i: (i, 0)),
            pl.BlockSpec((1, D_MODEL), lambda i: (0, 0)),
            pl.BlockSpec((1, D_MODEL), lambda i: (0, 0)),
            pl.BlockSpec((GMLP_GROUPS, GMLP_CHUNK, GMLP_CHUNK), lambda i: (0, 0, 0)),
            pl.BlockSpec((GMLP_CHUNK, GMLP_GROUPS), lambda i: (0, 0)),
        ],
        out_specs=(
            pl.BlockSpec((tr, D_MODEL), lambda i: (i, 0)),
            pl.BlockSpec((tr, D_MODEL), lambda i: (jnp.maximum(i - npb, 0), 0)),
        ),
        out_shape=(
            jax.ShapeDtypeStruct((m, D_MODEL), BF16),
            jax.ShapeDtypeStruct((m - n_prompt_rows, D_MODEL), F32),
        ),
        compiler_params=_params(("arbitrary",)),
        name="gmlp",
    )(u, v, ln_g, ln_b, ws, bs_t)


def _dup_head(pair, pair_swapped, low_half, odd):
    if odd:
        return jnp.where(low_half, pair_swapped, pair).astype(BF16)
    return jnp.where(low_half, pair, pair_swapped).astype(BF16)


def _attend(q_ref, kwin, vwin, valid, sink_ref, o_ref):
    rows, keys = q_ref.shape[0], kwin.shape[0]
    low_q = lax.broadcasted_iota(jnp.int32, (rows, LANES), 1) < HEAD_DIM
    low_k = lax.broadcasted_iota(jnp.int32, (keys, LANES), 1) < HEAD_DIM
    heads_per_pair = LANES // HEAD_DIM
    for c in range(N_KV_HEADS // heads_per_pair):
        kpair = kwin[:, c * LANES:(c + 1) * LANES]
        vpair = vwin[:, c * LANES:(c + 1) * LANES]
        kswap = pltpu.roll(kpair, HEAD_DIM, 1)
        vswap = pltpu.roll(vpair, HEAD_DIM, 1)
        for odd in range(heads_per_pair):
            hk = c * heads_per_pair + odd
            kdup = _dup_head(kpair, kswap, low_k, odd)
            vdup = _dup_head(vpair, vswap, low_k, odd)
            parts = []
            for r in range(Q_REP):
                h = hk * Q_REP + r
                qpair = q_ref[:, (h // 2) * LANES:(h // 2 + 1) * LANES]
                keep = low_q if h % 2 == 0 else jnp.logical_not(low_q)
                parts.append(jnp.where(keep, qpair, jnp.zeros_like(qpair)))
            s_all = lax.dot_general(jnp.concatenate(parts, axis=0), kdup, (((1,), (1,)), ((), ())),
                                    preferred_element_type=F32)
            ps, dens = [], []
            for r in range(Q_REP):
                s = s_all[r * rows:(r + 1) * rows]
                if valid is not None:
                    s = jnp.where(valid, s, NEG)
                sink = sink_ref[hk * Q_REP + r]
                mx = jnp.maximum(jnp.max(s, axis=-1, keepdims=True), sink)
                p = jnp.exp(s - mx)
                dens.append(jnp.sum(p, axis=-1, keepdims=True) + jnp.exp(sink - mx))
                ps.append(p.astype(BF16))
            o_all = jnp.dot(jnp.concatenate(ps, axis=0), vdup, preferred_element_type=F32)
            for r in range(0, Q_REP, 2):
                h = hk * Q_REP + r
                even = o_all[r * rows:(r + 1) * rows] / dens[r]
                oddh = o_all[(r + 1) * rows:(r + 2) * rows] / dens[r + 1]
                o_ref[:, (h // 2) * LANES:(h // 2 + 1) * LANES] = (
                    jnp.where(low_q, even, oddh).astype(o_ref.dtype))


def _attn_prompt_kernel(tq, counts, sink_ref, q_ref, kp_ref, kc_ref, vp_ref, vc_ref, *refs):
    n = len(counts)
    riders, o_ref, cast = refs[:n], refs[n], refs[n + 1:]
    t = pl.program_id(1)
    _rider_cast(pl.program_id(0) * pl.num_programs(1) + t, counts, riders, cast)
    kwin = jnp.concatenate([kp_ref[...], kc_ref[...]], axis=0)
    vwin = jnp.concatenate([vp_ref[...], vc_ref[...]], axis=0)
    qc = lax.broadcasted_iota(jnp.int32, (tq, 2 * tq), 0) // CHUNK
    kc = lax.broadcasted_iota(jnp.int32, (tq, 2 * tq), 1) // CHUNK - WINDOW // CHUNK
    valid = (kc <= qc) & (kc >= qc - WINDOW // CHUNK) & (kc + 2 * t >= 0)
    _attend(q_ref, kwin, vwin, valid, sink_ref, o_ref)


def _attn_prompt(sinks, q, k, va, batch, seq, riders, tq=128):
    m = q.shape[0]
    nt = seq // tq
    assert tq == WINDOW
    cur = lambda b, t: (b * nt + t, 0)
    prev = lambda b, t: (b * nt + jnp.maximum(t - 1, 0), 0)
    counts, r_in, r_out, r_shapes = _rider_specs(riders, batch * nt, lambda b, t: b * nt + t)
    return pl.pallas_call(
        functools.partial(_attn_prompt_kernel, tq, counts),
        grid=(batch, nt),
        in_specs=[
            pl.BlockSpec(memory_space=pltpu.SMEM),
            pl.BlockSpec((tq, Q_WIDTH), cur),
            pl.BlockSpec((tq, KV_WIDTH), prev),
            pl.BlockSpec((tq, KV_WIDTH), cur),
            pl.BlockSpec((tq, KV_WIDTH), prev),
            pl.BlockSpec((tq, KV_WIDTH), cur),
            *r_in,
        ],
        out_specs=(pl.BlockSpec((tq, Q_WIDTH), cur), *r_out),
        out_shape=(jax.ShapeDtypeStruct((m, Q_WIDTH), BF16), *r_shapes),
        compiler_params=_params(("arbitrary", "arbitrary")),
        name="attn_prompt",
    )(sinks, q, k, k, va, va, *riders)


def _attn_sample_kernel(sink_ref, q_ref, ck_ref, kn_ref, cv_ref, vn_ref, o_in_ref, o_ref):
    del o_in_ref
    kwin = jnp.concatenate([ck_ref[...], kn_ref[...]], axis=0)
    vwin = jnp.concatenate([cv_ref[...], vn_ref[...]], axis=0)
    _attend(q_ref, kwin, vwin, None, sink_ref, o_ref)


def _attn_sample(sinks, q, k, va, cache_k, cache_v, o, n_prompt_rows, dec_batch, dec_seq):
    first = n_prompt_rows // dec_seq
    new = lambda b: (first + b, 0)
    cached = lambda b: (b, 0, 0)
    cache_len = cache_k.shape[1]
    return pl.pallas_call(
        _attn_sample_kernel,
        grid=(dec_batch,),
        in_specs=[
            pl.BlockSpec(memory_space=pltpu.SMEM),
            pl.BlockSpec((dec_seq, Q_WIDTH), new),
            pl.BlockSpec((None, cache_len, KV_WIDTH), cached),
            pl.BlockSpec((dec_seq, KV_WIDTH), new),
            pl.BlockSpec((None, cache_len, KV_WIDTH), cached),
            pl.BlockSpec((dec_seq, KV_WIDTH), new),
            pl.BlockSpec(memory_space=pl.ANY),
        ],
        out_specs=pl.BlockSpec((dec_seq, Q_WIDTH), new),
        out_shape=jax.ShapeDtypeStruct(o.shape, o.dtype),
        input_output_aliases={6: 0},
        compiler_params=_params(("arbitrary",)),
        name="attn_sample",
    )(sinks, q, cache_k, k, cache_v, va, o)


def _branch_kernel(a_ref, o_ref, wa_ref, wb_ref, ga_ref, gb_ref, t_ref):
    ya = jnp.dot(a_ref[...], wa_ref[...], preferred_element_type=F32)
    yb = jnp.dot(o_ref[...], wb_ref[...], preferred_element_type=F32)
    t = ga_ref[...].astype(F32) * ya + gb_ref[...].astype(F32) * yb
    t_ref[...] = t.astype(t_ref.dtype)


def _branch(a, o, wa, wb, g, tm=1024, tn=512):
    m = a.shape[0]
    nj = D_MODEL // tn
    return pl.pallas_call(
        _branch_kernel,
        grid=(m // tm, nj),
        in_specs=[
            pl.BlockSpec((tm, D_MODEL), lambda i, j: (i, 0)),
            pl.BlockSpec((tm, Q_WIDTH), lambda i, j: (i, 0)),
            pl.BlockSpec((D_MODEL, tn), lambda i, j: (0, j)),
            pl.BlockSpec((Q_WIDTH, tn), lambda i, j: (0, j)),
            pl.BlockSpec((tm, tn), lambda i, j: (i, j)),
            pl.BlockSpec((tm, tn), lambda i, j: (i, j + nj)),
        ],
        out_specs=pl.BlockSpec((tm, tn), lambda i, j: (i, j)),
        out_shape=jax.ShapeDtypeStruct((m, D_MODEL), BF16),
        compiler_params=_params(("arbitrary", "arbitrary")),
        name="branch",
    )(a, o, wa, wb, g, g)


def _out_kernel(n_prompt_blocks, t_ref, w_ref, xp_ref, xs_ref, x1_ref):
    i = pl.program_id(0)
    y = jnp.dot(t_ref[...], w_ref[...], preferred_element_type=F32)

    @pl.when(i < n_prompt_blocks)
    def _():
        x1_ref[...] = xp_ref[...] + y

    @pl.when(i >= n_prompt_blocks)
    def _():
        x1_ref[...] = xs_ref[...] + y


def _out(t, w_out, xp, xs, tm=1024, tn=1024):
    m = t.shape[0]
    npb = xp.shape[0] // tm
    return pl.pallas_call(
        functools.partial(_out_kernel, npb),
        grid=(m // tm, D_MODEL // tn),
        in_specs=[
            pl.BlockSpec((tm, D_MODEL), lambda i, j: (i, 0)),
            pl.BlockSpec((D_MODEL, tn), lambda i, j: (0, j)),
            pl.BlockSpec((tm, tn), lambda i, j: (jnp.minimum(i, npb - 1), j)),
            pl.BlockSpec((tm, tn), lambda i, j: (jnp.maximum(i - npb, 0), j)),
        ],
        out_specs=pl.BlockSpec((tm, tn), lambda i, j: (i, j)),
        out_shape=jax.ShapeDtypeStruct((m, D_MODEL), F32),
        compiler_params=_params(("arbitrary", "arbitrary")),
        name="out_proj",
    )(t, w_out, xp, xs)


def _ffn_kernel(x1_ref, ng_ref, wg_ref, wu_ref, wd_ref, fg_ref, y_ref, h_ref):
    f = pl.program_id(1)

    @pl.when(f == 0)
    def _():
        x1 = x1_ref[...]
        h_ref[...] = _rms(x1, ng_ref[...]).astype(h_ref.dtype)
        y_ref[...] = x1

    h = h_ref[...]
    gate = jnp.dot(h, wg_ref[...], preferred_element_type=F32)
    up = jnp.dot(h, wu_ref[...], preferred_element_type=F32)
    hid = (jax.nn.silu(gate) * up).astype(BF16)
    y_ref[...] += jnp.dot(hid, wd_ref[...], preferred_element_type=F32)

    @pl.when(f == pl.num_programs(1) - 1)
    def _():
        y_ref[...] = _rms(y_ref[...], fg_ref[...])


def _ffn(x1, row_block_offset, n_row_blocks, norm_g, wg, wu, wd, final_g, tm=512, tf=512):
    d_ff = wg.shape[1]
    return pl.pallas_call(
        _ffn_kernel,
        grid=(n_row_blocks, d_ff // tf),
        in_specs=[
            pl.BlockSpec((tm, D_MODEL), lambda i, f: (i + row_block_offset, 0)),
            pl.BlockSpec((1, D_MODEL), lambda i, f: (0, 0)),
            pl.BlockSpec((D_MODEL, tf), lambda i, f: (0, f)),
            pl.BlockSpec((D_MODEL, tf), lambda i, f: (0, f)),
            pl.BlockSpec((tf, D_MODEL), lambda i, f: (f, 0)),
            pl.BlockSpec((1, D_MODEL), lambda i, f: (0, 0)),
        ],
        out_specs=pl.BlockSpec((tm, D_MODEL), lambda i, f: (i, 0)),
        out_shape=jax.ShapeDtypeStruct((n_row_blocks * tm, D_MODEL), F32),
        scratch_shapes=[pltpu.VMEM((tm, D_MODEL), BF16)],
        compiler_params=_params(("arbitrary", "arbitrary")),
        name="ffn",
    )(x1, norm_g, wg, wu, wd, final_g)


def _rope_tables(seq, past_len, dec_batch, dec_seq):
    pos = jnp.concatenate([jnp.arange(seq, dtype=F32),
                           jnp.tile(past_len + jnp.arange(dec_seq, dtype=F32), dec_batch)])
    inv_freq = jnp.float32(ROPE_THETA) ** (-(jnp.arange(ROT_HALF, dtype=F32) * 2.0 / ROT_DIM))
    ang = pos[:, None] * inv_freq[None, :]
    cos, sin = jnp.cos(ang), jnp.sin(ang)
    n = pos.shape[0]
    pad = jnp.zeros((n, HEAD_DIM - ROT_DIM), F32)
    zero = jnp.zeros((n, ROT_HALF), F32)
    cos_h = jnp.concatenate([cos, cos, pad + 1.0], axis=1)
    sa_h = jnp.concatenate([zero, sin, pad], axis=1)
    sb_h = jnp.concatenate([-sin, zero, pad], axis=1)
    rep = LANES // HEAD_DIM
    return jnp.tile(cos_h, (1, rep)), jnp.tile(sa_h, (1, rep)), jnp.tile(sb_h, (1, rep))


def kernel(x_prompt, x_sample, cache_swa_k, cache_swa_v, norm_mix_g, w_in, gmlp_ln_g, gmlp_ln_b,
           gmlp_ws, gmlp_bs, attn_sinks, w_gate, b_gate, w_branch_a, w_branch_b, w_out,
           norm_ffn_g, w_ffn_gate, w_ffn_up, w_ffn_down, final_norm_g):
    batch, seq, _ = x_prompt.shape
    dec_batch, dec_seq, _ = x_sample.shape
    depth = w_in.shape[0]
    past_len = 2048
    assert depth == 1 and cache_swa_k.shape[2] == WINDOW and dec_seq == CHUNK
    mp, ms = batch * seq, dec_batch * dec_seq

    xp = x_prompt.reshape(mp, D_MODEL)
    xs = x_sample.reshape(ms, D_MODEL)
    cos_t, sa_t, sb_t = _rope_tables(seq, past_len, dec_batch, dec_seq)
    row = lambda p: p.reshape(1, -1)

    l = 0
    xn = _norm(xp, xs, row(norm_mix_g[l]))
    g, w_in_b = _proj_gate(xn, w_gate[l], row(b_gate[l]), [w_in[l]])
    u, v, q, k, va = _proj_in(xn, w_in_b, cos_t, sa_t, sb_t, mp, seq)
    a, vn_s = _gmlp(u, v, row(gmlp_ln_g[l]), row(gmlp_ln_b[l]), gmlp_ws[l], gmlp_bs[l].T, mp)
    o, wa_b, wb_b, wo_b, wg, wu, wd = _attn_prompt(
        attn_sinks[l], q, k, va, batch, seq,
        [w_branch_a[l], w_branch_b[l], w_out[l], w_ffn_gate[l], w_ffn_up[l], w_ffn_down[l]])
    o = _attn_sample(attn_sinks[l], q, k, va,
                     cache_swa_k[l].reshape(dec_batch, WINDOW, KV_WIDTH),
                     cache_swa_v[l].reshape(dec_batch, WINDOW, KV_WIDTH),
                     o, mp, dec_batch, dec_seq)
    t = _branch(a, o, wa_b, wb_b, g)
    x1 = _out(t, wo_b, xp, xs)
    ffn_tm = 512
    ffn = functools.partial(_ffn, norm_g=row(norm_ffn_g[l]), wg=wg, wu=wu, wd=wd,
                            final_g=row(final_norm_g), tm=ffn_tm)
    y_prompt = ffn(x1, 0, mp // ffn_tm)
    y_sample = ffn(x1, mp // ffn_tm, ms // ffn_tm)

    keep = min(WINDOW, seq)
    kp = k[:mp].reshape(batch, seq, N_KV_HEADS, HEAD_DIM)[:, seq - keep:]
    vp = va[:mp].reshape(batch, seq, N_KV_HEADS, HEAD_DIM)[:, seq - keep:]
    return (
        y_prompt.reshape(batch, seq, D_MODEL),
        y_sample.reshape(dec_batch, dec_seq, D_MODEL),
        kp[None],
        vp[None],
        k[mp:].reshape(1, dec_batch, dec_seq, N_KV_HEADS, HEAD_DIM),
        va[mp:].reshape(1, dec_batch, dec_seq, N_KV_HEADS, HEAD_DIM),
        vn_s.reshape(1, dec_batch, dec_seq, D_MODEL),
    )
```

```python
import functools

import jax
import jax.numpy as jnp
from jax import lax
from jax.experimental import pallas as pl
from jax.experimental.pallas import tpu as pltpu

D_MODEL = 2048
CHUNK = 64
GMLP_CHUNK = 128
GMLP_GROUPS = 8
GMLP_GROUP_DIM = D_MODEL // GMLP_GROUPS
N_HEADS = 32
N_KV_HEADS = 4
HEAD_DIM = 64
Q_REP = N_HEADS // N_KV_HEADS
WINDOW = 128
ROPE_THETA = 500000.0
ROT_DIM = HEAD_DIM // 4
ROT_HALF = ROT_DIM // 2
Q_WIDTH = N_HEADS * HEAD_DIM
KV_WIDTH = N_KV_HEADS * HEAD_DIM
EPS = 1e-6
NEG = -1e30
LANES = 128

F32 = jnp.float32
BF16 = jnp.bfloat16

VMEM_LIMIT = 56 * 1024 * 1024


def _params(semantics):
    return pltpu.CompilerParams(dimension_semantics=semantics, vmem_limit_bytes=VMEM_LIMIT)


def _rms(x, g):
    return x * lax.rsqrt(jnp.mean(x * x, axis=-1, keepdims=True) + EPS) * g


BF16_SUBLANES = 16


def _rider_specs(weights, n_steps, step_of):
    counts, in_specs, out_specs, out_shapes = [], [], [], []
    for w in weights:
        rows, cols = w.shape
        nb = max(n for n in range(1, n_steps + 1)
                 if rows % n == 0 and (rows // n) % BF16_SUBLANES == 0)
        idx = lambda *g, nb=nb: (jnp.minimum(step_of(*g), nb - 1), 0)
        counts.append(nb)
        in_specs.append(pl.BlockSpec((rows // nb, cols), idx))
        out_specs.append(pl.BlockSpec((rows // nb, cols), idx))
        out_shapes.append(jax.ShapeDtypeStruct(w.shape, BF16))
    return counts, in_specs, out_specs, out_shapes


def _rider_cast(step, counts, src_refs, dst_refs):
    for nb, src, dst in zip(counts, src_refs, dst_refs):
        @pl.when(step < nb)
        def _():
            dst[...] = src[...].astype(dst.dtype)


def _norm_kernel(n_prompt_blocks, xp_ref, xs_ref, g_ref, o_ref):
    i = pl.program_id(0)

    @pl.when(i < n_prompt_blocks)
    def _():
        o_ref[...] = _rms(xp_ref[...], g_ref[...]).astype(BF16)

    @pl.when(i >= n_prompt_blocks)
    def _():
        o_ref[...] = _rms(xs_ref[...], g_ref[...]).astype(BF16)


def _norm(xp, xs, g, tr=512):
    mp, ms = xp.shape[0], xs.shape[0]
    npb, nsb = mp // tr, ms // tr
    return pl.pallas_call(
        functools.partial(_norm_kernel, npb),
        grid=(npb + nsb,),
        in_specs=[
            pl.BlockSpec((tr, D_MODEL), lambda i: (jnp.minimum(i, npb - 1), 0)),
            pl.BlockSpec((tr, D_MODEL), lambda i: (jnp.maximum(i - npb, 0), 0)),
            pl.BlockSpec((1, D_MODEL), lambda i: (0, 0)),
        ],
        out_specs=pl.BlockSpec((tr, D_MODEL), lambda i: (i, 0)),
        out_shape=jax.ShapeDtypeStruct((mp + ms, D_MODEL), BF16),
        compiler_params=_params(("arbitrary",)),
        name="norm",
    )(xp, xs, g)


def _rope(h, cos, sa, sb):
    return h * cos + pltpu.roll(h, ROT_HALF, 1) * sa + pltpu.roll(h, LANES - ROT_HALF, 1) * sb


def _proj_in_kernel(nu, nq, xn_ref, win_ref, wkv_ref, cos_ref, sa_ref, sb_ref,
                    u_ref, v_ref, q_ref, k_ref, va_ref):
    j = pl.program_id(1)
    xn = xn_ref[...]

    def mm(w_ref):
        return jnp.dot(xn, w_ref[...], preferred_element_type=F32)

    def rope_cols(h, scale):
        cos, sa, sb = cos_ref[...], sa_ref[...], sb_ref[...]
        parts = [_rope(h[:, c:c + LANES], cos, sa, sb) * scale for c in range(0, h.shape[1], LANES)]
        return jnp.concatenate(parts, axis=1)

    @pl.when(j < nu)
    def _():
        u_ref[...] = jax.nn.gelu(mm(win_ref)).astype(u_ref.dtype)

    @pl.when((j >= nu) & (j < 2 * nu))
    def _():
        v_ref[...] = jax.nn.gelu(mm(win_ref)).astype(v_ref.dtype)

    @pl.when((j >= 2 * nu) & (j < 2 * nu + nq))
    def _():
        q_ref[...] = rope_cols(mm(win_ref), HEAD_DIM ** -0.5).astype(q_ref.dtype)

    @pl.when(j == 2 * nu + nq)
    def _():
        h = mm(wkv_ref)
        k_ref[...] = rope_cols(h[:, :KV_WIDTH], 1.0)
        va_ref[...] = h[:, KV_WIDTH:]


def _proj_in(xn, w_in, cos_t, sa_t, sb_t, n_prompt_rows, seq, tm=1024, tn=1024):
    m = xn.shape[0]
    nu = D_MODEL // tn
    nq = Q_WIDTH // tn
    n_wide = 2 * nu + nq
    kv_block = (2 * D_MODEL + Q_WIDTH) // (2 * KV_WIDTH)
    npb = n_prompt_rows // tm
    blocks_per_seq = seq // tm

    def tab_idx(i, j):
        return (jnp.where(i < npb, i % blocks_per_seq, blocks_per_seq), 0)

    clip = lambda x, lo, hi: jnp.minimum(jnp.maximum(x, lo), hi)
    tab_spec = pl.BlockSpec((tm, LANES), tab_idx)
    out_shapes = (
        jax.ShapeDtypeStruct((m, D_MODEL), BF16),
        jax.ShapeDtypeStruct((m, D_MODEL), F32),
        jax.ShapeDtypeStruct((m, Q_WIDTH), BF16),
        jax.ShapeDtypeStruct((m, KV_WIDTH), F32),
        jax.ShapeDtypeStruct((m, KV_WIDTH), F32),
    )
    return pl.pallas_call(
        functools.partial(_proj_in_kernel, nu, nq),
        grid=(m // tm, n_wide + 1),
        in_specs=[
            pl.BlockSpec((tm, D_MODEL), lambda i, j: (i, 0)),
            pl.BlockSpec((D_MODEL, tn), lambda i, j: (0, jnp.minimum(j, n_wide - 1))),
            pl.BlockSpec((D_MODEL, 2 * KV_WIDTH), lambda i, j: (0, kv_block)),
            tab_spec, tab_spec, tab_spec,
        ],
        out_specs=(
            pl.BlockSpec((tm, tn), lambda i, j: (i, jnp.minimum(j, nu - 1))),
            pl.BlockSpec((tm, tn), lambda i, j: (i, clip(j - nu, 0, nu - 1))),
            pl.BlockSpec((tm, tn), lambda i, j: (i, clip(j - 2 * nu, 0, nq - 1))),
            pl.BlockSpec((tm, KV_WIDTH), lambda i, j: (i, 0)),
            pl.BlockSpec((tm, KV_WIDTH), lambda i, j: (i, 0)),
        ),
        out_shape=out_shapes,
        compiler_params=_params(("arbitrary", "arbitrary")),
        name="proj_in",
    )(xn, w_in, w_in, cos_t, sa_t, sb_t)


def _proj_gate_kernel(counts, xn_ref, w_ref, b_ref, *refs):
    n = len(counts)
    riders, g_ref, cast = refs[:n], refs[n], refs[n + 1:]
    h = jnp.dot(xn_ref[...], w_ref[...].astype(BF16), preferred_element_type=F32)
    g_ref[...] = jax.nn.sigmoid(h + b_ref[...]).astype(g_ref.dtype)
    _rider_cast(pl.program_id(0) * pl.num_programs(1) + pl.program_id(1), counts, riders, cast)


def _proj_gate(xn, w_gate, b_gate, riders, tm=1024, tn=1024):
    m = xn.shape[0]
    n = w_gate.shape[1]
    nj = n // tn
    counts, r_in, r_out, r_shapes = _rider_specs(riders, (m // tm) * nj, lambda i, j: i * nj + j)
    return pl.pallas_call(
        functools.partial(_proj_gate_kernel, counts),
        grid=(m // tm, nj),
        in_specs=[
            pl.BlockSpec((tm, D_MODEL), lambda i, j: (i, 0)),
            pl.BlockSpec((D_MODEL, tn), lambda i, j: (0, j)),
            pl.BlockSpec((1, tn), lambda i, j: (0, j)),
            *r_in,
        ],
        out_specs=(pl.BlockSpec((tm, tn), lambda i, j: (i, j)), *r_out),
        out_shape=(jax.ShapeDtypeStruct((m, n), BF16), *r_shapes),
        compiler_params=_params(("arbitrary", "arbitrary")),
        name="proj_gate",
    )(xn, w_gate, b_gate, *riders)


def _gmlp_kernel(n_prompt_blocks, tr, u_ref, v_ref, lng_ref, lnb_ref, ws_ref, bst_ref,
                 a_ref, vn_ref):
    i = pl.program_id(0)
    v = v_ref[...]
    mu = jnp.mean(v, axis=-1, keepdims=True)
    vc = v - mu
    var = jnp.mean(vc * vc, axis=-1, keepdims=True)
    vn = vc * lax.rsqrt(var + EPS) * lng_ref[...] + lnb_ref[...]
    vnb = vn.astype(BF16)

    def mix(length):
        r = lax.broadcasted_iota(jnp.int32, (length, length), 0) // CHUNK
        c = lax.broadcasted_iota(jnp.int32, (length, length), 1) // CHUNK
        for g in range(GMLP_GROUPS):
            w = jnp.where(c <= r, ws_ref[g, :length, :length], 0.0).astype(BF16)
            bias = bst_ref[:length, g:g + 1]
            cols = slice(g * GMLP_GROUP_DIM, (g + 1) * GMLP_GROUP_DIM)
            for b in range(tr // length):
                rows = slice(b * length, (b + 1) * length)
                s = jnp.dot(w, vnb[rows, cols], preferred_element_type=F32) + bias
                a_ref[rows, cols] = (u_ref[rows, cols].astype(F32) * s).astype(a_ref.dtype)

    @pl.when(i < n_prompt_blocks)
    def _():
        mix(GMLP_CHUNK)

    @pl.when(i >= n_prompt_blocks)
    def _():
        vn_ref[...] = vn
        mix(CHUNK)


def _gmlp(u, v, ln_g, ln_b, ws, bs_t, n_prompt_rows, tr=256):
    m = u.shape[0]
    npb = n_prompt_rows // tr
    return pl.pallas_call(
        functools.partial(_gmlp_kernel, npb, tr),
        grid=(m // tr,),
        in_specs=[
            pl.BlockSpec((tr, D_MODEL), lambda i: (i, 0)),
            pl.BlockSpec((tr, D_MODEL), lambda i: (i, 0)),
            pl.BlockSpec((1, D_MODEL), lambda i: (0, 0)),
            pl.BlockSpec((1, D_MODEL), lambda i: (0, 0)),
            pl.BlockSpec((GMLP_GROUPS, GMLP_CHUNK, GMLP_CHUNK), lambda i: (0, 0, 0)),
            pl.BlockSpec((GMLP_CHUNK, GMLP_GROUPS), lambda i: (0, 0)),
        ],
        out_specs=(
            pl.BlockSpec((tr, D_MODEL), lambda i: (i, 0)),
            pl.BlockSpec((tr, D_MODEL), lambda i: (jnp.maximum(i - npb, 0), 0)),
        ),
        out_shape=(
            jax.ShapeDtypeStruct((m, D_MODEL), BF16),
            jax.ShapeDtypeStruct((m - n_prompt_rows, D_MODEL), F32),
        ),
        compiler_params=_params(("arbitrary",)),
        name="gmlp",
    )(u, v, ln_g, ln_b, ws, bs_t)


def _dup_head(pair, pair_swapped, low_half, odd):
    if odd:
        return jnp.where(low_half, pair_swapped, pair).astype(BF16)
    return jnp.where(low_half, pair, pair_swapped).astype(BF16)


def _attend(q_ref, kwin, vwin, valid, sink_ref, o_ref):
    rows, keys = q_ref.shape[0], kwin.shape[0]
    low_q = lax.broadcasted_iota(jnp.int32, (rows, LANES), 1) < HEAD_DIM
    low_k = lax.broadcasted_iota(jnp.int32, (keys, LANES), 1) < HEAD_DIM
    heads_per_pair = LANES // HEAD_DIM
    for c in range(N_KV_HEADS // heads_per_pair):
        kpair = kwin[:, c * LANES:(c + 1) * LANES]
        vpair = vwin[:, c * LANES:(c + 1) * LANES]
        kswap = pltpu.roll(kpair, HEAD_DIM, 1)
        vswap = pltpu.roll(vpair, HEAD_DIM, 1)
        for odd in range(heads_per_pair):
            hk = c * heads_per_pair + odd
            kdup = _dup_head(kpair, kswap, low_k, odd)
            vdup = _dup_head(vpair, vswap, low_k, odd)
            parts = []
            for r in range(Q_REP):
                h = hk * Q_REP + r
                qpair = q_ref[:, (h // 2) * LANES:(h // 2 + 1) * LANES]
                keep = low_q if h % 2 == 0 else jnp.logical_not(low_q)
                parts.append(jnp.where(keep, qpair, jnp.zeros_like(qpair)))
            s_all = lax.dot_general(jnp.concatenate(parts, axis=0), kdup, (((1,), (1,)), ((), ())),
                                    preferred_element_type=F32)
            ps, dens = [], []
            for r in range(Q_REP):
                s = s_all[r * rows:(r + 1) * rows]
                if valid is not None:
                    s = jnp.where(valid, s, NEG)
                sink = sink_ref[hk * Q_REP + r]
                mx = jnp.maximum(jnp.max(s, axis=-1, keepdims=True), sink)
                p = jnp.exp(s - mx)
                dens.append(jnp.sum(p, axis=-1, keepdims=True) + jnp.exp(sink - mx))
                ps.append(p.astype(BF16))
            o_all = jnp.dot(jnp.concatenate(ps, axis=0), vdup, preferred_element_type=F32)
            for r in range(0, Q_REP, 2):
                h = hk * Q_REP + r
                even = o_all[r * rows:(r + 1) * rows] / dens[r]
                oddh = o_all[(r + 1) * rows:(r + 2) * rows] / dens[r + 1]
                o_ref[:, (h // 2) * LANES:(h // 2 + 1) * LANES] = (
                    jnp.where(low_q, even, oddh).astype(o_ref.dtype))


def _attn_prompt_kernel(tq, counts, sink_ref, q_ref, kp_ref, kc_ref, vp_ref, vc_ref, *refs):
    n = len(counts)
    riders, o_ref, cast = refs[:n], refs[n], refs[n + 1:]
    t = pl.program_id(1)
    _rider_cast(pl.program_id(0) * pl.num_programs(1) + t, counts, riders, cast)
    kwin = jnp.concatenate([kp_ref[...], kc_ref[...]], axis=0)
    vwin = jnp.concatenate([vp_ref[...], vc_ref[...]], axis=0)
    qc = lax.broadcasted_iota(jnp.int32, (tq, 2 * tq), 0) // CHUNK
    kc = lax.broadcasted_iota(jnp.int32, (tq, 2 * tq), 1) // CHUNK - WINDOW // CHUNK
    valid = (kc <= qc) & (kc >= qc - WINDOW // CHUNK) & (kc + 2 * t >= 0)
    _attend(q_ref, kwin, vwin, valid, sink_ref, o_ref)


def _attn_prompt(sinks, q, k, va, batch, seq, riders, tq=128):
    m = q.shape[0]
    nt = seq // tq
    assert tq == WINDOW
    cur = lambda b, t: (b * nt + t, 0)
    prev = lambda b, t: (b * nt + jnp.maximum(t - 1, 0), 0)
    counts, r_in, r_out, r_shapes = _rider_specs(riders, batch * nt, lambda b, t: b * nt + t)
    return pl.pallas_call(
        functools.partial(_attn_prompt_kernel, tq, counts),
        grid=(batch, nt),
        in_specs=[
            pl.BlockSpec(memory_space=pltpu.SMEM),
            pl.BlockSpec((tq, Q_WIDTH), cur),
            pl.BlockSpec((tq, KV_WIDTH), prev),
            pl.BlockSpec((tq, KV_WIDTH), cur),
            pl.BlockSpec((tq, KV_WIDTH), prev),
            pl.BlockSpec((tq, KV_WIDTH), cur),
            *r_in,
        ],
        out_specs=(pl.BlockSpec((tq, Q_WIDTH), cur), *r_out),
        out_shape=(jax.ShapeDtypeStruct((m, Q_WIDTH), BF16), *r_shapes),
        compiler_params=_params(("arbitrary", "arbitrary")),
        name="attn_prompt",
    )(sinks, q, k, k, va, va, *riders)


def _attn_sample_kernel(sink_ref, q_ref, ck_ref, kn_ref, cv_ref, vn_ref, o_in_ref, o_ref):
    del o_in_ref
    kwin = jnp.concatenate([ck_ref[...], kn_ref[...]], axis=0)
    vwin = jnp.concatenate([cv_ref[...], vn_ref[...]], axis=0)
    _attend(q_ref, kwin, vwin, None, sink_ref, o_ref)


def _attn_sample(sinks, q, k, va, cache_k, cache_v, o, n_prompt_rows, dec_batch, dec_seq):
    first = n_prompt_rows // dec_seq
    new = lambda b: (first + b, 0)
    cached = lambda b: (b, 0, 0)
    cache_len = cache_k.shape[1]
    return pl.pallas_call(
        _attn_sample_kernel,
        grid=(dec_batch,),
        in_specs=[
            pl.BlockSpec(memory_space=pltpu.SMEM),
            pl.BlockSpec((dec_seq, Q_WIDTH), new),
            pl.BlockSpec((None, cache_len, KV_WIDTH), cached),
            pl.BlockSpec((dec_seq, KV_WIDTH), new),
            pl.BlockSpec((None, cache_len, KV_WIDTH), cached),
            pl.BlockSpec((dec_seq, KV_WIDTH), new),
            pl.BlockSpec(memory_space=pl.ANY),
        ],
        out_specs=pl.BlockSpec((dec_seq, Q_WIDTH), new),
        out_shape=jax.ShapeDtypeStruct(o.shape, o.dtype),
        input_output_aliases={6: 0},
        compiler_params=_params(("arbitrary",)),
        name="attn_sample",
    )(sinks, q, cache_k, k, cache_v, va, o)


def _branch_kernel(a_ref, o_ref, wa_ref, wb_ref, ga_ref, gb_ref, t_ref):
    ya = jnp.dot(a_ref[...], wa_ref[...], preferred_element_type=F32)
    yb = jnp.dot(o_ref[...], wb_ref[...], preferred_element_type=F32)
    t = ga_ref[...].astype(F32) * ya + gb_ref[...].astype(F32) * yb
    t_ref[...] = t.astype(t_ref.dtype)


def _branch(a, o, wa, wb, g, tm=1024, tn=512):
    m = a.shape[0]
    nj = D_MODEL // tn
    return pl.pallas_call(
        _branch_kernel,
        grid=(m // tm, nj),
        in_specs=[
            pl.BlockSpec((tm, D_MODEL), lambda i, j: (i, 0)),
            pl.BlockSpec((tm, Q_WIDTH), lambda i, j: (i, 0)),
            pl.BlockSpec((D_MODEL, tn), lambda i, j: (0, j)),
            pl.BlockSpec((Q_WIDTH, tn), lambda i, j: (0, j)),
            pl.BlockSpec((tm, tn), lambda i, j: (i, j)),
            pl.BlockSpec((tm, tn), lambda i, j: (i, j + nj)),
        ],
        out_specs=pl.BlockSpec((tm, tn), lambda i, j: (i, j)),
        out_shape=jax.ShapeDtypeStruct((m, D_MODEL), BF16),
        compiler_params=_params(("arbitrary", "arbitrary")),
        name="branch",
    )(a, o, wa, wb, g, g)


def _out_kernel(n_prompt_blocks, t_ref, w_ref, xp_ref, xs_ref, x1_ref):
    i = pl.program_id(0)
    y = jnp.dot(t_ref[...], w_ref[...], preferred_element_type=F32)

    @pl.when(i < n_prompt_blocks)
    def _():
        x1_ref[...] = xp_ref[...] + y

    @pl.when(i >= n_prompt_blocks)
    def _():
        x1_ref[...] = xs_ref[...] + y


def _out(t, w_out, xp, xs, tm=1024, tn=1024):
    m = t.shape[0]
    npb = xp.shape[0] // tm
    return pl.pallas_call(
        functools.partial(_out_kernel, npb),
        grid=(m // tm, D_MODEL // tn),
        in_specs=[
            pl.BlockSpec((tm, D_MODEL), lambda i, j: (i, 0)),
            pl.BlockSpec((D_MODEL, tn), lambda i, j: (0, j)),
            pl.BlockSpec((tm, tn), lambda i, j: (jnp.minimum(i, npb - 1), j)),
            pl.BlockSpec((tm, tn), lambda i, j: (jnp.maximum(i - npb, 0), j)),
        ],
        out_specs=pl.BlockSpec((tm, tn), lambda i, j: (i, j)),
        out_shape=jax.ShapeDtypeStruct((m, D_MODEL), F32),
        compiler_params=_params(("arbitrary", "arbitrary")),
        name="out_proj",
    )(t, w_out, xp, xs)


def _ffn_kernel(x1_ref, ng_ref, wg_ref, wu_ref, wd_ref, fg_ref, y_ref, h_ref):
    f = pl.program_id(1)

    @pl.when(f == 0)
    def _():
        x1 = x1_ref[...]
        h_ref[...] = _rms(x1, ng_ref[...]).astype(h_ref.dtype)
        y_ref[...] = x1

    h = h_ref[...]
    gate = jnp.dot(h, wg_ref[...], preferred_element_type=F32)
    up = jnp.dot(h, wu_ref[...], preferred_element_type=F32)
    hid = (jax.nn.silu(gate) * up).astype(BF16)
    y_ref[...] += jnp.dot(hid, wd_ref[...], preferred_element_type=F32)

    @pl.when(f == pl.num_programs(1) - 1)
    def _():
        y_ref[...] = _rms(y_ref[...], fg_ref[...])


def _ffn(x1, row_block_offset, n_row_blocks, norm_g, wg, wu, wd, final_g, tm=512, tf=512):
    d_ff = wg.shape[1]
    return pl.pallas_call(
        _ffn_kernel,
        grid=(n_row_blocks, d_ff // tf),
        in_specs=[
            pl.BlockSpec((tm, D_MODEL), lambda i, f: (i + row_block_offset, 0)),
            pl.BlockSpec((1, D_MODEL), lambda i, f: (0, 0)),
            pl.BlockSpec((D_MODEL, tf), lambda i, f: (0, f)),
            pl.BlockSpec((D_MODEL, tf), lambda i, f: (0, f)),
            pl.BlockSpec((tf, D_MODEL), lambda i, f: (f, 0)),
            pl.BlockSpec((1, D_MODEL), lambda i, f: (0, 0)),
        ],
        out_specs=pl.BlockSpec((tm, D_MODEL), lambda i, f: (i, 0)),
        out_shape=jax.ShapeDtypeStruct((n_row_blocks * tm, D_MODEL), F32),
        scratch_shapes=[pltpu.VMEM((tm, D_MODEL), BF16)],
        compiler_params=_params(("arbitrary", "arbitrary")),
        name="ffn",
    )(x1, norm_g, wg, wu, wd, final_g)


def _rope_tables(seq, past_len, dec_batch, dec_seq):
    pos = jnp.concatenate([jnp.arange(seq, dtype=F32),
                           jnp.tile(past_len + jnp.arange(dec_seq, dtype=F32), dec_batch)])
    inv_freq = jnp.float32(ROPE_THETA) ** (-(jnp.arange(ROT_HALF, dtype=F32) * 2.0 / ROT_DIM))
    ang = pos[:, None] * inv_freq[None, :]
    cos, sin = jnp.cos(ang), jnp.sin(ang)
    n = pos.shape[0]
    pad = jnp.zeros((n, HEAD_DIM - ROT_DIM), F32)
    zero = jnp.zeros((n, ROT_HALF), F32)
    cos_h = jnp.concatenate([cos, cos, pad + 1.0], axis=1)
    sa_h = jnp.concatenate([zero, sin, pad], axis=1)
    sb_h = jnp.concatenate([-sin, zero, pad], axis=1)
    rep = LANES // HEAD_DIM
    return jnp.tile(cos_h, (1, rep)), jnp.tile(sa_h, (1, rep)), jnp.tile(sb_h, (1, rep))


def kernel(x_prompt, x_sample, cache_swa_k, cache_swa_v, norm_mix_g, w_in, gmlp_ln_g, gmlp_ln_b,
           gmlp_ws, gmlp_bs, attn_sinks, w_gate, b_gate, w_branch_a, w_branch_b, w_out,
           norm_ffn_g, w_ffn_gate, w_ffn_up, w_ffn_down, final_norm_g):
    batch, seq, _ = x_prompt.shape
    dec_batch, dec_seq, _ = x_sample.shape
    depth = w_in.shape[0]
    past_len = 2048
    assert depth == 1 and cache_swa_k.shape[2] == WINDOW and dec_seq == CHUNK
    mp, ms = batch * seq, dec_batch * dec_seq

    xp = x_prompt.reshape(mp, D_MODEL)
    xs = x_sample.reshape(ms, D_MODEL)
    cos_t, sa_t, sb_t = _rope_tables(seq, past_len, dec_batch, dec_seq)
    row = lambda p: p.reshape(1, -1)

    l = 0
    xn = _norm(xp, xs, row(norm_mix_g[l]))
    g, w_in_b = _proj_gate(xn, w_gate[l], row(b_gate[l]), [w_in[l]])
    u, v, q, k, va = _proj_in(xn, w_in_b, cos_t, sa_t, sb_t, mp, seq)
    a, vn_s = _gmlp(u, v, row(gmlp_ln_g[l]), row(gmlp_ln_b[l]), gmlp_ws[l], gmlp_bs[l].T, mp)
    o, wa_b, wb_b, wo_b, wg, wu, wd = _attn_prompt(
        attn_sinks[l], q, k, va, batch, seq,
        [w_branch_a[l], w_branch_b[l], w_out[l], w_ffn_gate[l], w_ffn_up[l], w_ffn_down[l]])
    o = _attn_sample(attn_sinks[l], q, k, va,
                     cache_swa_k[l].reshape(dec_batch, WINDOW, KV_WIDTH),
                     cache_swa_v[l].reshape(dec_batch, WINDOW, KV_WIDTH),
                     o, mp, dec_batch, dec_seq)
    t = _branch(a, o, wa_b, wb_b, g)
    x1 = _out(t, wo_b, xp, xs)
    ffn_tm = 1024
    ffn = functools.partial(_ffn, norm_g=row(norm_ffn_g[l]), wg=wg, wu=wu, wd=wd,
                            final_g=row(final_norm_g), tm=ffn_tm)
    y_prompt = ffn(x1, 0, mp // ffn_tm)
    y_sample = ffn(x1, mp // ffn_tm, ms // ffn_tm)

    keep = min(WINDOW, seq)
    tail = lambda z: jnp.stack([z[(b + 1) * seq - keep:(b + 1) * seq] for b in range(batch)]).reshape(
        batch, keep, N_KV_HEADS, HEAD_DIM)
    kp, vp = tail(k), tail(va)
    return (
        y_prompt.reshape(batch, seq, D_MODEL),
        y_sample.reshape(dec_batch, dec_seq, D_MODEL),
        kp[None],
        vp[None],
        k[mp:].reshape(1, dec_batch, dec_seq, N_KV_HEADS, HEAD_DIM),
        va[mp:].reshape(1, dec_batch, dec_seq, N_KV_HEADS, HEAD_DIM),
        vn_s.reshape(1, dec_batch, dec_seq, D_MODEL),
    )
```

```python
import functools

import jax
import jax.numpy as jnp
from jax import lax
from jax.experimental import pallas as pl
from jax.experimental.pallas import tpu as pltpu

D_MODEL = 2048
CHUNK = 64
GMLP_CHUNK = 128
GMLP_GROUPS = 8
GMLP_GROUP_DIM = D_MODEL // GMLP_GROUPS
N_HEADS = 32
N_KV_HEADS = 4
HEAD_DIM = 64
Q_REP = N_HEADS // N_KV_HEADS
WINDOW = 128
ROPE_THETA = 500000.0
ROT_DIM = HEAD_DIM // 4
ROT_HALF = ROT_DIM // 2
Q_WIDTH = N_HEADS * HEAD_DIM
KV_WIDTH = N_KV_HEADS * HEAD_DIM
EPS = 1e-6
NEG = -1e30
LANES = 128
LOG2E = 1.4426950408889634
Q_SCALE = HEAD_DIM ** -0.5 * LOG2E

F32 = jnp.float32
BF16 = jnp.bfloat16

VMEM_LIMIT = 56 * 1024 * 1024


def _params(semantics):
    return pltpu.CompilerParams(dimension_semantics=semantics, vmem_limit_bytes=VMEM_LIMIT)


def _rms(x, g):
    return x * lax.rsqrt(jnp.mean(x * x, axis=-1, keepdims=True) + EPS) * g


BF16_SUBLANES = 16


def _rider_specs(weights, n_steps, step_of):
    counts, in_specs, out_specs, out_shapes = [], [], [], []
    for w in weights:
        rows, cols = w.shape
        nb = max(n for n in range(1, n_steps + 1)
                 if rows % n == 0 and (rows // n) % BF16_SUBLANES == 0)
        idx = lambda *g, nb=nb: (jnp.minimum(step_of(*g), nb - 1), 0)
        counts.append(nb)
        in_specs.append(pl.BlockSpec((rows // nb, cols), idx))
        out_specs.append(pl.BlockSpec((rows // nb, cols), idx))
        out_shapes.append(jax.ShapeDtypeStruct(w.shape, BF16))
    return counts, in_specs, out_specs, out_shapes


def _rider_cast(step, counts, src_refs, dst_refs):
    for nb, src, dst in zip(counts, src_refs, dst_refs):
        @pl.when(step < nb)
        def _():
            dst[...] = src[...].astype(dst.dtype)


def _norm_kernel(n_prompt_blocks, xp_ref, xs_ref, g_ref, o_ref):
    i = pl.program_id(0)

    @pl.when(i < n_prompt_blocks)
    def _():
        o_ref[...] = _rms(xp_ref[...], g_ref[...]).astype(BF16)

    @pl.when(i >= n_prompt_blocks)
    def _():
        o_ref[...] = _rms(xs_ref[...], g_ref[...]).astype(BF16)


def _norm(xp, xs, g, tr=512):
    mp, ms = xp.shape[0], xs.shape[0]
    npb, nsb = mp // tr, ms // tr
    return pl.pallas_call(
        functools.partial(_norm_kernel, npb),
        grid=(npb + nsb,),
        in_specs=[
            pl.BlockSpec((tr, D_MODEL), lambda i: (jnp.minimum(i, npb - 1), 0)),
            pl.BlockSpec((tr, D_MODEL), lambda i: (jnp.maximum(i - npb, 0), 0)),
            pl.BlockSpec((1, D_MODEL), lambda i: (0, 0)),
        ],
        out_specs=pl.BlockSpec((tr, D_MODEL), lambda i: (i, 0)),
        out_shape=jax.ShapeDtypeStruct((mp + ms, D_MODEL), BF16),
        compiler_params=_params(("arbitrary",)),
        name="norm",
    )(xp, xs, g)


def _rope(h, cos, sa, sb):
    return h * cos + pltpu.roll(h, ROT_HALF, 1) * sa + pltpu.roll(h, LANES - ROT_HALF, 1) * sb


def _proj_in_kernel(nu, nq, xn_ref, win_ref, wkv_ref, cos_ref, sa_ref, sb_ref,
                    u_ref, v_ref, q_ref, k_ref, va_ref):
    j = pl.program_id(1)
    xn = xn_ref[...]

    def mm(w_ref):
        return jnp.dot(xn, w_ref[...], preferred_element_type=F32)

    def rope_cols(h, scale):
        cos, sa, sb = cos_ref[...], sa_ref[...], sb_ref[...]
        parts = [_rope(h[:, c:c + LANES], cos, sa, sb) * scale for c in range(0, h.shape[1], LANES)]
        return jnp.concatenate(parts, axis=1)

    @pl.when(j < nu)
    def _():
        u_ref[...] = jax.nn.gelu(mm(win_ref)).astype(u_ref.dtype)

    @pl.when((j >= nu) & (j < 2 * nu))
    def _():
        v_ref[...] = jax.nn.gelu(mm(win_ref)).astype(v_ref.dtype)

    @pl.when((j >= 2 * nu) & (j < 2 * nu + nq))
    def _():
        q_ref[...] = rope_cols(mm(win_ref), Q_SCALE).astype(q_ref.dtype)

    @pl.when(j == 2 * nu + nq)
    def _():
        h = mm(wkv_ref)
        k_ref[...] = rope_cols(h[:, :KV_WIDTH], 1.0)
        va_ref[...] = h[:, KV_WIDTH:]


def _proj_in(xn, w_in, cos_t, sa_t, sb_t, n_prompt_rows, seq, tm=1024, tn=1024):
    m = xn.shape[0]
    nu = D_MODEL // tn
    nq = Q_WIDTH // tn
    n_wide = 2 * nu + nq
    kv_block = (2 * D_MODEL + Q_WIDTH) // (2 * KV_WIDTH)
    npb = n_prompt_rows // tm
    blocks_per_seq = seq // tm

    def tab_idx(i, j):
        return (jnp.where(i < npb, i % blocks_per_seq, blocks_per_seq), 0)

    clip = lambda x, lo, hi: jnp.minimum(jnp.maximum(x, lo), hi)
    tab_spec = pl.BlockSpec((tm, LANES), tab_idx)
    out_shapes = (
        jax.ShapeDtypeStruct((m, D_MODEL), BF16),
        jax.ShapeDtypeStruct((m, D_MODEL), BF16),
        jax.ShapeDtypeStruct((m, Q_WIDTH), BF16),
        jax.ShapeDtypeStruct((m, KV_WIDTH), F32),
        jax.ShapeDtypeStruct((m, KV_WIDTH), F32),
    )
    return pl.pallas_call(
        functools.partial(_proj_in_kernel, nu, nq),
        grid=(m // tm, n_wide + 1),
        in_specs=[
            pl.BlockSpec((tm, D_MODEL), lambda i, j: (i, 0)),
            pl.BlockSpec((D_MODEL, tn), lambda i, j: (0, jnp.minimum(j, n_wide - 1))),
            pl.BlockSpec((D_MODEL, 2 * KV_WIDTH), lambda i, j: (0, kv_block)),
            tab_spec, tab_spec, tab_spec,
        ],
        out_specs=(
            pl.BlockSpec((tm, tn), lambda i, j: (i, jnp.minimum(j, nu - 1))),
            pl.BlockSpec((tm, tn), lambda i, j: (i, clip(j - nu, 0, nu - 1))),
            pl.BlockSpec((tm, tn), lambda i, j: (i, clip(j - 2 * nu, 0, nq - 1))),
            pl.BlockSpec((tm, KV_WIDTH), lambda i, j: (i, 0)),
            pl.BlockSpec((tm, KV_WIDTH), lambda i, j: (i, 0)),
        ),
        out_shape=out_shapes,
        compiler_params=_params(("arbitrary", "arbitrary")),
        name="proj_in",
    )(xn, w_in, w_in, cos_t, sa_t, sb_t)


def _proj_gate_kernel(counts, xn_ref, w_ref, b_ref, *refs):
    n = len(counts)
    riders, g_ref, cast = refs[:n], refs[n], refs[n + 1:]
    h = jnp.dot(xn_ref[...], w_ref[...].astype(BF16), preferred_element_type=F32)
    g_ref[...] = jax.nn.sigmoid(h + b_ref[...]).astype(g_ref.dtype)
    _rider_cast(pl.program_id(0) * pl.num_programs(1) + pl.program_id(1), counts, riders, cast)


def _proj_gate(xn, w_gate, b_gate, riders, tm=1024, tn=1024):
    m = xn.shape[0]
    n = w_gate.shape[1]
    nj = n // tn
    counts, r_in, r_out, r_shapes = _rider_specs(riders, (m // tm) * nj, lambda i, j: i * nj + j)
    return pl.pallas_call(
        functools.partial(_proj_gate_kernel, counts),
        grid=(m // tm, nj),
        in_specs=[
            pl.BlockSpec((tm, D_MODEL), lambda i, j: (i, 0)),
            pl.BlockSpec((D_MODEL, tn), lambda i, j: (0, j)),
            pl.BlockSpec((1, tn), lambda i, j: (0, j)),
            *r_in,
        ],
        out_specs=(pl.BlockSpec((tm, tn), lambda i, j: (i, j)), *r_out),
        out_shape=(jax.ShapeDtypeStruct((m, n), BF16), *r_shapes),
        compiler_params=_params(("arbitrary", "arbitrary")),
        name="proj_gate",
    )(xn, w_gate, b_gate, *riders)


def _gmlp_kernel(n_prompt_blocks, tr, u_ref, v_ref, lng_ref, lnb_ref, ws_ref, bst_ref,
                 a_ref, vn_ref):
    i = pl.program_id(0)
    v = v_ref[...].astype(F32)
    mu = jnp.mean(v, axis=-1, keepdims=True)
    vc = v - mu
    var = jnp.mean(vc * vc, axis=-1, keepdims=True)
    vn = vc * lax.rsqrt(var + EPS) * lng_ref[...] + lnb_ref[...]
    vnb = vn.astype(BF16)

    def mix(length):
        r = lax.broadcasted_iota(jnp.int32, (length, length), 0) // CHUNK
        c = lax.broadcasted_iota(jnp.int32, (length, length), 1) // CHUNK
        for g in range(GMLP_GROUPS):
            w = jnp.where(c <= r, ws_ref[g, :length, :length], 0.0).astype(BF16)
            bias = bst_ref[:length, g:g + 1]
            cols = slice(g * GMLP_GROUP_DIM, (g + 1) * GMLP_GROUP_DIM)
            for b in range(tr // length):
                rows = slice(b * length, (b + 1) * length)
                s = jnp.dot(w, vnb[rows, cols], preferred_element_type=F32) + bias
                a_ref[rows, cols] = (u_ref[rows, cols].astype(F32) * s).astype(a_ref.dtype)

    @pl.when(i < n_prompt_blocks)
    def _():
        mix(GMLP_CHUNK)

    @pl.when(i >= n_prompt_blocks)
    def _():
        vn_ref[...] = vn
        mix(CHUNK)


def _gmlp(u, v, ln_g, ln_b, ws, bs_t, n_prompt_rows, tr=512):
    m = u.shape[0]
    npb = n_prompt_rows // tr
    return pl.pallas_call(
        functools.partial(_gmlp_kernel, npb, tr),
        grid=(m // tr,),
        in_specs=[
            pl.BlockSpec((tr, D_MODEL), lambda i: (i, 0)),
            pl.BlockSpec((tr, D_MODEL), lambda i: (i, 0)),
            pl.BlockSpec((1, D_MODEL), lambda i: (0, 0)),
            pl.BlockSpec((1, D_MODEL), lambda i: (0, 0)),
            pl.BlockSpec((GMLP_GROUPS, GMLP_CHUNK, GMLP_CHUNK), lambda i: (0, 0, 0)),
            pl.BlockSpec((GMLP_CHUNK, GMLP_GROUPS), lambda i: (0, 0)),
        ],
        out_specs=(
            pl.BlockSpec((tr, D_MODEL), lambda i: (i, 0)),
            pl.BlockSpec((tr, D_MODEL), lambda i: (jnp.maximum(i - npb, 0), 0)),
        ),
        out_shape=(
            jax.ShapeDtypeStruct((m, D_MODEL), BF16),
            jax.ShapeDtypeStruct((m - n_prompt_rows, D_MODEL), F32),
        ),
        compiler_params=_params(("arbitrary",)),
        name="gmlp",
    )(u, v, ln_g, ln_b, ws, bs_t)


def _dup_head(pair, pair_swapped, low_half, odd):
    if odd:
        return jnp.where(low_half, pair_swapped, pair).astype(BF16)
    return jnp.where(low_half, pair, pair_swapped).astype(BF16)


def _attend(q_ref, kwin, vwin, mask, sink_ref, o_ref):
    rows, keys = q_ref.shape[0], kwin.shape[0]
    ones = jnp.ones((keys, LANES), BF16)
    low_q = lax.broadcasted_iota(jnp.int32, (rows, LANES), 1) < HEAD_DIM
    low_k = lax.broadcasted_iota(jnp.int32, (keys, LANES), 1) < HEAD_DIM
    heads_per_pair = LANES // HEAD_DIM
    for c in range(N_KV_HEADS // heads_per_pair):
        kpair = kwin[:, c * LANES:(c + 1) * LANES]
        vpair = vwin[:, c * LANES:(c + 1) * LANES]
        kswap = pltpu.roll(kpair, HEAD_DIM, 1)
        vswap = pltpu.roll(vpair, HEAD_DIM, 1)
        for odd in range(heads_per_pair):
            hk = c * heads_per_pair + odd
            kdup = _dup_head(kpair, kswap, low_k, odd)
            vaug = jnp.concatenate([_dup_head(vpair, vswap, low_k, odd), ones], axis=1)
            parts = []
            for r in range(Q_REP):
                h = hk * Q_REP + r
                qpair = q_ref[:, (h // 2) * LANES:(h // 2 + 1) * LANES]
                keep = low_q if h % 2 == 0 else jnp.logical_not(low_q)
                parts.append(jnp.where(keep, qpair, jnp.zeros_like(qpair)))
            s_all = lax.dot_general(jnp.concatenate(parts, axis=0), kdup, (((1,), (1,)), ((), ())),
                                    preferred_element_type=F32)
            ps, sinks, mxs = [], [], []
            for r in range(Q_REP):
                s = s_all[r * rows:(r + 1) * rows]
                if mask is not None:
                    s = mask(s)
                sink = sink_ref[hk * Q_REP + r] * LOG2E
                mx = jnp.maximum(jnp.max(s, axis=-1, keepdims=True), sink)
                ps.append(jnp.exp2(s - mx).astype(BF16))
                sinks.append(sink)
                mxs.append(mx)
            o_all = jnp.dot(jnp.concatenate(ps, axis=0), vaug, preferred_element_type=F32)
            for r in range(0, Q_REP, 2):
                h = hk * Q_REP + r
                even, oddh = o_all[r * rows:(r + 1) * rows], o_all[(r + 1) * rows:(r + 2) * rows]
                esink = jnp.exp2(jnp.where(low_q, sinks[r], sinks[r + 1])
                                 - jnp.where(low_q, mxs[r], mxs[r + 1]))
                den = jnp.where(low_q, even[:, LANES:], oddh[:, LANES:]) + esink
                num = jnp.where(low_q, even[:, :LANES], oddh[:, :LANES])
                o_ref[:, (h // 2) * LANES:(h // 2 + 1) * LANES] = (num / den).astype(o_ref.dtype)


def _attn_prompt_kernel(tq, counts, sink_ref, q_ref, kp_ref, kc_ref, vp_ref, vc_ref, *refs):
    n = len(counts)
    riders, o_ref, cast = refs[:n], refs[n], refs[n + 1:]
    t = pl.program_id(1)
    _rider_cast(pl.program_id(0) * pl.num_programs(1) + t, counts, riders, cast)
    kwin = jnp.concatenate([kp_ref[...], kc_ref[...]], axis=0)
    vwin = jnp.concatenate([vp_ref[...], vc_ref[...]], axis=0)
    row = lax.broadcasted_iota(jnp.int32, (tq, tq), 0)
    col = lax.broadcasted_iota(jnp.int32, (tq, tq), 1)
    valid_prev = ((row < CHUNK) | (col >= CHUNK)) & (t > 0)
    valid_cur_top = lax.broadcasted_iota(jnp.int32, (CHUNK, tq), 1) < CHUNK

    def mask(s):
        prev = jnp.where(valid_prev, s[:, :tq], NEG)
        cur_top = jnp.where(valid_cur_top, s[:CHUNK, tq:], NEG)
        cur = jnp.concatenate([cur_top, s[CHUNK:, tq:]], axis=0)
        return jnp.concatenate([prev, cur], axis=1)

    _attend(q_ref, kwin, vwin, mask, sink_ref, o_ref)


def _attn_prompt(sinks, q, k, va, batch, seq, riders, tq=128):
    m = q.shape[0]
    nt = seq // tq
    assert tq == WINDOW
    cur = lambda b, t: (b * nt + t, 0)
    prev = lambda b, t: (b * nt + jnp.maximum(t - 1, 0), 0)
    counts, r_in, r_out, r_shapes = _rider_specs(riders, batch * nt, lambda b, t: b * nt + t)
    return pl.pallas_call(
        functools.partial(_attn_prompt_kernel, tq, counts),
        grid=(batch, nt),
        in_specs=[
            pl.BlockSpec(memory_space=pltpu.SMEM),
            pl.BlockSpec((tq, Q_WIDTH), cur),
            pl.BlockSpec((tq, KV_WIDTH), prev),
            pl.BlockSpec((tq, KV_WIDTH), cur),
            pl.BlockSpec((tq, KV_WIDTH), prev),
            pl.BlockSpec((tq, KV_WIDTH), cur),
            *r_in,
        ],
        out_specs=(pl.BlockSpec((tq, Q_WIDTH), cur), *r_out),
        out_shape=(jax.ShapeDtypeStruct((m, Q_WIDTH), BF16), *r_shapes),
        compiler_params=_params(("arbitrary", "arbitrary")),
        name="attn_prompt",
    )(sinks, q, k, k, va, va, *riders)


def _attn_sample_kernel(sink_ref, q_ref, ck_ref, kn_ref, cv_ref, vn_ref, o_in_ref, o_ref):
    del o_in_ref
    kwin = jnp.concatenate([ck_ref[...], kn_ref[...]], axis=0)
    vwin = jnp.concatenate([cv_ref[...], vn_ref[...]], axis=0)
    _attend(q_ref, kwin, vwin, None, sink_ref, o_ref)


def _attn_sample(sinks, q, k, va, cache_k, cache_v, o, n_prompt_rows, dec_batch, dec_seq):
    first = n_prompt_rows // dec_seq
    new = lambda b: (first + b, 0)
    cached = lambda b: (b, 0, 0)
    cache_len = cache_k.shape[1]
    return pl.pallas_call(
        _attn_sample_kernel,
        grid=(dec_batch,),
        in_specs=[
            pl.BlockSpec(memory_space=pltpu.SMEM),
            pl.BlockSpec((dec_seq, Q_WIDTH), new),
            pl.BlockSpec((None, cache_len, KV_WIDTH), cached),
            pl.BlockSpec((dec_seq, KV_WIDTH), new),
            pl.BlockSpec((None, cache_len, KV_WIDTH), cached),
            pl.BlockSpec((dec_seq, KV_WIDTH), new),
            pl.BlockSpec(memory_space=pl.ANY),
        ],
        out_specs=pl.BlockSpec((dec_seq, Q_WIDTH), new),
        out_shape=jax.ShapeDtypeStruct(o.shape, o.dtype),
        input_output_aliases={6: 0},
        compiler_params=_params(("arbitrary",)),
        name="attn_sample",
    )(sinks, q, cache_k, k, cache_v, va, o)


def _branch_kernel(a_ref, o_ref, wa_ref, wb_ref, ga_ref, gb_ref, t_ref):
    ya = jnp.dot(a_ref[...], wa_ref[...], preferred_element_type=F32)
    yb = jnp.dot(o_ref[...], wb_ref[...], preferred_element_type=F32)
    t = ga_ref[...].astype(F32) * ya + gb_ref[...].astype(F32) * yb
    t_ref[...] = t.astype(t_ref.dtype)


def _branch(a, o, wa, wb, g, tm=1024, tn=512):
    m = a.shape[0]
    nj = D_MODEL // tn
    return pl.pallas_call(
        _branch_kernel,
        grid=(m // tm, nj),
        in_specs=[
            pl.BlockSpec((tm, D_MODEL), lambda i, j: (i, 0)),
            pl.BlockSpec((tm, Q_WIDTH), lambda i, j: (i, 0)),
            pl.BlockSpec((D_MODEL, tn), lambda i, j: (0, j)),
            pl.BlockSpec((Q_WIDTH, tn), lambda i, j: (0, j)),
            pl.BlockSpec((tm, tn), lambda i, j: (i, j)),
            pl.BlockSpec((tm, tn), lambda i, j: (i, j + nj)),
        ],
        out_specs=pl.BlockSpec((tm, tn), lambda i, j: (i, j)),
        out_shape=jax.ShapeDtypeStruct((m, D_MODEL), BF16),
        compiler_params=_params(("arbitrary", "arbitrary")),
        name="branch",
    )(a, o, wa, wb, g, g)


def _out_kernel(n_prompt_blocks, t_ref, w_ref, xp_ref, xs_ref, x1_ref):
    i = pl.program_id(0)
    y = jnp.dot(t_ref[...], w_ref[...], preferred_element_type=F32)

    @pl.when(i < n_prompt_blocks)
    def _():
        x1_ref[...] = xp_ref[...] + y

    @pl.when(i >= n_prompt_blocks)
    def _():
        x1_ref[...] = xs_ref[...] + y


def _out(t, w_out, xp, xs, tm=1024, tn=1024):
    m = t.shape[0]
    npb = xp.shape[0] // tm
    return pl.pallas_call(
        functools.partial(_out_kernel, npb),
        grid=(m // tm, D_MODEL // tn),
        in_specs=[
            pl.BlockSpec((tm, D_MODEL), lambda i, j: (i, 0)),
            pl.BlockSpec((D_MODEL, tn), lambda i, j: (0, j)),
            pl.BlockSpec((tm, tn), lambda i, j: (jnp.minimum(i, npb - 1), j)),
            pl.BlockSpec((tm, tn), lambda i, j: (jnp.maximum(i - npb, 0), j)),
        ],
        out_specs=pl.BlockSpec((tm, tn), lambda i, j: (i, j)),
        out_shape=jax.ShapeDtypeStruct((m, D_MODEL), F32),
        compiler_params=_params(("arbitrary", "arbitrary")),
        name="out_proj",
    )(t, w_out, xp, xs)


def _ffn_kernel(x1_ref, ng_ref, wg_ref, wu_ref, wd_ref, fg_ref, y_ref, h_ref):
    f = pl.program_id(1)

    @pl.when(f == 0)
    def _():
        x1 = x1_ref[...]
        h_ref[...] = _rms(x1, ng_ref[...]).astype(h_ref.dtype)
        y_ref[...] = x1

    h = h_ref[...]
    gate = jnp.dot(h, wg_ref[...], preferred_element_type=F32)
    up = jnp.dot(h, wu_ref[...], preferred_element_type=F32)
    hid = (jax.nn.silu(gate) * up).astype(BF16)
    y_ref[...] += jnp.dot(hid, wd_ref[...], preferred_element_type=F32)

    @pl.when(f == pl.num_programs(1) - 1)
    def _():
        y_ref[...] = _rms(y_ref[...], fg_ref[...])


def _ffn(x1, row_block_offset, n_row_blocks, norm_g, wg, wu, wd, final_g, tm=512, tf=512):
    d_ff = wg.shape[1]
    return pl.pallas_call(
        _ffn_kernel,
        grid=(n_row_blocks, d_ff // tf),
        in_specs=[
            pl.BlockSpec((tm, D_MODEL), lambda i, f: (i + row_block_offset, 0)),
            pl.BlockSpec((1, D_MODEL), lambda i, f: (0, 0)),
            pl.BlockSpec((D_MODEL, tf), lambda i, f: (0, f)),
            pl.BlockSpec((D_MODEL, tf), lambda i, f: (0, f)),
            pl.BlockSpec((tf, D_MODEL), lambda i, f: (f, 0)),
            pl.BlockSpec((1, D_MODEL), lambda i, f: (0, 0)),
        ],
        out_specs=pl.BlockSpec((tm, D_MODEL), lambda i, f: (i, 0)),
        out_shape=jax.ShapeDtypeStruct((n_row_blocks * tm, D_MODEL), F32),
        scratch_shapes=[pltpu.VMEM((tm, D_MODEL), BF16)],
        compiler_params=_params(("arbitrary", "arbitrary")),
        name="ffn",
    )(x1, norm_g, wg, wu, wd, final_g)


def _rope_tables(seq, past_len, dec_batch, dec_seq):
    pos = jnp.concatenate([jnp.arange(seq, dtype=F32),
                           jnp.tile(past_len + jnp.arange(dec_seq, dtype=F32), dec_batch)])
    inv_freq = jnp.float32(ROPE_THETA) ** (-(jnp.arange(ROT_HALF, dtype=F32) * 2.0 / ROT_DIM))
    ang = pos[:, None] * inv_freq[None, :]
    cos, sin = jnp.cos(ang), jnp.sin(ang)
    n = pos.shape[0]
    pad = jnp.zeros((n, HEAD_DIM - ROT_DIM), F32)
    zero = jnp.zeros((n, ROT_HALF), F32)
    cos_h = jnp.concatenate([cos, cos, pad + 1.0], axis=1)
    sa_h = jnp.concatenate([zero, sin, pad], axis=1)
    sb_h = jnp.concatenate([-sin, zero, pad], axis=1)
    rep = LANES // HEAD_DIM
    return jnp.tile(cos_h, (1, rep)), jnp.tile(sa_h, (1, rep)), jnp.tile(sb_h, (1, rep))


def kernel(x_prompt, x_sample, cache_swa_k, cache_swa_v, norm_mix_g, w_in, gmlp_ln_g, gmlp_ln_b,
           gmlp_ws, gmlp_bs, attn_sinks, w_gate, b_gate, w_branch_a, w_branch_b, w_out,
           norm_ffn_g, w_ffn_gate, w_ffn_up, w_ffn_down, final_norm_g):
    batch, seq, _ = x_prompt.shape
    dec_batch, dec_seq, _ = x_sample.shape
    depth = w_in.shape[0]
    past_len = 2048
    assert depth == 1 and cache_swa_k.shape[2] == WINDOW and dec_seq == CHUNK
    mp, ms = batch * seq, dec_batch * dec_seq

    xp = x_prompt.reshape(mp, D_MODEL)
    xs = x_sample.reshape(ms, D_MODEL)
    cos_t, sa_t, sb_t = _rope_tables(seq, past_len, dec_batch, dec_seq)
    row = lambda p: p.reshape(1, -1)

    l = 0
    xn = _norm(xp, xs, row(norm_mix_g[l]))
    g, w_in_b = _proj_gate(xn, w_gate[l], row(b_gate[l]), [w_in[l]])
    u, v, q, k, va = _proj_in(xn, w_in_b, cos_t, sa_t, sb_t, mp, seq)
    a, vn_s = _gmlp(u, v, row(gmlp_ln_g[l]), row(gmlp_ln_b[l]), gmlp_ws[l], gmlp_bs[l].T, mp)
    o, wa_b, wb_b, wo_b, wg, wu, wd = _attn_prompt(
        attn_sinks[l], q, k, va, batch, seq,
        [w_branch_a[l], w_branch_b[l], w_out[l], w_ffn_gate[l], w_ffn_up[l], w_ffn_down[l]])
    o = _attn_sample(attn_sinks[l], q, k, va,
                     cache_swa_k[l].reshape(dec_batch, WINDOW, KV_WIDTH),
                     cache_swa_v[l].reshape(dec_batch, WINDOW, KV_WIDTH),
                     o, mp, dec_batch, dec_seq)
    t = _branch(a, o, wa_b, wb_b, g)
    x1 = _out(t, wo_b, xp, xs)
    ffn_tm = 1024
    ffn = functools.partial(_ffn, norm_g=row(norm_ffn_g[l]), wg=wg, wu=wu, wd=wd,
                            final_g=row(final_norm_g), tm=ffn_tm)
    y_prompt = ffn(x1, 0, mp // ffn_tm)
    y_sample = ffn(x1, mp // ffn_tm, ms // ffn_tm)

    keep = min(WINDOW, seq)
    tail = lambda z: jnp.stack([z[(b + 1) * seq - keep:(b + 1) * seq] for b in range(batch)]).reshape(
        batch, keep, N_KV_HEADS, HEAD_DIM)
    kp, vp = tail(k), tail(va)
    return (
        y_prompt.reshape(batch, seq, D_MODEL),
        y_sample.reshape(dec_batch, dec_seq, D_MODEL),
        kp[None],
        vp[None],
        k[mp:].reshape(1, dec_batch, dec_seq, N_KV_HEADS, HEAD_DIM),
        va[mp:].reshape(1, dec_batch, dec_seq, N_KV_HEADS, HEAD_DIM),
        vn_s.reshape(1, dec_batch, dec_seq, D_MODEL),
    )
```

```python
import functools

import jax
import jax.numpy as jnp
from jax import lax
from jax.experimental import pallas as pl
from jax.experimental.pallas import tpu as pltpu

D_MODEL = 2048
CHUNK = 64
GMLP_CHUNK = 128
GMLP_GROUPS = 8
GMLP_GROUP_DIM = D_MODEL // GMLP_GROUPS
N_HEADS = 32
N_KV_HEADS = 4
HEAD_DIM = 64
Q_REP = N_HEADS // N_KV_HEADS
WINDOW = 128
ROPE_THETA = 500000.0
ROT_DIM = HEAD_DIM // 4
ROT_HALF = ROT_DIM // 2
Q_WIDTH = N_HEADS * HEAD_DIM
KV_WIDTH = N_KV_HEADS * HEAD_DIM
EPS = 1e-6
NEG = -1e30
LANES = 128
LOG2E = 1.4426950408889634
Q_SCALE = HEAD_DIM ** -0.5 * LOG2E

F32 = jnp.float32
BF16 = jnp.bfloat16

VMEM_LIMIT = 56 * 1024 * 1024


def _params(semantics):
    return pltpu.CompilerParams(dimension_semantics=semantics, vmem_limit_bytes=VMEM_LIMIT)


def _rms(x, g):
    return x * lax.rsqrt(jnp.mean(x * x, axis=-1, keepdims=True) + EPS) * g


BF16_SUBLANES = 16


def _rider_specs(weights, n_steps, step_of):
    counts, in_specs, out_specs, out_shapes = [], [], [], []
    for w in weights:
        rows, cols = w.shape
        nb = max(n for n in range(1, n_steps + 1)
                 if rows % n == 0 and (rows // n) % BF16_SUBLANES == 0)
        idx = lambda *g, nb=nb: (jnp.minimum(step_of(*g), nb - 1), 0)
        counts.append(nb)
        in_specs.append(pl.BlockSpec((rows // nb, cols), idx))
        out_specs.append(pl.BlockSpec((rows // nb, cols), idx))
        out_shapes.append(jax.ShapeDtypeStruct(w.shape, BF16))
    return counts, in_specs, out_specs, out_shapes


def _rider_cast(step, counts, src_refs, dst_refs):
    for nb, src, dst in zip(counts, src_refs, dst_refs):
        @pl.when(step < nb)
        def _():
            dst[...] = src[...].astype(dst.dtype)


def _norm_kernel(n_prompt_blocks, xp_ref, xs_ref, g_ref, o_ref):
    i = pl.program_id(0)

    @pl.when(i < n_prompt_blocks)
    def _():
        o_ref[...] = _rms(xp_ref[...], g_ref[...]).astype(BF16)

    @pl.when(i >= n_prompt_blocks)
    def _():
        o_ref[...] = _rms(xs_ref[...], g_ref[...]).astype(BF16)


def _norm(xp, xs, g, tr=512):
    mp, ms = xp.shape[0], xs.shape[0]
    npb, nsb = mp // tr, ms // tr
    return pl.pallas_call(
        functools.partial(_norm_kernel, npb),
        grid=(npb + nsb,),
        in_specs=[
            pl.BlockSpec((tr, D_MODEL), lambda i: (jnp.minimum(i, npb - 1), 0)),
            pl.BlockSpec((tr, D_MODEL), lambda i: (jnp.maximum(i - npb, 0), 0)),
            pl.BlockSpec((1, D_MODEL), lambda i: (0, 0)),
        ],
        out_specs=pl.BlockSpec((tr, D_MODEL), lambda i: (i, 0)),
        out_shape=jax.ShapeDtypeStruct((mp + ms, D_MODEL), BF16),
        compiler_params=_params(("arbitrary",)),
        name="norm",
    )(xp, xs, g)


def _rope(h, cos, sa, sb):
    return h * cos + pltpu.roll(h, ROT_HALF, 1) * sa + pltpu.roll(h, LANES - ROT_HALF, 1) * sb


def _proj_in_kernel(n_uv, xn_ref, win_ref, wkv_ref, cos_ref, sa_ref, sb_ref,
                    uv_ref, q_ref, k_ref, va_ref):
    j = pl.program_id(1)
    xn = xn_ref[...]

    def mm(w_ref):
        return jnp.dot(xn, w_ref[...], preferred_element_type=F32)

    def rope_cols(h, scale):
        cos, sa, sb = cos_ref[...], sa_ref[...], sb_ref[...]
        parts = [_rope(h[:, c:c + LANES], cos, sa, sb) * scale for c in range(0, h.shape[1], LANES)]
        return jnp.concatenate(parts, axis=1)

    @pl.when(j == 0)
    def _():
        h = mm(wkv_ref)
        k_ref[...] = rope_cols(h[:, :KV_WIDTH], 1.0)
        va_ref[...] = h[:, KV_WIDTH:]

    @pl.when((j >= 1) & (j <= n_uv))
    def _():
        uv_ref[...] = jax.nn.gelu(mm(win_ref)).astype(uv_ref.dtype)

    @pl.when(j > n_uv)
    def _():
        q_ref[...] = rope_cols(mm(win_ref), Q_SCALE).astype(q_ref.dtype)


def _proj_in(xn, w_in, cos_t, sa_t, sb_t, n_prompt_rows, seq, tm=1024, tn=1024):
    m = xn.shape[0]
    n_uv = (2 * D_MODEL) // tn
    nq = Q_WIDTH // tn
    n_wide = n_uv + nq
    kv_block = (2 * D_MODEL + Q_WIDTH) // (2 * KV_WIDTH)
    npb = n_prompt_rows // tm
    blocks_per_seq = seq // tm

    def tab_idx(i, j):
        return (jnp.where(i < npb, i % blocks_per_seq, blocks_per_seq), 0)

    clip = lambda x, lo, hi: jnp.minimum(jnp.maximum(x, lo), hi)
    tab_spec = pl.BlockSpec((tm, LANES), tab_idx)
    out_shapes = (
        jax.ShapeDtypeStruct((m, 2 * D_MODEL), BF16),
        jax.ShapeDtypeStruct((m, Q_WIDTH), BF16),
        jax.ShapeDtypeStruct((m, KV_WIDTH), F32),
        jax.ShapeDtypeStruct((m, KV_WIDTH), F32),
    )
    return pl.pallas_call(
        functools.partial(_proj_in_kernel, n_uv),
        grid=(m // tm, n_wide + 1),
        in_specs=[
            pl.BlockSpec((tm, D_MODEL), lambda i, j: (i, 0)),
            pl.BlockSpec((D_MODEL, tn), lambda i, j: (0, clip(j - 1, 0, n_wide - 1))),
            pl.BlockSpec((D_MODEL, 2 * KV_WIDTH), lambda i, j: (0, kv_block)),
            tab_spec, tab_spec, tab_spec,
        ],
        out_specs=(
            pl.BlockSpec((tm, tn), lambda i, j: (i, clip(j - 1, 0, n_uv - 1))),
            pl.BlockSpec((tm, tn), lambda i, j: (i, clip(j - 1 - n_uv, 0, nq - 1))),
            pl.BlockSpec((tm, KV_WIDTH), lambda i, j: (i, 0)),
            pl.BlockSpec((tm, KV_WIDTH), lambda i, j: (i, 0)),
        ),
        out_shape=out_shapes,
        compiler_params=_params(("arbitrary", "arbitrary")),
        name="proj_in",
    )(xn, w_in, w_in, cos_t, sa_t, sb_t)


def _proj_gate_kernel(counts, xn_ref, w_ref, b_ref, *refs):
    n = len(counts)
    riders, g_ref, cast = refs[:n], refs[n], refs[n + 1:]
    h = jnp.dot(xn_ref[...], w_ref[...].astype(BF16), preferred_element_type=F32)
    g_ref[...] = jax.nn.sigmoid(h + b_ref[...]).astype(g_ref.dtype)
    _rider_cast(pl.program_id(0) * pl.num_programs(1) + pl.program_id(1), counts, riders, cast)


def _proj_gate(xn, w_gate, b_gate, riders, tm=1536, tn=1024):
    m = xn.shape[0]
    n = w_gate.shape[1]
    nj = n // tn
    counts, r_in, r_out, r_shapes = _rider_specs(riders, (m // tm) * nj, lambda i, j: i * nj + j)
    return pl.pallas_call(
        functools.partial(_proj_gate_kernel, counts),
        grid=(m // tm, nj),
        in_specs=[
            pl.BlockSpec((tm, D_MODEL), lambda i, j: (i, 0)),
            pl.BlockSpec((D_MODEL, tn), lambda i, j: (0, j)),
            pl.BlockSpec((1, tn), lambda i, j: (0, j)),
            *r_in,
        ],
        out_specs=(pl.BlockSpec((tm, tn), lambda i, j: (i, j)), *r_out),
        out_shape=(jax.ShapeDtypeStruct((m, n), BF16), *r_shapes),
        compiler_params=_params(("arbitrary", "arbitrary")),
        name="proj_gate",
    )(xn, w_gate, b_gate, *riders)


def _gmlp_kernel(n_prompt_blocks, tr, u_ref, v_ref, lng_ref, lnb_ref, ws_ref, bst_ref,
                 a_ref, vn_ref):
    i = pl.program_id(0)
    v = v_ref[...].astype(F32)
    mu = jnp.mean(v, axis=-1, keepdims=True)
    vc = v - mu
    var = jnp.mean(vc * vc, axis=-1, keepdims=True)
    vn = vc * lax.rsqrt(var + EPS) * lng_ref[...] + lnb_ref[...]
    vnb = vn.astype(BF16)

    def mix(length):
        r = lax.broadcasted_iota(jnp.int32, (length, length), 0) // CHUNK
        c = lax.broadcasted_iota(jnp.int32, (length, length), 1) // CHUNK
        for g in range(GMLP_GROUPS):
            w = jnp.where(c <= r, ws_ref[g, :length, :length], 0.0).astype(BF16)
            bias = bst_ref[:length, g:g + 1]
            cols = slice(g * GMLP_GROUP_DIM, (g + 1) * GMLP_GROUP_DIM)
            for b in range(tr // length):
                rows = slice(b * length, (b + 1) * length)
                s = jnp.dot(w, vnb[rows, cols], preferred_element_type=F32) + bias
                a_ref[rows, cols] = (u_ref[rows, cols].astype(F32) * s).astype(a_ref.dtype)

    @pl.when(i < n_prompt_blocks)
    def _():
        mix(GMLP_CHUNK)

    @pl.when(i >= n_prompt_blocks)
    def _():
        vn_ref[...] = vn
        mix(CHUNK)


def _gmlp(uv, ln_g, ln_b, ws, bs_t, n_prompt_rows, tr=512):
    m = uv.shape[0]
    npb = n_prompt_rows // tr
    return pl.pallas_call(
        functools.partial(_gmlp_kernel, npb, tr),
        grid=(m // tr,),
        in_specs=[
            pl.BlockSpec((tr, D_MODEL), lambda i: (i, 0)),
            pl.BlockSpec((tr, D_MODEL), lambda i: (i, 1)),
            pl.BlockSpec((1, D_MODEL), lambda i: (0, 0)),
            pl.BlockSpec((1, D_MODEL), lambda i: (0, 0)),
            pl.BlockSpec((GMLP_GROUPS, GMLP_CHUNK, GMLP_CHUNK), lambda i: (0, 0, 0)),
            pl.BlockSpec((GMLP_CHUNK, GMLP_GROUPS), lambda i: (0, 0)),
        ],
        out_specs=(
            pl.BlockSpec((tr, D_MODEL), lambda i: (i, 0)),
            pl.BlockSpec((tr, D_MODEL), lambda i: (jnp.maximum(i - npb, 0), 0)),
        ),
        out_shape=(
            jax.ShapeDtypeStruct((m, D_MODEL), BF16),
            jax.ShapeDtypeStruct((m - n_prompt_rows, D_MODEL), F32),
        ),
        compiler_params=_params(("arbitrary",)),
        name="gmlp",
    )(uv, uv, ln_g, ln_b, ws, bs_t)


def _dup_head(pair, pair_swapped, low_half, odd):
    if odd:
        return jnp.where(low_half, pair_swapped, pair).astype(BF16)
    return jnp.where(low_half, pair, pair_swapped).astype(BF16)


def _attend(q_ref, kwin, vwin, mask, sink_ref, o_ref):
    rows, keys = q_ref.shape[0], kwin.shape[0]
    ones = jnp.ones((keys, LANES), BF16)
    low_q = lax.broadcasted_iota(jnp.int32, (rows, LANES), 1) < HEAD_DIM
    low_k = lax.broadcasted_iota(jnp.int32, (keys, LANES), 1) < HEAD_DIM
    heads_per_pair = LANES // HEAD_DIM
    for c in range(N_KV_HEADS // heads_per_pair):
        kpair = kwin[:, c * LANES:(c + 1) * LANES]
        vpair = vwin[:, c * LANES:(c + 1) * LANES]
        kswap = pltpu.roll(kpair, HEAD_DIM, 1)
        vswap = pltpu.roll(vpair, HEAD_DIM, 1)
        for odd in range(heads_per_pair):
            hk = c * heads_per_pair + odd
            kdup = _dup_head(kpair, kswap, low_k, odd)
            vaug = jnp.concatenate([_dup_head(vpair, vswap, low_k, odd), ones], axis=1)
            parts = []
            for r in range(Q_REP):
                h = hk * Q_REP + r
                qpair = q_ref[:, (h // 2) * LANES:(h // 2 + 1) * LANES]
                keep = low_q if h % 2 == 0 else jnp.logical_not(low_q)
                parts.append(jnp.where(keep, qpair, jnp.zeros_like(qpair)))
            s_all = lax.dot_general(jnp.concatenate(parts, axis=0), kdup, (((1,), (1,)), ((), ())),
                                    preferred_element_type=F32)
            ps, sinks, mxs = [], [], []
            for r in range(Q_REP):
                s = s_all[r * rows:(r + 1) * rows]
                if mask is not None:
                    s = mask(s)
                sink = sink_ref[hk * Q_REP + r] * LOG2E
                mx = jnp.maximum(jnp.max(s, axis=-1, keepdims=True), sink)
                ps.append(jnp.exp2(s - mx).astype(BF16))
                sinks.append(sink)
                mxs.append(mx)
            o_all = jnp.dot(jnp.concatenate(ps, axis=0), vaug, preferred_element_type=F32)
            for r in range(0, Q_REP, 2):
                h = hk * Q_REP + r
                even, oddh = o_all[r * rows:(r + 1) * rows], o_all[(r + 1) * rows:(r + 2) * rows]
                esink = jnp.exp2(jnp.where(low_q, sinks[r], sinks[r + 1])
                                 - jnp.where(low_q, mxs[r], mxs[r + 1]))
                den = jnp.where(low_q, even[:, LANES:], oddh[:, LANES:]) + esink
                num = jnp.where(low_q, even[:, :LANES], oddh[:, :LANES])
                o_ref[:, (h // 2) * LANES:(h // 2 + 1) * LANES] = (num / den).astype(o_ref.dtype)


def _attn_prompt_kernel(tq, counts, sink_ref, q_ref, kp_ref, kc_ref, vp_ref, vc_ref, *refs):
    n = len(counts)
    riders, o_ref, cast = refs[:n], refs[n], refs[n + 1:]
    t = pl.program_id(1)
    _rider_cast(pl.program_id(0) * pl.num_programs(1) + t, counts, riders, cast)
    kwin = jnp.concatenate([kp_ref[...], kc_ref[...]], axis=0)
    vwin = jnp.concatenate([vp_ref[...], vc_ref[...]], axis=0)
    row = lax.broadcasted_iota(jnp.int32, (tq, tq), 0)
    col = lax.broadcasted_iota(jnp.int32, (tq, tq), 1)
    valid_prev = ((row < CHUNK) | (col >= CHUNK)) & (t > 0)
    valid_cur_top = lax.broadcasted_iota(jnp.int32, (CHUNK, tq), 1) < CHUNK

    def mask(s):
        prev = jnp.where(valid_prev, s[:, :tq], NEG)
        cur_top = jnp.where(valid_cur_top, s[:CHUNK, tq:], NEG)
        cur = jnp.concatenate([cur_top, s[CHUNK:, tq:]], axis=0)
        return jnp.concatenate([prev, cur], axis=1)

    _attend(q_ref, kwin, vwin, mask, sink_ref, o_ref)


def _attn_prompt(sinks, q, k, va, batch, seq, riders, tq=128):
    m = q.shape[0]
    nt = seq // tq
    assert tq == WINDOW
    cur = lambda b, t: (b * nt + t, 0)
    prev = lambda b, t: (b * nt + jnp.maximum(t - 1, 0), 0)
    counts, r_in, r_out, r_shapes = _rider_specs(riders, batch * nt, lambda b, t: b * nt + t)
    return pl.pallas_call(
        functools.partial(_attn_prompt_kernel, tq, counts),
        grid=(batch, nt),
        in_specs=[
            pl.BlockSpec(memory_space=pltpu.SMEM),
            pl.BlockSpec((tq, Q_WIDTH), cur),
            pl.BlockSpec((tq, KV_WIDTH), prev),
            pl.BlockSpec((tq, KV_WIDTH), cur),
            pl.BlockSpec((tq, KV_WIDTH), prev),
            pl.BlockSpec((tq, KV_WIDTH), cur),
            *r_in,
        ],
        out_specs=(pl.BlockSpec((tq, Q_WIDTH), cur), *r_out),
        out_shape=(jax.ShapeDtypeStruct((m, Q_WIDTH), BF16), *r_shapes),
        compiler_params=_params(("arbitrary", "arbitrary")),
        name="attn_prompt",
    )(sinks, q, k, k, va, va, *riders)


def _attn_sample_kernel(sink_ref, q_ref, ck_ref, kn_ref, cv_ref, vn_ref, o_in_ref, o_ref):
    del o_in_ref
    kwin = jnp.concatenate([ck_ref[...], kn_ref[...]], axis=0)
    vwin = jnp.concatenate([cv_ref[...], vn_ref[...]], axis=0)
    _attend(q_ref, kwin, vwin, None, sink_ref, o_ref)


def _attn_sample(sinks, q, k, va, cache_k, cache_v, o, n_prompt_rows, dec_batch, dec_seq):
    first = n_prompt_rows // dec_seq
    new = lambda b: (first + b, 0)
    cached = lambda b: (b, 0, 0)
    cache_len = cache_k.shape[1]
    return pl.pallas_call(
        _attn_sample_kernel,
        grid=(dec_batch,),
        in_specs=[
            pl.BlockSpec(memory_space=pltpu.SMEM),
            pl.BlockSpec((dec_seq, Q_WIDTH), new),
            pl.BlockSpec((None, cache_len, KV_WIDTH), cached),
            pl.BlockSpec((dec_seq, KV_WIDTH), new),
            pl.BlockSpec((None, cache_len, KV_WIDTH), cached),
            pl.BlockSpec((dec_seq, KV_WIDTH), new),
            pl.BlockSpec(memory_space=pl.ANY),
        ],
        out_specs=pl.BlockSpec((dec_seq, Q_WIDTH), new),
        out_shape=jax.ShapeDtypeStruct(o.shape, o.dtype),
        input_output_aliases={6: 0},
        compiler_params=_params(("arbitrary",)),
        name="attn_sample",
    )(sinks, q, cache_k, k, cache_v, va, o)


def _branch_kernel(a_ref, o_ref, wa_ref, wb_ref, ga_ref, gb_ref, t_ref):
    ya = jnp.dot(a_ref[...], wa_ref[...], preferred_element_type=F32)
    yb = jnp.dot(o_ref[...], wb_ref[...], preferred_element_type=F32)
    t = ga_ref[...].astype(F32) * ya + gb_ref[...].astype(F32) * yb
    t_ref[...] = t.astype(t_ref.dtype)


def _branch(a, o, wa, wb, g, tm=1536, tn=512):
    m = a.shape[0]
    nj = D_MODEL // tn
    return pl.pallas_call(
        _branch_kernel,
        grid=(m // tm, nj),
        in_specs=[
            pl.BlockSpec((tm, D_MODEL), lambda i, j: (i, 0)),
            pl.BlockSpec((tm, Q_WIDTH), lambda i, j: (i, 0)),
            pl.BlockSpec((D_MODEL, tn), lambda i, j: (0, j)),
            pl.BlockSpec((Q_WIDTH, tn), lambda i, j: (0, j)),
            pl.BlockSpec((tm, tn), lambda i, j: (i, j)),
            pl.BlockSpec((tm, tn), lambda i, j: (i, j + nj)),
        ],
        out_specs=pl.BlockSpec((tm, tn), lambda i, j: (i, j)),
        out_shape=jax.ShapeDtypeStruct((m, D_MODEL), BF16),
        compiler_params=_params(("arbitrary", "arbitrary")),
        name="branch",
    )(a, o, wa, wb, g, g)


def _out_kernel(n_prompt_blocks, t_ref, w_ref, xp_ref, xs_ref, x1_ref):
    i = pl.program_id(0)
    y = jnp.dot(t_ref[...], w_ref[...], preferred_element_type=F32)

    @pl.when(i < n_prompt_blocks)
    def _():
        x1_ref[...] = xp_ref[...] + y

    @pl.when(i >= n_prompt_blocks)
    def _():
        x1_ref[...] = xs_ref[...] + y


def _out(t, w_out, xp, xs, tm=512, tn=D_MODEL):
    m = t.shape[0]
    npb = xp.shape[0] // tm
    return pl.pallas_call(
        functools.partial(_out_kernel, npb),
        grid=(m // tm, D_MODEL // tn),
        in_specs=[
            pl.BlockSpec((tm, D_MODEL), lambda i, j: (i, 0)),
            pl.BlockSpec((D_MODEL, tn), lambda i, j: (0, j)),
            pl.BlockSpec((tm, tn), lambda i, j: (jnp.minimum(i, npb - 1), j)),
            pl.BlockSpec((tm, tn), lambda i, j: (jnp.maximum(i - npb, 0), j)),
        ],
        out_specs=pl.BlockSpec((tm, tn), lambda i, j: (i, j)),
        out_shape=jax.ShapeDtypeStruct((m, D_MODEL), F32),
        compiler_params=_params(("arbitrary", "arbitrary")),
        name="out_proj",
    )(t, w_out, xp, xs)


def _ffn_kernel(x1_ref, ng_ref, wg_ref, wu_ref, wd_ref, fg_ref, y_ref, h_ref):
    f = pl.program_id(1)

    @pl.when(f == 0)
    def _():
        x1 = x1_ref[...]
        h_ref[...] = _rms(x1, ng_ref[...]).astype(h_ref.dtype)
        y_ref[...] = x1

    h = h_ref[...]
    gate = jnp.dot(h, wg_ref[...], preferred_element_type=F32)
    up = jnp.dot(h, wu_ref[...], preferred_element_type=F32)
    hid = (jax.nn.silu(gate) * up).astype(BF16)
    y_ref[...] += jnp.dot(hid, wd_ref[...], preferred_element_type=F32)

    @pl.when(f == pl.num_programs(1) - 1)
    def _():
        y_ref[...] = _rms(y_ref[...], fg_ref[...])


def _ffn(x1, row_block_offset, n_row_blocks, norm_g, wg, wu, wd, final_g, tm=512, tf=512):
    d_ff = wg.shape[1]
    return pl.pallas_call(
        _ffn_kernel,
        grid=(n_row_blocks, d_ff // tf),
        in_specs=[
            pl.BlockSpec((tm, D_MODEL), lambda i, f: (i + row_block_offset, 0)),
            pl.BlockSpec((1, D_MODEL), lambda i, f: (0, 0)),
            pl.BlockSpec((D_MODEL, tf), lambda i, f: (0, f)),
            pl.BlockSpec((D_MODEL, tf), lambda i, f: (0, f)),
            pl.BlockSpec((tf, D_MODEL), lambda i, f: (f, 0)),
            pl.BlockSpec((1, D_MODEL), lambda i, f: (0, 0)),
        ],
        out_specs=pl.BlockSpec((tm, D_MODEL), lambda i, f: (i, 0)),
        out_shape=jax.ShapeDtypeStruct((n_row_blocks * tm, D_MODEL), F32),
        scratch_shapes=[pltpu.VMEM((tm, D_MODEL), BF16)],
        compiler_params=_params(("arbitrary", "arbitrary")),
        name="ffn",
    )(x1, norm_g, wg, wu, wd, final_g)


def _rope_tables(seq, past_len, dec_batch, dec_seq):
    pos = jnp.concatenate([jnp.arange(seq, dtype=F32),
                           jnp.tile(past_len + jnp.arange(dec_seq, dtype=F32), dec_batch)])
    inv_freq = jnp.float32(ROPE_THETA) ** (-(jnp.arange(ROT_HALF, dtype=F32) * 2.0 / ROT_DIM))
    ang = pos[:, None] * inv_freq[None, :]
    cos, sin = jnp.cos(ang), jnp.sin(ang)
    n = pos.shape[0]
    pad = jnp.zeros((n, HEAD_DIM - ROT_DIM), F32)
    zero = jnp.zeros((n, ROT_HALF), F32)
    cos_h = jnp.concatenate([cos, cos, pad + 1.0], axis=1)
    sa_h = jnp.concatenate([zero, sin, pad], axis=1)
    sb_h = jnp.concatenate([-sin, zero, pad], axis=1)
    rep = LANES // HEAD_DIM
    return jnp.tile(cos_h, (1, rep)), jnp.tile(sa_h, (1, rep)), jnp.tile(sb_h, (1, rep))


def kernel(x_prompt, x_sample, cache_swa_k, cache_swa_v, norm_mix_g, w_in, gmlp_ln_g, gmlp_ln_b,
           gmlp_ws, gmlp_bs, attn_sinks, w_gate, b_gate, w_branch_a, w_branch_b, w_out,
           norm_ffn_g, w_ffn_gate, w_ffn_up, w_ffn_down, final_norm_g):
    batch, seq, _ = x_prompt.shape
    dec_batch, dec_seq, _ = x_sample.shape
    depth = w_in.shape[0]
    past_len = 2048
    assert depth == 1 and cache_swa_k.shape[2] == WINDOW and dec_seq == CHUNK
    mp, ms = batch * seq, dec_batch * dec_seq

    xp = x_prompt.reshape(mp, D_MODEL)
    xs = x_sample.reshape(ms, D_MODEL)
    cos_t, sa_t, sb_t = _rope_tables(seq, past_len, dec_batch, dec_seq)
    row = lambda p: p.reshape(1, -1)

    l = 0
    xn = _norm(xp, xs, row(norm_mix_g[l]))
    g, w_in_b = _proj_gate(xn, w_gate[l], row(b_gate[l]), [w_in[l]])
    uv, q, k, va = _proj_in(xn, w_in_b, cos_t, sa_t, sb_t, mp, seq)
    a, vn_s = _gmlp(uv, row(gmlp_ln_g[l]), row(gmlp_ln_b[l]), gmlp_ws[l], gmlp_bs[l].T, mp)
    o, wa_b, wb_b, wo_b, wg, wu, wd = _attn_prompt(
        attn_sinks[l], q, k, va, batch, seq,
        [w_branch_a[l], w_branch_b[l], w_out[l], w_ffn_gate[l], w_ffn_up[l], w_ffn_down[l]])
    o = _attn_sample(attn_sinks[l], q, k, va,
                     cache_swa_k[l].reshape(dec_batch, WINDOW, KV_WIDTH),
                     cache_swa_v[l].reshape(dec_batch, WINDOW, KV_WIDTH),
                     o, mp, dec_batch, dec_seq)
    t = _branch(a, o, wa_b, wb_b, g)
    x1 = _out(t, wo_b, xp, xs)
    ffn_tm = 1024
    ffn = functools.partial(_ffn, norm_g=row(norm_ffn_g[l]), wg=wg, wu=wu, wd=wd,
                            final_g=row(final_norm_g), tm=ffn_tm)
    y_prompt = ffn(x1, 0, mp // ffn_tm)
    y_sample = ffn(x1, mp // ffn_tm, ms // ffn_tm)

    keep = min(WINDOW, seq)
    tail = lambda z: jnp.stack([z[(b + 1) * seq - keep:(b + 1) * seq] for b in range(batch)]).reshape(
        batch, keep, N_KV_HEADS, HEAD_DIM)
    kp, vp = tail(k), tail(va)
    return (
        y_prompt.reshape(batch, seq, D_MODEL),
        y_sample.reshape(dec_batch, dec_seq, D_MODEL),
        kp[None],
        vp[None],
        k[mp:].reshape(1, dec_batch, dec_seq, N_KV_HEADS, HEAD_DIM),
        va[mp:].reshape(1, dec_batch, dec_seq, N_KV_HEADS, HEAD_DIM),
        vn_s.reshape(1, dec_batch, dec_seq, D_MODEL),
    )
```

```python
import functools

import jax
import jax.numpy as jnp
from jax import lax
from jax.experimental import pallas as pl
from jax.experimental.pallas import tpu as pltpu

D_MODEL = 2048
CHUNK = 64
GMLP_CHUNK = 128
GMLP_GROUPS = 8
GMLP_GROUP_DIM = D_MODEL // GMLP_GROUPS
N_HEADS = 32
N_KV_HEADS = 4
HEAD_DIM = 64
Q_REP = N_HEADS // N_KV_HEADS
WINDOW = 128
ROPE_THETA = 500000.0
ROT_DIM = HEAD_DIM // 4
ROT_HALF = ROT_DIM // 2
Q_WIDTH = N_HEADS * HEAD_DIM
KV_WIDTH = N_KV_HEADS * HEAD_DIM
EPS = 1e-6
NEG = -1e30
LANES = 128
LOG2E = 1.4426950408889634
Q_SCALE = HEAD_DIM ** -0.5 * LOG2E

F32 = jnp.float32
BF16 = jnp.bfloat16

VMEM_LIMIT = 60 * 1024 * 1024


def _params(semantics):
    return pltpu.CompilerParams(dimension_semantics=semantics, vmem_limit_bytes=VMEM_LIMIT)


def _pack_rows(x):
    return pltpu.bitcast(x.astype(BF16), jnp.uint32)


def _unpack_rows(words):
    return pltpu.bitcast(words, BF16)


def _rms(x, g):
    return x * lax.rsqrt(jnp.mean(x * x, axis=-1, keepdims=True) + EPS) * g


BF16_SUBLANES = 16


def _rider_specs(weights, n_steps, step_of):
    counts, in_specs, out_specs, out_shapes = [], [], [], []
    for w in weights:
        rows, cols = w.shape
        nb = max(n for n in range(1, n_steps + 1)
                 if rows % n == 0 and (rows // n) % BF16_SUBLANES == 0)
        idx = lambda *g, nb=nb: (jnp.minimum(step_of(*g), nb - 1), 0)
        counts.append(nb)
        in_specs.append(pl.BlockSpec((rows // nb, cols), idx))
        out_specs.append(pl.BlockSpec((rows // nb, cols), idx))
        out_shapes.append(jax.ShapeDtypeStruct(w.shape, BF16))
    return counts, in_specs, out_specs, out_shapes


def _rider_cast(step, counts, src_refs, dst_refs):
    for nb, src, dst in zip(counts, src_refs, dst_refs):
        @pl.when(step < nb)
        def _():
            dst[...] = src[...].astype(dst.dtype)


def _norm_kernel(n_prompt_blocks, xp_ref, xs_ref, g_ref, o_ref):
    i = pl.program_id(0)

    @pl.when(i < n_prompt_blocks)
    def _():
        o_ref[...] = _pack_rows(_rms(xp_ref[...], g_ref[...]))

    @pl.when(i >= n_prompt_blocks)
    def _():
        o_ref[...] = _pack_rows(_rms(xs_ref[...], g_ref[...]))


def _norm(xp, xs, g, tr=512):
    mp, ms = xp.shape[0], xs.shape[0]
    npb, nsb = mp // tr, ms // tr
    return pl.pallas_call(
        functools.partial(_norm_kernel, npb),
        grid=(npb + nsb,),
        in_specs=[
            pl.BlockSpec((tr, D_MODEL), lambda i: (jnp.minimum(i, npb - 1), 0)),
            pl.BlockSpec((tr, D_MODEL), lambda i: (jnp.maximum(i - npb, 0), 0)),
            pl.BlockSpec((1, D_MODEL), lambda i: (0, 0)),
        ],
        out_specs=pl.BlockSpec((tr // 2, D_MODEL), lambda i: (i, 0)),
        out_shape=jax.ShapeDtypeStruct(((mp + ms) // 2, D_MODEL), jnp.uint32),
        compiler_params=_params(("arbitrary",)),
        name="norm",
    )(xp, xs, g)


def _rope(h, cos, sa, sb):
    return h * cos + pltpu.roll(h, ROT_HALF, 1) * sa + pltpu.roll(h, LANES - ROT_HALF, 1) * sb


def _proj_in_kernel(n_uv, xn_ref, win_ref, wkv_ref, cos_ref, sa_ref, sb_ref,
                    uv_ref, q_ref, k_ref, va_ref):
    j = pl.program_id(1)
    xn = _unpack_rows(xn_ref[...])

    def mm(w_ref):
        return jnp.dot(xn, w_ref[...], preferred_element_type=F32)

    def rope_cols(h, scale):
        cos, sa, sb = cos_ref[...], sa_ref[...], sb_ref[...]
        parts = [_rope(h[:, c:c + LANES], cos, sa, sb) * scale for c in range(0, h.shape[1], LANES)]
        return jnp.concatenate(parts, axis=1)

    @pl.when(j == 0)
    def _():
        h = mm(wkv_ref)
        k_ref[...] = rope_cols(h[:, :KV_WIDTH], 1.0)
        va_ref[...] = h[:, KV_WIDTH:]

    @pl.when((j >= 1) & (j <= n_uv))
    def _():
        uv_ref[...] = jax.nn.gelu(mm(win_ref)).astype(uv_ref.dtype)

    @pl.when(j > n_uv)
    def _():
        q_ref[...] = rope_cols(mm(win_ref), Q_SCALE).astype(q_ref.dtype)


def _proj_in(xn, w_in, cos_t, sa_t, sb_t, tm=1536, tn=1024):
    m = 2 * xn.shape[0]
    n_uv = (2 * D_MODEL) // tn
    nq = Q_WIDTH // tn
    n_wide = n_uv + nq
    kv_block = (2 * D_MODEL + Q_WIDTH) // (2 * KV_WIDTH)
    clip = lambda x, lo, hi: jnp.minimum(jnp.maximum(x, lo), hi)
    tab_spec = pl.BlockSpec((tm, LANES), lambda i, j: (i, 0))
    out_shapes = (
        jax.ShapeDtypeStruct((m, 2 * D_MODEL), BF16),
        jax.ShapeDtypeStruct((m, Q_WIDTH), BF16),
        jax.ShapeDtypeStruct((m, KV_WIDTH), F32),
        jax.ShapeDtypeStruct((m, KV_WIDTH), F32),
    )
    return pl.pallas_call(
        functools.partial(_proj_in_kernel, n_uv),
        grid=(m // tm, n_wide + 1),
        in_specs=[
            pl.BlockSpec((tm // 2, D_MODEL), lambda i, j: (i, 0)),
            pl.BlockSpec((D_MODEL, tn), lambda i, j: (0, clip(j - 1, 0, n_wide - 1))),
            pl.BlockSpec((D_MODEL, 2 * KV_WIDTH), lambda i, j: (0, kv_block)),
            tab_spec, tab_spec, tab_spec,
        ],
        out_specs=(
            pl.BlockSpec((tm, tn), lambda i, j: (i, clip(j - 1, 0, n_uv - 1))),
            pl.BlockSpec((tm, tn), lambda i, j: (i, clip(j - 1 - n_uv, 0, nq - 1))),
            pl.BlockSpec((tm, KV_WIDTH), lambda i, j: (i, 0)),
            pl.BlockSpec((tm, KV_WIDTH), lambda i, j: (i, 0)),
        ),
        out_shape=out_shapes,
        compiler_params=_params(("arbitrary", "arbitrary")),
        name="proj_in",
    )(xn, w_in, w_in, cos_t, sa_t, sb_t)


def _proj_gate_kernel(counts, xn_ref, w_ref, b_ref, *refs):
    n = len(counts)
    riders, g_ref, cast = refs[:n], refs[n], refs[n + 1:]
    h = jnp.dot(_unpack_rows(xn_ref[...]), w_ref[...].astype(BF16), preferred_element_type=F32)
    g_ref[...] = jax.nn.sigmoid(h + b_ref[...]).astype(g_ref.dtype)
    _rider_cast(pl.program_id(0) * pl.num_programs(1) + pl.program_id(1), counts, riders, cast)


def _proj_gate(xn, w_gate, b_gate, riders, tm=1536, tn=1024):
    m = 2 * xn.shape[0]
    n = w_gate.shape[1]
    nj = n // tn
    counts, r_in, r_out, r_shapes = _rider_specs(riders, (m // tm) * nj, lambda i, j: i * nj + j)
    return pl.pallas_call(
        functools.partial(_proj_gate_kernel, counts),
        grid=(m // tm, nj),
        in_specs=[
            pl.BlockSpec((tm // 2, D_MODEL), lambda i, j: (i, 0)),
            pl.BlockSpec((D_MODEL, tn), lambda i, j: (0, j)),
            pl.BlockSpec((1, tn), lambda i, j: (0, j)),
            *r_in,
        ],
        out_specs=(pl.BlockSpec((tm, tn), lambda i, j: (i, j)), *r_out),
        out_shape=(jax.ShapeDtypeStruct((m, n), BF16), *r_shapes),
        compiler_params=_params(("arbitrary", "arbitrary")),
        name="proj_gate",
    )(xn, w_gate, b_gate, *riders)


def _gmlp_kernel(n_prompt_blocks, tr, u_ref, v_ref, lng_ref, lnb_ref, ws_ref, bst_ref,
                 a_ref, vn_ref):
    i = pl.program_id(0)
    v = v_ref[...].astype(F32)
    mu = jnp.mean(v, axis=-1, keepdims=True)
    vc = v - mu
    var = jnp.mean(vc * vc, axis=-1, keepdims=True)
    vn = vc * lax.rsqrt(var + EPS) * lng_ref[...] + lnb_ref[...]
    vnb = vn.astype(BF16)

    def mix(length):
        r = lax.broadcasted_iota(jnp.int32, (length, length), 0) // CHUNK
        c = lax.broadcasted_iota(jnp.int32, (length, length), 1) // CHUNK
        for g in range(GMLP_GROUPS):
            w = jnp.where(c <= r, ws_ref[g, :length, :length], 0.0).astype(BF16)
            bias = bst_ref[:length, g:g + 1]
            cols = slice(g * GMLP_GROUP_DIM, (g + 1) * GMLP_GROUP_DIM)
            for b in range(tr // length):
                rows = slice(b * length, (b + 1) * length)
                s = jnp.dot(w, vnb[rows, cols], preferred_element_type=F32) + bias
                a_ref[rows, cols] = (u_ref[rows, cols].astype(F32) * s).astype(a_ref.dtype)

    @pl.when(i < n_prompt_blocks)
    def _():
        mix(GMLP_CHUNK)

    @pl.when(i >= n_prompt_blocks)
    def _():
        vn_ref[...] = vn
        mix(CHUNK)


def _gmlp(uv, ln_g, ln_b, ws, bs_t, n_prompt_rows, tr=512):
    m = uv.shape[0]
    npb = n_prompt_rows // tr
    return pl.pallas_call(
        functools.partial(_gmlp_kernel, npb, tr),
        grid=(m // tr,),
        in_specs=[
            pl.BlockSpec((tr, D_MODEL), lambda i: (i, 0)),
            pl.BlockSpec((tr, D_MODEL), lambda i: (i, 1)),
            pl.BlockSpec((1, D_MODEL), lambda i: (0, 0)),
            pl.BlockSpec((1, D_MODEL), lambda i: (0, 0)),
            pl.BlockSpec((GMLP_GROUPS, GMLP_CHUNK, GMLP_CHUNK), lambda i: (0, 0, 0)),
            pl.BlockSpec((GMLP_CHUNK, GMLP_GROUPS), lambda i: (0, 0)),
        ],
        out_specs=(
            pl.BlockSpec((tr, D_MODEL), lambda i: (i, 0)),
            pl.BlockSpec((tr, D_MODEL), lambda i: (jnp.maximum(i - npb, 0), 0)),
        ),
        out_shape=(
            jax.ShapeDtypeStruct((m, D_MODEL), BF16),
            jax.ShapeDtypeStruct((m - n_prompt_rows, D_MODEL), F32),
        ),
        compiler_params=_params(("arbitrary",)),
        name="gmlp",
    )(uv, uv, ln_g, ln_b, ws, bs_t)


def _dup_head(pair, pair_swapped, low_half, odd):
    if odd:
        return jnp.where(low_half, pair_swapped, pair).astype(BF16)
    return jnp.where(low_half, pair, pair_swapped).astype(BF16)


def _attend(q_ref, kwin, vwin, mask, sink_ref, o_ref):
    rows, keys = q_ref.shape[0], kwin.shape[0]
    ones = jnp.ones((keys, LANES), BF16)
    low_q = lax.broadcasted_iota(jnp.int32, (rows, LANES), 1) < HEAD_DIM
    low_k = lax.broadcasted_iota(jnp.int32, (keys, LANES), 1) < HEAD_DIM
    heads_per_pair = LANES // HEAD_DIM
    for c in range(N_KV_HEADS // heads_per_pair):
        kpair = kwin[:, c * LANES:(c + 1) * LANES]
        vpair = vwin[:, c * LANES:(c + 1) * LANES]
        kswap = pltpu.roll(kpair, HEAD_DIM, 1)
        vswap = pltpu.roll(vpair, HEAD_DIM, 1)
        for odd in range(heads_per_pair):
            hk = c * heads_per_pair + odd
            kdup = _dup_head(kpair, kswap, low_k, odd)
            vaug = jnp.concatenate([_dup_head(vpair, vswap, low_k, odd), ones], axis=1)
            parts = []
            for r in range(Q_REP):
                h = hk * Q_REP + r
                qpair = q_ref[:, (h // 2) * LANES:(h // 2 + 1) * LANES]
                keep = low_q if h % 2 == 0 else jnp.logical_not(low_q)
                parts.append(jnp.where(keep, qpair, jnp.zeros_like(qpair)))
            s_all = lax.dot_general(jnp.concatenate(parts, axis=0), kdup, (((1,), (1,)), ((), ())),
                                    preferred_element_type=F32)
            ps, sinks, mxs = [], [], []
            for r in range(Q_REP):
                s = s_all[r * rows:(r + 1) * rows]
                if mask is not None:
                    s = mask(s)
                sink = sink_ref[hk * Q_REP + r] * LOG2E
                mx = jnp.maximum(jnp.max(s, axis=-1, keepdims=True), sink)
                ps.append(jnp.exp2(s - mx).astype(BF16))
                sinks.append(sink)
                mxs.append(mx)
            o_all = jnp.dot(jnp.concatenate(ps, axis=0), vaug, preferred_element_type=F32)
            for r in range(0, Q_REP, 2):
                h = hk * Q_REP + r
                even, oddh = o_all[r * rows:(r + 1) * rows], o_all[(r + 1) * rows:(r + 2) * rows]
                esink = jnp.exp2(jnp.where(low_q, sinks[r], sinks[r + 1])
                                 - jnp.where(low_q, mxs[r], mxs[r + 1]))
                den = jnp.where(low_q, even[:, LANES:], oddh[:, LANES:]) + esink
                num = jnp.where(low_q, even[:, :LANES], oddh[:, :LANES])
                o_ref[:, (h // 2) * LANES:(h // 2 + 1) * LANES] = (num / den).astype(o_ref.dtype)


def _attn_prompt_kernel(tq, counts, sink_ref, q_ref, kp_ref, kc_ref, vp_ref, vc_ref, *refs):
    n = len(counts)
    riders, o_ref, cast = refs[:n], refs[n], refs[n + 1:]
    t = pl.program_id(1)
    _rider_cast(pl.program_id(0) * pl.num_programs(1) + t, counts, riders, cast)
    kwin = jnp.concatenate([kp_ref[...], kc_ref[...]], axis=0)
    vwin = jnp.concatenate([vp_ref[...], vc_ref[...]], axis=0)
    row = lax.broadcasted_iota(jnp.int32, (tq, tq), 0)
    col = lax.broadcasted_iota(jnp.int32, (tq, tq), 1)
    valid_prev = ((row < CHUNK) | (col >= CHUNK)) & (t > 0)
    valid_cur_top = lax.broadcasted_iota(jnp.int32, (CHUNK, tq), 1) < CHUNK

    def mask(s):
        prev = jnp.where(valid_prev, s[:, :tq], NEG)
        cur_top = jnp.where(valid_cur_top, s[:CHUNK, tq:], NEG)
        cur = jnp.concatenate([cur_top, s[CHUNK:, tq:]], axis=0)
        return jnp.concatenate([prev, cur], axis=1)

    _attend(q_ref, kwin, vwin, mask, sink_ref, o_ref)


def _attn_prompt(sinks, q, k, va, batch, seq, riders, tq=128):
    m = q.shape[0]
    nt = seq // tq
    assert tq == WINDOW
    cur = lambda b, t: (b * nt + t, 0)
    prev = lambda b, t: (b * nt + jnp.maximum(t - 1, 0), 0)
    counts, r_in, r_out, r_shapes = _rider_specs(riders, batch * nt, lambda b, t: b * nt + t)
    return pl.pallas_call(
        functools.partial(_attn_prompt_kernel, tq, counts),
        grid=(batch, nt),
        in_specs=[
            pl.BlockSpec(memory_space=pltpu.SMEM),
            pl.BlockSpec((tq, Q_WIDTH), cur),
            pl.BlockSpec((tq, KV_WIDTH), prev),
            pl.BlockSpec((tq, KV_WIDTH), cur),
            pl.BlockSpec((tq, KV_WIDTH), prev),
            pl.BlockSpec((tq, KV_WIDTH), cur),
            *r_in,
        ],
        out_specs=(pl.BlockSpec((tq, Q_WIDTH), cur), *r_out),
        out_shape=(jax.ShapeDtypeStruct((m, Q_WIDTH), BF16), *r_shapes),
        compiler_params=_params(("arbitrary", "arbitrary")),
        name="attn_prompt",
    )(sinks, q, k, k, va, va, *riders)


def _attn_sample_kernel(sink_ref, q_ref, ck_ref, kn_ref, cv_ref, vn_ref, o_in_ref, o_ref):
    del o_in_ref
    kwin = jnp.concatenate([ck_ref[...], kn_ref[...]], axis=0)
    vwin = jnp.concatenate([cv_ref[...], vn_ref[...]], axis=0)
    _attend(q_ref, kwin, vwin, None, sink_ref, o_ref)


def _attn_sample(sinks, q, k, va, cache_k, cache_v, o, n_prompt_rows, dec_batch, dec_seq):
    first = n_prompt_rows // dec_seq
    new = lambda b: (first + b, 0)
    cached = lambda b: (b, 0, 0)
    cache_len = cache_k.shape[1]
    return pl.pallas_call(
        _attn_sample_kernel,
        grid=(dec_batch,),
        in_specs=[
            pl.BlockSpec(memory_space=pltpu.SMEM),
            pl.BlockSpec((dec_seq, Q_WIDTH), new),
            pl.BlockSpec((None, cache_len, KV_WIDTH), cached),
            pl.BlockSpec((dec_seq, KV_WIDTH), new),
            pl.BlockSpec((None, cache_len, KV_WIDTH), cached),
            pl.BlockSpec((dec_seq, KV_WIDTH), new),
            pl.BlockSpec(memory_space=pl.ANY),
        ],
        out_specs=pl.BlockSpec((dec_seq, Q_WIDTH), new),
        out_shape=jax.ShapeDtypeStruct(o.shape, o.dtype),
        input_output_aliases={6: 0},
        compiler_params=_params(("arbitrary",)),
        name="attn_sample",
    )(sinks, q, cache_k, k, cache_v, va, o)


def _branch_kernel(a_ref, o_ref, wa_ref, wb_ref, ga_ref, gb_ref, t_ref):
    ya = jnp.dot(a_ref[...], wa_ref[...], preferred_element_type=F32)
    yb = jnp.dot(o_ref[...], wb_ref[...], preferred_element_type=F32)
    t = ga_ref[...].astype(F32) * ya + gb_ref[...].astype(F32) * yb
    t_ref[...] = t.astype(t_ref.dtype)


def _branch(a, o, wa, wb, g, tm=1536, tn=512):
    m = a.shape[0]
    nj = D_MODEL // tn
    return pl.pallas_call(
        _branch_kernel,
        grid=(m // tm, nj),
        in_specs=[
            pl.BlockSpec((tm, D_MODEL), lambda i, j: (i, 0)),
            pl.BlockSpec((tm, Q_WIDTH), lambda i, j: (i, 0)),
            pl.BlockSpec((D_MODEL, tn), lambda i, j: (0, j)),
            pl.BlockSpec((Q_WIDTH, tn), lambda i, j: (0, j)),
            pl.BlockSpec((tm, tn), lambda i, j: (i, j)),
            pl.BlockSpec((tm, tn), lambda i, j: (i, j + nj)),
        ],
        out_specs=pl.BlockSpec((tm, tn), lambda i, j: (i, j)),
        out_shape=jax.ShapeDtypeStruct((m, D_MODEL), BF16),
        compiler_params=_params(("arbitrary", "arbitrary")),
        name="branch",
    )(a, o, wa, wb, g, g)


def _out_kernel(n_prompt_blocks, t_ref, w_ref, xp_ref, xs_ref, x1_ref):
    i = pl.program_id(0)
    y = jnp.dot(t_ref[...], w_ref[...], preferred_element_type=F32)

    @pl.when(i < n_prompt_blocks)
    def _():
        x1_ref[...] = xp_ref[...] + y

    @pl.when(i >= n_prompt_blocks)
    def _():
        x1_ref[...] = xs_ref[...] + y


def _out(t, w_out, xp, xs, tm=512, tn=D_MODEL):
    m = t.shape[0]
    npb = xp.shape[0] // tm
    return pl.pallas_call(
        functools.partial(_out_kernel, npb),
        grid=(m // tm, D_MODEL // tn),
        in_specs=[
            pl.BlockSpec((tm, D_MODEL), lambda i, j: (i, 0)),
            pl.BlockSpec((D_MODEL, tn), lambda i, j: (0, j)),
            pl.BlockSpec((tm, tn), lambda i, j: (jnp.minimum(i, npb - 1), j)),
            pl.BlockSpec((tm, tn), lambda i, j: (jnp.maximum(i - npb, 0), j)),
        ],
        out_specs=pl.BlockSpec((tm, tn), lambda i, j: (i, j)),
        out_shape=jax.ShapeDtypeStruct((m, D_MODEL), F32),
        compiler_params=_params(("arbitrary", "arbitrary")),
        name="out_proj",
    )(t, w_out, xp, xs)


def _ffn_kernel(x1_ref, ng_ref, wg_ref, wu_ref, wd_ref, fg_ref, y_ref, h_ref):
    f = pl.program_id(1)

    @pl.when(f == 0)
    def _():
        x1 = x1_ref[...]
        h_ref[...] = _rms(x1, ng_ref[...]).astype(h_ref.dtype)
        y_ref[...] = x1

    h = h_ref[...]
    gate = jnp.dot(h, wg_ref[...], preferred_element_type=F32)
    up = jnp.dot(h, wu_ref[...], preferred_element_type=F32)
    hid = (jax.nn.silu(gate) * up).astype(BF16)
    y_ref[...] += jnp.dot(hid, wd_ref[...], preferred_element_type=F32)

    @pl.when(f == pl.num_programs(1) - 1)
    def _():
        y_ref[...] = _rms(y_ref[...], fg_ref[...])


def _ffn(x1, row_block_offset, n_row_blocks, norm_g, wg, wu, wd, final_g, tm=512, tf=512):
    d_ff = wg.shape[1]
    return pl.pallas_call(
        _ffn_kernel,
        grid=(n_row_blocks, d_ff // tf),
        in_specs=[
            pl.BlockSpec((tm, D_MODEL), lambda i, f: (i + row_block_offset, 0)),
            pl.BlockSpec((1, D_MODEL), lambda i, f: (0, 0)),
            pl.BlockSpec((D_MODEL, tf), lambda i, f: (0, f)),
            pl.BlockSpec((D_MODEL, tf), lambda i, f: (0, f)),
            pl.BlockSpec((tf, D_MODEL), lambda i, f: (f, 0)),
            pl.BlockSpec((1, D_MODEL), lambda i, f: (0, 0)),
        ],
        out_specs=pl.BlockSpec((tm, D_MODEL), lambda i, f: (i, 0)),
        out_shape=jax.ShapeDtypeStruct((n_row_blocks * tm, D_MODEL), F32),
        scratch_shapes=[pltpu.VMEM((tm, D_MODEL), BF16)],
        compiler_params=_params(("arbitrary", "arbitrary")),
        name="ffn",
    )(x1, norm_g, wg, wu, wd, final_g)


def _rope_tables(batch, seq, past_len, dec_batch, dec_seq):
    pos = jnp.concatenate([jnp.tile(jnp.arange(seq, dtype=F32), batch),
                           jnp.tile(past_len + jnp.arange(dec_seq, dtype=F32), dec_batch)])
    inv_freq = jnp.float32(ROPE_THETA) ** (-(jnp.arange(ROT_HALF, dtype=F32) * 2.0 / ROT_DIM))
    ang = pos[:, None] * inv_freq[None, :]
    cos, sin = jnp.cos(ang), jnp.sin(ang)
    n = pos.shape[0]
    pad = jnp.zeros((n, HEAD_DIM - ROT_DIM), F32)
    zero = jnp.zeros((n, ROT_HALF), F32)
    cos_h = jnp.concatenate([cos, cos, pad + 1.0], axis=1)
    sa_h = jnp.concatenate([zero, sin, pad], axis=1)
    sb_h = jnp.concatenate([-sin, zero, pad], axis=1)
    rep = LANES // HEAD_DIM
    return jnp.tile(cos_h, (1, rep)), jnp.tile(sa_h, (1, rep)), jnp.tile(sb_h, (1, rep))


def kernel(x_prompt, x_sample, cache_swa_k, cache_swa_v, norm_mix_g, w_in, gmlp_ln_g, gmlp_ln_b,
           gmlp_ws, gmlp_bs, attn_sinks, w_gate, b_gate, w_branch_a, w_branch_b, w_out,
           norm_ffn_g, w_ffn_gate, w_ffn_up, w_ffn_down, final_norm_g):
    batch, seq, _ = x_prompt.shape
    dec_batch, dec_seq, _ = x_sample.shape
    depth = w_in.shape[0]
    past_len = 2048
    assert depth == 1 and cache_swa_k.shape[2] == WINDOW and dec_seq == CHUNK
    mp, ms = batch * seq, dec_batch * dec_seq

    xp = x_prompt.reshape(mp, D_MODEL)
    xs = x_sample.reshape(ms, D_MODEL)
    cos_t, sa_t, sb_t = _rope_tables(batch, seq, past_len, dec_batch, dec_seq)
    row = lambda p: p.reshape(1, -1)

    l = 0
    xn = _norm(xp, xs, row(norm_mix_g[l]))
    g, w_in_b = _proj_gate(xn, w_gate[l], row(b_gate[l]), [w_in[l]])
    uv, q, k, va = _proj_in(xn, w_in_b, cos_t, sa_t, sb_t)
    a, vn_s = _gmlp(uv, row(gmlp_ln_g[l]), row(gmlp_ln_b[l]), gmlp_ws[l], gmlp_bs[l].T, mp)
    o, wa_b, wb_b, wo_b, wg, wu, wd = _attn_prompt(
        attn_sinks[l], q, k, va, batch, seq,
        [w_branch_a[l], w_branch_b[l], w_out[l], w_ffn_gate[l], w_ffn_up[l], w_ffn_down[l]])
    o = _attn_sample(attn_sinks[l], q, k, va,
                     cache_swa_k[l].reshape(dec_batch, WINDOW, KV_WIDTH),
                     cache_swa_v[l].reshape(dec_batch, WINDOW, KV_WIDTH),
                     o, mp, dec_batch, dec_seq)
    t = _branch(a, o, wa_b, wb_b, g)
    x1 = _out(t, wo_b, xp, xs)
    ffn_tm = 1024
    ffn = functools.partial(_ffn, norm_g=row(norm_ffn_g[l]), wg=wg, wu=wu, wd=wd,
                            final_g=row(final_norm_g), tm=ffn_tm)
    y_prompt = ffn(x1, 0, mp // ffn_tm)
    y_sample = ffn(x1, mp // ffn_tm, ms // ffn_tm)

    keep = min(WINDOW, seq)
    tail = lambda z: jnp.stack([z[(b + 1) * seq - keep:(b + 1) * seq] for b in range(batch)]).reshape(
        batch, keep, N_KV_HEADS, HEAD_DIM)
    kp, vp = tail(k), tail(va)
    return (
        y_prompt.reshape(batch, seq, D_MODEL),
        y_sample.reshape(dec_batch, dec_seq, D_MODEL),
        kp[None],
        vp[None],
        k[mp:].reshape(1, dec_batch, dec_seq, N_KV_HEADS, HEAD_DIM),
        va[mp:].reshape(1, dec_batch, dec_seq, N_KV_HEADS, HEAD_DIM),
        vn_s.reshape(1, dec_batch, dec_seq, D_MODEL),
    )
```

```python
import functools

import jax
import jax.numpy as jnp
from jax import lax
from jax.experimental import pallas as pl
from jax.experimental.pallas import tpu as pltpu

D_MODEL = 2048
CHUNK = 64
GMLP_CHUNK = 128
GMLP_GROUPS = 8
GMLP_GROUP_DIM = D_MODEL // GMLP_GROUPS
N_HEADS = 32
N_KV_HEADS = 4
HEAD_DIM = 64
Q_REP = N_HEADS // N_KV_HEADS
WINDOW = 128
ROPE_THETA = 500000.0
ROT_DIM = HEAD_DIM // 4
ROT_HALF = ROT_DIM // 2
Q_WIDTH = N_HEADS * HEAD_DIM
KV_WIDTH = N_KV_HEADS * HEAD_DIM
EPS = 1e-6
NEG = -1e30
LANES = 128
LOG2E = 1.4426950408889634
Q_SCALE = HEAD_DIM ** -0.5 * LOG2E

F32 = jnp.float32
BF16 = jnp.bfloat16

VMEM_LIMIT = 60 * 1024 * 1024


def _params(semantics):
    return pltpu.CompilerParams(dimension_semantics=semantics, vmem_limit_bytes=VMEM_LIMIT)


def _rms(x, g):
    return x * lax.rsqrt(jnp.mean(x * x, axis=-1, keepdims=True) + EPS) * g


BF16_SUBLANES = 16


def _rider_specs(weights, n_steps, step_of):
    counts, in_specs, out_specs, out_shapes = [], [], [], []
    for w in weights:
        rows, cols = w.shape
        nb = max(n for n in range(1, n_steps + 1)
                 if rows % n == 0 and (rows // n) % BF16_SUBLANES == 0)
        idx = lambda *g, nb=nb: (jnp.minimum(step_of(*g), nb - 1), 0)
        counts.append(nb)
        in_specs.append(pl.BlockSpec((rows // nb, cols), idx))
        out_specs.append(pl.BlockSpec((rows // nb, cols), idx))
        out_shapes.append(jax.ShapeDtypeStruct(w.shape, BF16))
    return counts, in_specs, out_specs, out_shapes


def _rider_cast(step, counts, src_refs, dst_refs):
    for nb, src, dst in zip(counts, src_refs, dst_refs):
        @pl.when(step < nb)
        def _():
            dst[...] = src[...].astype(dst.dtype)


def _norm_kernel(n_prompt_blocks, xp_ref, xs_ref, g_ref, o_ref):
    i = pl.program_id(0)

    @pl.when(i < n_prompt_blocks)
    def _():
        o_ref[...] = _rms(xp_ref[...], g_ref[...]).astype(BF16)

    @pl.when(i >= n_prompt_blocks)
    def _():
        o_ref[...] = _rms(xs_ref[...], g_ref[...]).astype(BF16)


def _norm(xp, xs, g, tr=512):
    mp, ms = xp.shape[0], xs.shape[0]
    npb, nsb = mp // tr, ms // tr
    return pl.pallas_call(
        functools.partial(_norm_kernel, npb),
        grid=(npb + nsb,),
        in_specs=[
            pl.BlockSpec((tr, D_MODEL), lambda i: (jnp.minimum(i, npb - 1), 0)),
            pl.BlockSpec((tr, D_MODEL), lambda i: (jnp.maximum(i - npb, 0), 0)),
            pl.BlockSpec((1, D_MODEL), lambda i: (0, 0)),
        ],
        out_specs=pl.BlockSpec((tr, D_MODEL), lambda i: (i, 0)),
        out_shape=jax.ShapeDtypeStruct((mp + ms, D_MODEL), BF16),
        compiler_params=_params(("arbitrary",)),
        name="norm",
    )(xp, xs, g)


def _rope(h, cos, sa, sb):
    return h * cos + pltpu.roll(h, ROT_HALF, 1) * sa + pltpu.roll(h, LANES - ROT_HALF, 1) * sb


def _proj_in_kernel(n_uv, counts, xn_ref, win_ref, wkv_ref, cos_ref, sa_ref, sb_ref, *refs):
    n = len(counts)
    riders, (uv_ref, q_ref, k_ref, va_ref), cast = refs[:n], refs[n:n + 4], refs[n + 4:]
    j = pl.program_id(1)
    _rider_cast(pl.program_id(0) * pl.num_programs(1) + j, counts, riders, cast)
    xn = xn_ref[...]

    def mm(w_ref):
        return jnp.dot(xn, w_ref[...], preferred_element_type=F32)

    def rope_cols(h, scale):
        cos, sa, sb = cos_ref[...], sa_ref[...], sb_ref[...]
        parts = [_rope(h[:, c:c + LANES], cos, sa, sb) * scale for c in range(0, h.shape[1], LANES)]
        return jnp.concatenate(parts, axis=1)

    @pl.when(j == 0)
    def _():
        h = mm(wkv_ref)
        k_ref[...] = rope_cols(h[:, :KV_WIDTH], 1.0)
        va_ref[...] = h[:, KV_WIDTH:]

    @pl.when((j >= 1) & (j <= n_uv))
    def _():
        uv_ref[...] = jax.nn.gelu(mm(win_ref)).astype(uv_ref.dtype)

    @pl.when(j > n_uv)
    def _():
        q_ref[...] = rope_cols(mm(win_ref), Q_SCALE).astype(q_ref.dtype)


def _proj_in(xn, w_in, cos_t, sa_t, sb_t, n_prompt_rows, seq, riders, tm=1024, tn=1024):
    m = xn.shape[0]
    n_uv = (2 * D_MODEL) // tn
    nq = Q_WIDTH // tn
    n_wide = n_uv + nq
    kv_block = (2 * D_MODEL + Q_WIDTH) // (2 * KV_WIDTH)
    npb = n_prompt_rows // tm
    blocks_per_seq = seq // tm

    def tab_idx(i, j):
        return (jnp.where(i < npb, i % blocks_per_seq, blocks_per_seq), 0)

    clip = lambda x, lo, hi: jnp.minimum(jnp.maximum(x, lo), hi)
    tab_spec = pl.BlockSpec((tm, LANES), tab_idx)
    n_steps = n_wide + 1
    counts, r_in, r_out, r_shapes = _rider_specs(riders, (m // tm) * n_steps,
                                                 lambda i, j: i * n_steps + j)
    out_shapes = (
        jax.ShapeDtypeStruct((m, 2 * D_MODEL), BF16),
        jax.ShapeDtypeStruct((m, Q_WIDTH), BF16),
        jax.ShapeDtypeStruct((m, KV_WIDTH), F32),
        jax.ShapeDtypeStruct((m, KV_WIDTH), F32),
        *r_shapes,
    )
    return pl.pallas_call(
        functools.partial(_proj_in_kernel, n_uv, counts),
        grid=(m // tm, n_steps),
        in_specs=[
            pl.BlockSpec((tm, D_MODEL), lambda i, j: (i, 0)),
            pl.BlockSpec((D_MODEL, tn), lambda i, j: (0, clip(j - 1, 0, n_wide - 1))),
            pl.BlockSpec((D_MODEL, 2 * KV_WIDTH), lambda i, j: (0, kv_block)),
            tab_spec, tab_spec, tab_spec,
            *r_in,
        ],
        out_specs=(
            pl.BlockSpec((tm, tn), lambda i, j: (i, clip(j - 1, 0, n_uv - 1))),
            pl.BlockSpec((tm, tn), lambda i, j: (i, clip(j - 1 - n_uv, 0, nq - 1))),
            pl.BlockSpec((tm, KV_WIDTH), lambda i, j: (i, 0)),
            pl.BlockSpec((tm, KV_WIDTH), lambda i, j: (i, 0)),
            *r_out,
        ),
        out_shape=out_shapes,
        compiler_params=_params(("arbitrary", "arbitrary")),
        name="proj_in",
    )(xn, w_in, w_in, cos_t, sa_t, sb_t, *riders)


def _proj_gate_kernel(counts, xn_ref, w_ref, b_ref, *refs):
    n = len(counts)
    riders, g_ref, cast = refs[:n], refs[n], refs[n + 1:]
    h = jnp.dot(xn_ref[...], w_ref[...].astype(BF16), preferred_element_type=F32)
    g_ref[...] = jax.nn.sigmoid(h + b_ref[...]).astype(g_ref.dtype)
    _rider_cast(pl.program_id(0) * pl.num_programs(1) + pl.program_id(1), counts, riders, cast)


def _proj_gate(xn, w_gate, b_gate, riders, tm=1536, tn=1024):
    m = xn.shape[0]
    n = w_gate.shape[1]
    nj = n // tn
    counts, r_in, r_out, r_shapes = _rider_specs(riders, (m // tm) * nj, lambda i, j: i * nj + j)
    return pl.pallas_call(
        functools.partial(_proj_gate_kernel, counts),
        grid=(m // tm, nj),
        in_specs=[
            pl.BlockSpec((tm, D_MODEL), lambda i, j: (i, 0)),
            pl.BlockSpec((D_MODEL, tn), lambda i, j: (0, j)),
            pl.BlockSpec((1, tn), lambda i, j: (0, j)),
            *r_in,
        ],
        out_specs=(pl.BlockSpec((tm, tn), lambda i, j: (i, j)), *r_out),
        out_shape=(jax.ShapeDtypeStruct((m, n), BF16), *r_shapes),
        compiler_params=_params(("arbitrary", "arbitrary")),
        name="proj_gate",
    )(xn, w_gate, b_gate, *riders)


def _gmlp_kernel(n_prompt_blocks, tr, u_ref, v_ref, lng_ref, lnb_ref, ws_ref, bst_ref,
                 a_ref, vn_ref):
    i = pl.program_id(0)
    v = v_ref[...].astype(F32)
    mu = jnp.mean(v, axis=-1, keepdims=True)
    vc = v - mu
    var = jnp.mean(vc * vc, axis=-1, keepdims=True)
    vn = vc * lax.rsqrt(var + EPS) * lng_ref[...] + lnb_ref[...]
    vnb = vn.astype(BF16)

    def mix(length):
        r = lax.broadcasted_iota(jnp.int32, (length, length), 0) // CHUNK
        c = lax.broadcasted_iota(jnp.int32, (length, length), 1) // CHUNK
        for g in range(GMLP_GROUPS):
            w = jnp.where(c <= r, ws_ref[g, :length, :length], 0.0).astype(BF16)
            bias = bst_ref[:length, g:g + 1]
            cols = slice(g * GMLP_GROUP_DIM, (g + 1) * GMLP_GROUP_DIM)
            for b in range(tr // length):
                rows = slice(b * length, (b + 1) * length)
                s = jnp.dot(w, vnb[rows, cols], preferred_element_type=F32) + bias
                a_ref[rows, cols] = (u_ref[rows, cols].astype(F32) * s).astype(a_ref.dtype)

    @pl.when(i < n_prompt_blocks)
    def _():
        mix(GMLP_CHUNK)

    @pl.when(i >= n_prompt_blocks)
    def _():
        vn_ref[...] = vn
        mix(CHUNK)


def _gmlp(uv, ln_g, ln_b, ws, bs_t, n_prompt_rows, tr=512):
    m = uv.shape[0]
    npb = n_prompt_rows // tr
    return pl.pallas_call(
        functools.partial(_gmlp_kernel, npb, tr),
        grid=(m // tr,),
        in_specs=[
            pl.BlockSpec((tr, D_MODEL), lambda i: (i, 0)),
            pl.BlockSpec((tr, D_MODEL), lambda i: (i, 1)),
            pl.BlockSpec((1, D_MODEL), lambda i: (0, 0)),
            pl.BlockSpec((1, D_MODEL), lambda i: (0, 0)),
            pl.BlockSpec((GMLP_GROUPS, GMLP_CHUNK, GMLP_CHUNK), lambda i: (0, 0, 0)),
            pl.BlockSpec((GMLP_CHUNK, GMLP_GROUPS), lambda i: (0, 0)),
        ],
        out_specs=(
            pl.BlockSpec((tr, D_MODEL), lambda i: (i, 0)),
            pl.BlockSpec((tr, D_MODEL), lambda i: (jnp.maximum(i - npb, 0), 0)),
        ),
        out_shape=(
            jax.ShapeDtypeStruct((m, D_MODEL), BF16),
            jax.ShapeDtypeStruct((m - n_prompt_rows, D_MODEL), F32),
        ),
        compiler_params=_params(("arbitrary",)),
        name="gmlp",
    )(uv, uv, ln_g, ln_b, ws, bs_t)


def _dup_head(pair, pair_swapped, low_half, odd):
    if odd:
        return jnp.where(low_half, pair_swapped, pair).astype(BF16)
    return jnp.where(low_half, pair, pair_swapped).astype(BF16)


def _attend(q_ref, kwin, vwin, mask, sink_ref, o_ref):
    rows, keys = q_ref.shape[0], kwin.shape[0]
    ones = jnp.ones((keys, LANES), BF16)
    low_q = lax.broadcasted_iota(jnp.int32, (rows, LANES), 1) < HEAD_DIM
    low_k = lax.broadcasted_iota(jnp.int32, (keys, LANES), 1) < HEAD_DIM
    heads_per_pair = LANES // HEAD_DIM
    for c in range(N_KV_HEADS // heads_per_pair):
        kpair = kwin[:, c * LANES:(c + 1) * LANES]
        vpair = vwin[:, c * LANES:(c + 1) * LANES]
        kswap = pltpu.roll(kpair, HEAD_DIM, 1)
        vswap = pltpu.roll(vpair, HEAD_DIM, 1)
        for odd in range(heads_per_pair):
            hk = c * heads_per_pair + odd
            kdup = _dup_head(kpair, kswap, low_k, odd)
            vaug = jnp.concatenate([_dup_head(vpair, vswap, low_k, odd), ones], axis=1)
            parts = []
            for r in range(Q_REP):
                h = hk * Q_REP + r
                qpair = q_ref[:, (h // 2) * LANES:(h // 2 + 1) * LANES]
                keep = low_q if h % 2 == 0 else jnp.logical_not(low_q)
                parts.append(jnp.where(keep, qpair, jnp.zeros_like(qpair)))
            s_all = lax.dot_general(jnp.concatenate(parts, axis=0), kdup, (((1,), (1,)), ((), ())),
                                    preferred_element_type=F32)
            ps, sinks, mxs = [], [], []
            for r in range(Q_REP):
                s = s_all[r * rows:(r + 1) * rows]
                if mask is not None:
                    s = mask(s)
                sink = sink_ref[hk * Q_REP + r] * LOG2E
                mx = jnp.maximum(jnp.max(s, axis=-1, keepdims=True), sink)
                ps.append(jnp.exp2(s - mx).astype(BF16))
                sinks.append(sink)
                mxs.append(mx)
            o_all = jnp.dot(jnp.concatenate(ps, axis=0), vaug, preferred_element_type=F32)
            for r in range(0, Q_REP, 2):
                h = hk * Q_REP + r
                even, oddh = o_all[r * rows:(r + 1) * rows], o_all[(r + 1) * rows:(r + 2) * rows]
                esink = jnp.exp2(jnp.where(low_q, sinks[r], sinks[r + 1])
                                 - jnp.where(low_q, mxs[r], mxs[r + 1]))
                den = jnp.where(low_q, even[:, LANES:], oddh[:, LANES:]) + esink
                num = jnp.where(low_q, even[:, :LANES], oddh[:, :LANES])
                o_ref[:, (h // 2) * LANES:(h // 2 + 1) * LANES] = (num / den).astype(o_ref.dtype)


def _attn_prompt_kernel(tq, counts, sink_ref, q_ref, kp_ref, kc_ref, vp_ref, vc_ref, *refs):
    n = len(counts)
    riders, o_ref, cast = refs[:n], refs[n], refs[n + 1:]
    t = pl.program_id(1)
    _rider_cast(pl.program_id(0) * pl.num_programs(1) + t, counts, riders, cast)
    kwin = jnp.concatenate([kp_ref[...], kc_ref[...]], axis=0)
    vwin = jnp.concatenate([vp_ref[...], vc_ref[...]], axis=0)
    row = lax.broadcasted_iota(jnp.int32, (tq, tq), 0)
    col = lax.broadcasted_iota(jnp.int32, (tq, tq), 1)
    valid_prev = ((row < CHUNK) | (col >= CHUNK)) & (t > 0)
    valid_cur_top = lax.broadcasted_iota(jnp.int32, (CHUNK, tq), 1) < CHUNK

    def mask(s):
        prev = jnp.where(valid_prev, s[:, :tq], NEG)
        cur_top = jnp.where(valid_cur_top, s[:CHUNK, tq:], NEG)
        cur = jnp.concatenate([cur_top, s[CHUNK:, tq:]], axis=0)
        return jnp.concatenate([prev, cur], axis=1)

    _attend(q_ref, kwin, vwin, mask, sink_ref, o_ref)


def _attn_prompt(sinks, q, k, va, batch, seq, riders, tq=128):
    m = q.shape[0]
    nt = seq // tq
    assert tq == WINDOW
    cur = lambda b, t: (b * nt + t, 0)
    prev = lambda b, t: (b * nt + jnp.maximum(t - 1, 0), 0)
    counts, r_in, r_out, r_shapes = _rider_specs(riders, batch * nt, lambda b, t: b * nt + t)
    return pl.pallas_call(
        functools.partial(_attn_prompt_kernel, tq, counts),
        grid=(batch, nt),
        in_specs=[
            pl.BlockSpec(memory_space=pltpu.SMEM),
            pl.BlockSpec((tq, Q_WIDTH), cur),
            pl.BlockSpec((tq, KV_WIDTH), prev),
            pl.BlockSpec((tq, KV_WIDTH), cur),
            pl.BlockSpec((tq, KV_WIDTH), prev),
            pl.BlockSpec((tq, KV_WIDTH), cur),
            *r_in,
        ],
        out_specs=(pl.BlockSpec((tq, Q_WIDTH), cur), *r_out),
        out_shape=(jax.ShapeDtypeStruct((m, Q_WIDTH), BF16), *r_shapes),
        compiler_params=_params(("arbitrary", "arbitrary")),
        name="attn_prompt",
    )(sinks, q, k, k, va, va, *riders)


def _attn_sample_kernel(sink_ref, q_ref, ck_ref, kn_ref, cv_ref, vn_ref, o_in_ref, o_ref):
    del o_in_ref
    kwin = jnp.concatenate([ck_ref[...], kn_ref[...]], axis=0)
    vwin = jnp.concatenate([cv_ref[...], vn_ref[...]], axis=0)
    _attend(q_ref, kwin, vwin, None, sink_ref, o_ref)


def _attn_sample(sinks, q, k, va, cache_k, cache_v, o, n_prompt_rows, dec_batch, dec_seq):
    first = n_prompt_rows // dec_seq
    new = lambda b: (first + b, 0)
    cached = lambda b: (b, 0, 0)
    cache_len = cache_k.shape[1]
    return pl.pallas_call(
        _attn_sample_kernel,
        grid=(dec_batch,),
        in_specs=[
            pl.BlockSpec(memory_space=pltpu.SMEM),
            pl.BlockSpec((dec_seq, Q_WIDTH), new),
            pl.BlockSpec((None, cache_len, KV_WIDTH), cached),
            pl.BlockSpec((dec_seq, KV_WIDTH), new),
            pl.BlockSpec((None, cache_len, KV_WIDTH), cached),
            pl.BlockSpec((dec_seq, KV_WIDTH), new),
            pl.BlockSpec(memory_space=pl.ANY),
        ],
        out_specs=pl.BlockSpec((dec_seq, Q_WIDTH), new),
        out_shape=jax.ShapeDtypeStruct(o.shape, o.dtype),
        input_output_aliases={6: 0},
        compiler_params=_params(("arbitrary",)),
        name="attn_sample",
    )(sinks, q, cache_k, k, cache_v, va, o)


def _branch_kernel(a_ref, o_ref, wa_ref, wb_ref, ga_ref, gb_ref, t_ref):
    ya = jnp.dot(a_ref[...], wa_ref[...], preferred_element_type=F32)
    yb = jnp.dot(o_ref[...], wb_ref[...], preferred_element_type=F32)
    t = ga_ref[...].astype(F32) * ya + gb_ref[...].astype(F32) * yb
    t_ref[...] = t.astype(t_ref.dtype)


def _branch(a, o, wa, wb, g, tm=1536, tn=512):
    m = a.shape[0]
    nj = D_MODEL // tn
    return pl.pallas_call(
        _branch_kernel,
        grid=(m // tm, nj),
        in_specs=[
            pl.BlockSpec((tm, D_MODEL), lambda i, j: (i, 0)),
            pl.BlockSpec((tm, Q_WIDTH), lambda i, j: (i, 0)),
            pl.BlockSpec((D_MODEL, tn), lambda i, j: (0, j)),
            pl.BlockSpec((Q_WIDTH, tn), lambda i, j: (0, j)),
            pl.BlockSpec((tm, tn), lambda i, j: (i, j)),
            pl.BlockSpec((tm, tn), lambda i, j: (i, j + nj)),
        ],
        out_specs=pl.BlockSpec((tm, tn), lambda i, j: (i, j)),
        out_shape=jax.ShapeDtypeStruct((m, D_MODEL), BF16),
        compiler_params=_params(("arbitrary", "arbitrary")),
        name="branch",
    )(a, o, wa, wb, g, g)


def _out_kernel(n_prompt_blocks, t_ref, w_ref, xp_ref, xs_ref, x1_ref):
    i = pl.program_id(0)
    y = jnp.dot(t_ref[...], w_ref[...], preferred_element_type=F32)

    @pl.when(i < n_prompt_blocks)
    def _():
        x1_ref[...] = xp_ref[...] + y

    @pl.when(i >= n_prompt_blocks)
    def _():
        x1_ref[...] = xs_ref[...] + y


def _out(t, w_out, xp, xs, tm=512, tn=D_MODEL):
    m = t.shape[0]
    npb = xp.shape[0] // tm
    return pl.pallas_call(
        functools.partial(_out_kernel, npb),
        grid=(m // tm, D_MODEL // tn),
        in_specs=[
            pl.BlockSpec((tm, D_MODEL), lambda i, j: (i, 0)),
            pl.BlockSpec((D_MODEL, tn), lambda i, j: (0, j)),
            pl.BlockSpec((tm, tn), lambda i, j: (jnp.minimum(i, npb - 1), j)),
            pl.BlockSpec((tm, tn), lambda i, j: (jnp.maximum(i - npb, 0), j)),
        ],
        out_specs=pl.BlockSpec((tm, tn), lambda i, j: (i, j)),
        out_shape=jax.ShapeDtypeStruct((m, D_MODEL), F32),
        compiler_params=_params(("arbitrary", "arbitrary")),
        name="out_proj",
    )(t, w_out, xp, xs)


def _ffn_kernel(nf, x1_ref, ng_ref, wg_ref, wu_ref, wd_ref, fg_ref, y_ref, h_ref, hid_ref):
    f = pl.program_id(1)

    def gate_up():
        h = h_ref[...]
        gate = jnp.dot(h, wg_ref[...], preferred_element_type=F32)
        up = jnp.dot(h, wu_ref[...], preferred_element_type=F32)
        return (jax.nn.silu(gate) * up).astype(hid_ref.dtype)

    def down():
        return jnp.dot(hid_ref[(f + 1) % 2], wd_ref[...], preferred_element_type=F32)

    @pl.when(f == 0)
    def _():
        x1 = x1_ref[...]
        h_ref[...] = _rms(x1, ng_ref[...]).astype(h_ref.dtype)
        y_ref[...] = x1
        hid_ref[0] = gate_up()

    @pl.when((f > 0) & (f < nf))
    def _():
        y_ref[...] += down()
        hid_ref[f % 2] = gate_up()

    @pl.when(f == nf)
    def _():
        y_ref[...] = _rms(y_ref[...] + down(), fg_ref[...])


def _ffn(x1, row_block_offset, n_row_blocks, norm_g, wg, wu, wd, final_g, tm=512, tf=512):
    nf = wg.shape[1] // tf
    return pl.pallas_call(
        functools.partial(_ffn_kernel, nf),
        grid=(n_row_blocks, nf + 1),
        in_specs=[
            pl.BlockSpec((tm, D_MODEL), lambda i, f: (i + row_block_offset, 0)),
            pl.BlockSpec((1, D_MODEL), lambda i, f: (0, 0)),
            pl.BlockSpec((D_MODEL, tf), lambda i, f: (0, jnp.minimum(f, nf - 1))),
            pl.BlockSpec((D_MODEL, tf), lambda i, f: (0, jnp.minimum(f, nf - 1))),
            pl.BlockSpec((tf, D_MODEL), lambda i, f: (jnp.maximum(f - 1, 0), 0)),
            pl.BlockSpec((1, D_MODEL), lambda i, f: (0, 0)),
        ],
        out_specs=pl.BlockSpec((tm, D_MODEL), lambda i, f: (i, 0)),
        out_shape=jax.ShapeDtypeStruct((n_row_blocks * tm, D_MODEL), F32),
        scratch_shapes=[pltpu.VMEM((tm, D_MODEL), BF16), pltpu.VMEM((2, tm, tf), BF16)],
        compiler_params=_params(("arbitrary", "arbitrary")),
        name="ffn",
    )(x1, norm_g, wg, wu, wd, final_g)


def _rope_tables(seq, past_len, dec_batch, dec_seq):
    pos = jnp.concatenate([jnp.arange(seq, dtype=F32),
                           jnp.tile(past_len + jnp.arange(dec_seq, dtype=F32), dec_batch)])
    inv_freq = jnp.float32(ROPE_THETA) ** (-(jnp.arange(ROT_HALF, dtype=F32) * 2.0 / ROT_DIM))
    ang = pos[:, None] * inv_freq[None, :]
    cos, sin = jnp.cos(ang), jnp.sin(ang)
    n = pos.shape[0]
    pad = jnp.zeros((n, HEAD_DIM - ROT_DIM), F32)
    zero = jnp.zeros((n, ROT_HALF), F32)
    cos_h = jnp.concatenate([cos, cos, pad + 1.0], axis=1)
    sa_h = jnp.concatenate([zero, sin, pad], axis=1)
    sb_h = jnp.concatenate([-sin, zero, pad], axis=1)
    rep = LANES // HEAD_DIM
    return jnp.tile(cos_h, (1, rep)), jnp.tile(sa_h, (1, rep)), jnp.tile(sb_h, (1, rep))


def kernel(x_prompt, x_sample, cache_swa_k, cache_swa_v, norm_mix_g, w_in, gmlp_ln_g, gmlp_ln_b,
           gmlp_ws, gmlp_bs, attn_sinks, w_gate, b_gate, w_branch_a, w_branch_b, w_out,
           norm_ffn_g, w_ffn_gate, w_ffn_up, w_ffn_down, final_norm_g):
    batch, seq, _ = x_prompt.shape
    dec_batch, dec_seq, _ = x_sample.shape
    depth = w_in.shape[0]
    past_len = 2048
    assert depth == 1 and cache_swa_k.shape[2] == WINDOW and dec_seq == CHUNK
    mp, ms = batch * seq, dec_batch * dec_seq

    xp = x_prompt.reshape(mp, D_MODEL)
    xs = x_sample.reshape(ms, D_MODEL)
    cos_t, sa_t, sb_t = _rope_tables(seq, past_len, dec_batch, dec_seq)
    row = lambda p: p.reshape(1, -1)

    l = 0
    xn = _norm(xp, xs, row(norm_mix_g[l]))
    g, w_in_b = _proj_gate(xn, w_gate[l], row(b_gate[l]), [w_in[l]])
    uv, q, k, va, wg, wu = _proj_in(xn, w_in_b, cos_t, sa_t, sb_t, mp, seq,
                                    [w_ffn_gate[l], w_ffn_up[l]])
    a, vn_s = _gmlp(uv, row(gmlp_ln_g[l]), row(gmlp_ln_b[l]), gmlp_ws[l], gmlp_bs[l].T, mp)
    o, wa_b, wb_b, wo_b, wd = _attn_prompt(
        attn_sinks[l], q, k, va, batch, seq,
        [w_branch_a[l], w_branch_b[l], w_out[l], w_ffn_down[l]])
    o = _attn_sample(attn_sinks[l], q, k, va,
                     cache_swa_k[l].reshape(dec_batch, WINDOW, KV_WIDTH),
                     cache_swa_v[l].reshape(dec_batch, WINDOW, KV_WIDTH),
                     o, mp, dec_batch, dec_seq)
    t = _branch(a, o, wa_b, wb_b, g)
    x1 = _out(t, wo_b, xp, xs)
    ffn_tm = 1024
    ffn = functools.partial(_ffn, norm_g=row(norm_ffn_g[l]), wg=wg, wu=wu, wd=wd,
                            final_g=row(final_norm_g), tm=ffn_tm)
    y_prompt = ffn(x1, 0, mp // ffn_tm)
    y_sample = ffn(x1, mp // ffn_tm, ms // ffn_tm)

    keep = min(WINDOW, seq)
    tail = lambda z: jnp.stack([z[(b + 1) * seq - keep:(b + 1) * seq] for b in range(batch)]).reshape(
        batch, keep, N_KV_HEADS, HEAD_DIM)
    kp, vp = tail(k), tail(va)
    return (
        y_prompt.reshape(batch, seq, D_MODEL),
        y_sample.reshape(dec_batch, dec_seq, D_MODEL),
        kp[None],
        vp[None],
        k[mp:].reshape(1, dec_batch, dec_seq, N_KV_HEADS, HEAD_DIM),
        va[mp:].reshape(1, dec_batch, dec_seq, N_KV_HEADS, HEAD_DIM),
        vn_s.reshape(1, dec_batch, dec_seq, D_MODEL),
    )
```

```python
import functools

import jax
import jax.numpy as jnp
from jax import lax
from jax.experimental import pallas as pl
from jax.experimental.pallas import tpu as pltpu

D_MODEL = 2048
CHUNK = 64
GMLP_CHUNK = 128
GMLP_GROUPS = 8
GMLP_GROUP_DIM = D_MODEL // GMLP_GROUPS
N_HEADS = 32
N_KV_HEADS = 4
HEAD_DIM = 64
Q_REP = N_HEADS // N_KV_HEADS
WINDOW = 128
ROPE_THETA = 500000.0
ROT_DIM = HEAD_DIM // 4
ROT_HALF = ROT_DIM // 2
Q_WIDTH = N_HEADS * HEAD_DIM
KV_WIDTH = N_KV_HEADS * HEAD_DIM
EPS = 1e-6
NEG = -1e30
LANES = 128
LOG2E = 1.4426950408889634
Q_SCALE = HEAD_DIM ** -0.5 * LOG2E

F32 = jnp.float32
BF16 = jnp.bfloat16

VMEM_LIMIT = 56 * 1024 * 1024


def _params(semantics):
    return pltpu.CompilerParams(dimension_semantics=semantics, vmem_limit_bytes=VMEM_LIMIT)


def _rms(x, g):
    return x * lax.rsqrt(jnp.mean(x * x, axis=-1, keepdims=True) + EPS) * g


BF16_SUBLANES = 16


def _rider_specs(weights, n_steps, step_of):
    counts, in_specs, out_specs, out_shapes = [], [], [], []
    for w in weights:
        rows, cols = w.shape
        nb = max(n for n in range(1, n_steps + 1)
                 if rows % n == 0 and (rows // n) % BF16_SUBLANES == 0)
        idx = lambda *g, nb=nb: (jnp.minimum(step_of(*g), nb - 1), 0)
        counts.append(nb)
        in_specs.append(pl.BlockSpec((rows // nb, cols), idx))
        out_specs.append(pl.BlockSpec((rows // nb, cols), idx))
        out_shapes.append(jax.ShapeDtypeStruct(w.shape, BF16))
    return counts, in_specs, out_specs, out_shapes


def _rider_cast(step, counts, src_refs, dst_refs):
    for nb, src, dst in zip(counts, src_refs, dst_refs):
        @pl.when(step < nb)
        def _():
            dst[...] = src[...].astype(dst.dtype)


def _norm_kernel(n_prompt_blocks, xp_ref, xs_ref, g_ref, o_ref):
    i = pl.program_id(0)

    @pl.when(i < n_prompt_blocks)
    def _():
        o_ref[...] = _rms(xp_ref[...], g_ref[...]).astype(BF16)

    @pl.when(i >= n_prompt_blocks)
    def _():
        o_ref[...] = _rms(xs_ref[...], g_ref[...]).astype(BF16)


def _norm(xp, xs, g, tr=512):
    mp, ms = xp.shape[0], xs.shape[0]
    npb, nsb = mp // tr, ms // tr
    return pl.pallas_call(
        functools.partial(_norm_kernel, npb),
        grid=(npb + nsb,),
        in_specs=[
            pl.BlockSpec((tr, D_MODEL), lambda i: (jnp.minimum(i, npb - 1), 0)),
            pl.BlockSpec((tr, D_MODEL), lambda i: (jnp.maximum(i - npb, 0), 0)),
            pl.BlockSpec((1, D_MODEL), lambda i: (0, 0)),
        ],
        out_specs=pl.BlockSpec((tr, D_MODEL), lambda i: (i, 0)),
        out_shape=jax.ShapeDtypeStruct((mp + ms, D_MODEL), BF16),
        compiler_params=_params(("arbitrary",)),
        name="norm",
    )(xp, xs, g)


def _rope(h, cos, sa, sb):
    return h * cos + pltpu.roll(h, ROT_HALF, 1) * sa + pltpu.roll(h, LANES - ROT_HALF, 1) * sb


def _proj_in_kernel(n_uv, counts, xn_ref, win_ref, wkv_ref, cos_ref, sa_ref, sb_ref, *refs):
    n = len(counts)
    riders, (uv_ref, q_ref, k_ref, va_ref), cast = refs[:n], refs[n:n + 4], refs[n + 4:]
    j = pl.program_id(1)
    _rider_cast(pl.program_id(0) * pl.num_programs(1) + j, counts, riders, cast)
    xn = xn_ref[...]

    def mm(w_ref):
        return jnp.dot(xn, w_ref[...], preferred_element_type=F32)

    def rope_cols(h, scale):
        cos, sa, sb = cos_ref[...], sa_ref[...], sb_ref[...]
        parts = [_rope(h[:, c:c + LANES], cos, sa, sb) * scale for c in range(0, h.shape[1], LANES)]
        return jnp.concatenate(parts, axis=1)

    @pl.when(j == 0)
    def _():
        h = mm(wkv_ref)
        k_ref[...] = rope_cols(h[:, :KV_WIDTH], 1.0)
        va_ref[...] = h[:, KV_WIDTH:]

    @pl.when((j >= 1) & (j <= n_uv))
    def _():
        uv_ref[...] = jax.nn.gelu(mm(win_ref)).astype(uv_ref.dtype)

    @pl.when(j > n_uv)
    def _():
        q_ref[...] = rope_cols(mm(win_ref), Q_SCALE).astype(q_ref.dtype)


def _proj_in(xn, w_in, cos_t, sa_t, sb_t, n_prompt_rows, seq, riders, tm=1024, tn=1024):
    m = xn.shape[0]
    n_uv = (2 * D_MODEL) // tn
    nq = Q_WIDTH // tn
    n_wide = n_uv + nq
    kv_block = (2 * D_MODEL + Q_WIDTH) // (2 * KV_WIDTH)
    npb = n_prompt_rows // tm
    blocks_per_seq = seq // tm

    def tab_idx(i, j):
        return (jnp.where(i < npb, i % blocks_per_seq, blocks_per_seq), 0)

    clip = lambda x, lo, hi: jnp.minimum(jnp.maximum(x, lo), hi)
    tab_spec = pl.BlockSpec((tm, LANES), tab_idx)
    n_steps = n_wide + 1
    counts, r_in, r_out, r_shapes = _rider_specs(riders, (m // tm) * n_steps,
                                                 lambda i, j: i * n_steps + j)
    out_shapes = (
        jax.ShapeDtypeStruct((m, 2 * D_MODEL), BF16),
        jax.ShapeDtypeStruct((m, Q_WIDTH), BF16),
        jax.ShapeDtypeStruct((m, KV_WIDTH), F32),
        jax.ShapeDtypeStruct((m, KV_WIDTH), F32),
        *r_shapes,
    )
    return pl.pallas_call(
        functools.partial(_proj_in_kernel, n_uv, counts),
        grid=(m // tm, n_steps),
        in_specs=[
            pl.BlockSpec((tm, D_MODEL), lambda i, j: (i, 0)),
            pl.BlockSpec((D_MODEL, tn), lambda i, j: (0, clip(j - 1, 0, n_wide - 1))),
            pl.BlockSpec((D_MODEL, 2 * KV_WIDTH), lambda i, j: (0, kv_block)),
            tab_spec, tab_spec, tab_spec,
            *r_in,
        ],
        out_specs=(
            pl.BlockSpec((tm, tn), lambda i, j: (i, clip(j - 1, 0, n_uv - 1))),
            pl.BlockSpec((tm, tn), lambda i, j: (i, clip(j - 1 - n_uv, 0, nq - 1))),
            pl.BlockSpec((tm, KV_WIDTH), lambda i, j: (i, 0)),
            pl.BlockSpec((tm, KV_WIDTH), lambda i, j: (i, 0)),
            *r_out,
        ),
        out_shape=out_shapes,
        compiler_params=_params(("arbitrary", "arbitrary")),
        name="proj_in",
    )(xn, w_in, w_in, cos_t, sa_t, sb_t, *riders)


def _proj_gate_kernel(counts, xn_ref, w_ref, b_ref, *refs):
    n = len(counts)
    riders, g_ref, cast = refs[:n], refs[n], refs[n + 1:]
    h = jnp.dot(xn_ref[...], w_ref[...].astype(BF16), preferred_element_type=F32)
    g_ref[...] = jax.nn.sigmoid(h + b_ref[...]).astype(g_ref.dtype)
    _rider_cast(pl.program_id(0) * pl.num_programs(1) + pl.program_id(1), counts, riders, cast)


def _proj_gate(xn, w_gate, b_gate, riders, tm=1536, tn=1024):
    m = xn.shape[0]
    n = w_gate.shape[1]
    nj = n // tn
    counts, r_in, r_out, r_shapes = _rider_specs(riders, (m // tm) * nj, lambda i, j: i * nj + j)
    return pl.pallas_call(
        functools.partial(_proj_gate_kernel, counts),
        grid=(m // tm, nj),
        in_specs=[
            pl.BlockSpec((tm, D_MODEL), lambda i, j: (i, 0)),
            pl.BlockSpec((D_MODEL, tn), lambda i, j: (0, j)),
            pl.BlockSpec((1, tn), lambda i, j: (0, j)),
            *r_in,
        ],
        out_specs=(pl.BlockSpec((tm, tn), lambda i, j: (i, j)), *r_out),
        out_shape=(jax.ShapeDtypeStruct((m, n), BF16), *r_shapes),
        compiler_params=_params(("arbitrary", "arbitrary")),
        name="proj_gate",
    )(xn, w_gate, b_gate, *riders)


def _gmlp_kernel(n_prompt_blocks, tr, u_ref, v_ref, lng_ref, lnb_ref, ws_ref, bst_ref,
                 a_ref, vn_ref):
    i = pl.program_id(0)
    v = v_ref[...].astype(F32)
    mu = jnp.mean(v, axis=-1, keepdims=True)
    vc = v - mu
    var = jnp.mean(vc * vc, axis=-1, keepdims=True)
    vn = vc * lax.rsqrt(var + EPS) * lng_ref[...] + lnb_ref[...]
    vnb = vn.astype(BF16)

    def mix(length):
        r = lax.broadcasted_iota(jnp.int32, (length, length), 0) // CHUNK
        c = lax.broadcasted_iota(jnp.int32, (length, length), 1) // CHUNK
        for g in range(GMLP_GROUPS):
            w = jnp.where(c <= r, ws_ref[g, :length, :length], 0.0).astype(BF16)
            bias = bst_ref[:length, g:g + 1]
            cols = slice(g * GMLP_GROUP_DIM, (g + 1) * GMLP_GROUP_DIM)
            for b in range(tr // length):
                rows = slice(b * length, (b + 1) * length)
                s = jnp.dot(w, vnb[rows, cols], preferred_element_type=F32) + bias
                a_ref[rows, cols] = (u_ref[rows, cols].astype(F32) * s).astype(a_ref.dtype)

    @pl.when(i < n_prompt_blocks)
    def _():
        mix(GMLP_CHUNK)

    @pl.when(i >= n_prompt_blocks)
    def _():
        vn_ref[...] = vn
        mix(CHUNK)


def _gmlp(uv, ln_g, ln_b, ws, bs_t, n_prompt_rows, tr=512):
    m = uv.shape[0]
    npb = n_prompt_rows // tr
    return pl.pallas_call(
        functools.partial(_gmlp_kernel, npb, tr),
        grid=(m // tr,),
        in_specs=[
            pl.BlockSpec((tr, D_MODEL), lambda i: (i, 0)),
            pl.BlockSpec((tr, D_MODEL), lambda i: (i, 1)),
            pl.BlockSpec((1, D_MODEL), lambda i: (0, 0)),
            pl.BlockSpec((1, D_MODEL), lambda i: (0, 0)),
            pl.BlockSpec((GMLP_GROUPS, GMLP_CHUNK, GMLP_CHUNK), lambda i: (0, 0, 0)),
            pl.BlockSpec((GMLP_CHUNK, GMLP_GROUPS), lambda i: (0, 0)),
        ],
        out_specs=(
            pl.BlockSpec((tr, D_MODEL), lambda i: (i, 0)),
            pl.BlockSpec((tr, D_MODEL), lambda i: (jnp.maximum(i - npb, 0), 0)),
        ),
        out_shape=(
            jax.ShapeDtypeStruct((m, D_MODEL), BF16),
            jax.ShapeDtypeStruct((m - n_prompt_rows, D_MODEL), F32),
        ),
        compiler_params=_params(("arbitrary",)),
        name="gmlp",
    )(uv, uv, ln_g, ln_b, ws, bs_t)


def _dup_head(pair, pair_swapped, low_half, odd):
    if odd:
        return jnp.where(low_half, pair_swapped, pair).astype(BF16)
    return jnp.where(low_half, pair, pair_swapped).astype(BF16)


def _attend(q_ref, kwin, vwin, mask, sink_ref, o_ref):
    rows, keys = q_ref.shape[0], kwin.shape[0]
    ones = jnp.ones((keys, LANES), BF16)
    low_q = lax.broadcasted_iota(jnp.int32, (rows, LANES), 1) < HEAD_DIM
    low_k = lax.broadcasted_iota(jnp.int32, (keys, LANES), 1) < HEAD_DIM
    heads_per_pair = LANES // HEAD_DIM
    for c in range(N_KV_HEADS // heads_per_pair):
        kpair = kwin[:, c * LANES:(c + 1) * LANES]
        vpair = vwin[:, c * LANES:(c + 1) * LANES]
        kswap = pltpu.roll(kpair, HEAD_DIM, 1)
        vswap = pltpu.roll(vpair, HEAD_DIM, 1)
        for odd in range(heads_per_pair):
            hk = c * heads_per_pair + odd
            kdup = _dup_head(kpair, kswap, low_k, odd)
            vaug = jnp.concatenate([_dup_head(vpair, vswap, low_k, odd), ones], axis=1)
            parts = []
            for r in range(Q_REP):
                h = hk * Q_REP + r
                qpair = q_ref[:, (h // 2) * LANES:(h // 2 + 1) * LANES]
                keep = low_q if h % 2 == 0 else jnp.logical_not(low_q)
                parts.append(jnp.where(keep, qpair, jnp.zeros_like(qpair)))
            s_all = lax.dot_general(jnp.concatenate(parts, axis=0), kdup, (((1,), (1,)), ((), ())),
                                    preferred_element_type=F32)
            ps, sinks, mxs = [], [], []
            for r in range(Q_REP):
                s = s_all[r * rows:(r + 1) * rows]
                if mask is not None:
                    s = mask(s)
                sink = sink_ref[hk * Q_REP + r] * LOG2E
                mx = jnp.maximum(jnp.max(s, axis=-1, keepdims=True), sink)
                ps.append(jnp.exp2(s - mx).astype(BF16))
                sinks.append(sink)
                mxs.append(mx)
            o_all = jnp.dot(jnp.concatenate(ps, axis=0), vaug, preferred_element_type=F32)
            for r in range(0, Q_REP, 2):
                h = hk * Q_REP + r
                even, oddh = o_all[r * rows:(r + 1) * rows], o_all[(r + 1) * rows:(r + 2) * rows]
                esink = jnp.exp2(jnp.where(low_q, sinks[r], sinks[r + 1])
                                 - jnp.where(low_q, mxs[r], mxs[r + 1]))
                den = jnp.where(low_q, even[:, LANES:], oddh[:, LANES:]) + esink
                num = jnp.where(low_q, even[:, :LANES], oddh[:, :LANES])
                o_ref[:, (h // 2) * LANES:(h // 2 + 1) * LANES] = (num / den).astype(o_ref.dtype)


def _attn_prompt_kernel(tq, counts, sink_ref, q_ref, kp_ref, kc_ref, vp_ref, vc_ref, *refs):
    n = len(counts)
    riders, o_ref, cast = refs[:n], refs[n], refs[n + 1:]
    t = pl.program_id(1)
    _rider_cast(pl.program_id(0) * pl.num_programs(1) + t, counts, riders, cast)
    kwin = jnp.concatenate([kp_ref[...], kc_ref[...]], axis=0)
    vwin = jnp.concatenate([vp_ref[...], vc_ref[...]], axis=0)
    row = lax.broadcasted_iota(jnp.int32, (tq, tq), 0)
    col = lax.broadcasted_iota(jnp.int32, (tq, tq), 1)
    valid_prev = ((row < CHUNK) | (col >= CHUNK)) & (t > 0)
    valid_cur_top = lax.broadcasted_iota(jnp.int32, (CHUNK, tq), 1) < CHUNK

    def mask(s):
        prev = jnp.where(valid_prev, s[:, :tq], NEG)
        cur_top = jnp.where(valid_cur_top, s[:CHUNK, tq:], NEG)
        cur = jnp.concatenate([cur_top, s[CHUNK:, tq:]], axis=0)
        return jnp.concatenate([prev, cur], axis=1)

    _attend(q_ref, kwin, vwin, mask, sink_ref, o_ref)


def _attn_prompt(sinks, q, k, va, batch, seq, riders, tq=128):
    m = q.shape[0]
    nt = seq // tq
    assert tq == WINDOW
    cur = lambda b, t: (b * nt + t, 0)
    prev = lambda b, t: (b * nt + jnp.maximum(t - 1, 0), 0)
    counts, r_in, r_out, r_shapes = _rider_specs(riders, batch * nt, lambda b, t: b * nt + t)
    return pl.pallas_call(
        functools.partial(_attn_prompt_kernel, tq, counts),
        grid=(batch, nt),
        in_specs=[
            pl.BlockSpec(memory_space=pltpu.SMEM),
            pl.BlockSpec((tq, Q_WIDTH), cur),
            pl.BlockSpec((tq, KV_WIDTH), prev),
            pl.BlockSpec((tq, KV_WIDTH), cur),
            pl.BlockSpec((tq, KV_WIDTH), prev),
            pl.BlockSpec((tq, KV_WIDTH), cur),
            *r_in,
        ],
        out_specs=(pl.BlockSpec((tq, Q_WIDTH), cur), *r_out),
        out_shape=(jax.ShapeDtypeStruct((m, Q_WIDTH), BF16), *r_shapes),
        compiler_params=_params(("arbitrary", "arbitrary")),
        name="attn_prompt",
    )(sinks, q, k, k, va, va, *riders)


def _attn_sample_kernel(sink_ref, q_ref, ck_ref, kn_ref, cv_ref, vn_ref, o_in_ref, o_ref):
    del o_in_ref
    kwin = jnp.concatenate([ck_ref[...], kn_ref[...]], axis=0)
    vwin = jnp.concatenate([cv_ref[...], vn_ref[...]], axis=0)
    _attend(q_ref, kwin, vwin, None, sink_ref, o_ref)


def _attn_sample(sinks, q, k, va, cache_k, cache_v, o, n_prompt_rows, dec_batch, dec_seq):
    first = n_prompt_rows // dec_seq
    new = lambda b: (first + b, 0)
    cached = lambda b: (b, 0, 0)
    cache_len = cache_k.shape[1]
    return pl.pallas_call(
        _attn_sample_kernel,
        grid=(dec_batch,),
        in_specs=[
            pl.BlockSpec(memory_space=pltpu.SMEM),
            pl.BlockSpec((dec_seq, Q_WIDTH), new),
            pl.BlockSpec((None, cache_len, KV_WIDTH), cached),
            pl.BlockSpec((dec_seq, KV_WIDTH), new),
            pl.BlockSpec((None, cache_len, KV_WIDTH), cached),
            pl.BlockSpec((dec_seq, KV_WIDTH), new),
            pl.BlockSpec(memory_space=pl.ANY),
        ],
        out_specs=pl.BlockSpec((dec_seq, Q_WIDTH), new),
        out_shape=jax.ShapeDtypeStruct(o.shape, o.dtype),
        input_output_aliases={6: 0},
        compiler_params=_params(("arbitrary",)),
        name="attn_sample",
    )(sinks, q, cache_k, k, cache_v, va, o)


def _branch_kernel(a_ref, o_ref, wa_ref, wb_ref, ga_ref, gb_ref, t_ref):
    ya = jnp.dot(a_ref[...], wa_ref[...], preferred_element_type=F32)
    yb = jnp.dot(o_ref[...], wb_ref[...], preferred_element_type=F32)
    t = ga_ref[...].astype(F32) * ya + gb_ref[...].astype(F32) * yb
    t_ref[...] = t.astype(t_ref.dtype)


def _branch(a, o, wa, wb, g, tm=1536, tn=512):
    m = a.shape[0]
    nj = D_MODEL // tn
    return pl.pallas_call(
        _branch_kernel,
        grid=(m // tm, nj),
        in_specs=[
            pl.BlockSpec((tm, D_MODEL), lambda i, j: (i, 0)),
            pl.BlockSpec((tm, Q_WIDTH), lambda i, j: (i, 0)),
            pl.BlockSpec((D_MODEL, tn), lambda i, j: (0, j)),
            pl.BlockSpec((Q_WIDTH, tn), lambda i, j: (0, j)),
            pl.BlockSpec((tm, tn), lambda i, j: (i, j)),
            pl.BlockSpec((tm, tn), lambda i, j: (i, j + nj)),
        ],
        out_specs=pl.BlockSpec((tm, tn), lambda i, j: (i, j)),
        out_shape=jax.ShapeDtypeStruct((m, D_MODEL), BF16),
        compiler_params=_params(("arbitrary", "arbitrary")),
        name="branch",
    )(a, o, wa, wb, g, g)


def _out_kernel(n_prompt_blocks, t_ref, w_ref, xp_ref, xs_ref, x1_ref):
    i = pl.program_id(0)
    y = jnp.dot(t_ref[...], w_ref[...], preferred_element_type=F32)

    @pl.when(i < n_prompt_blocks)
    def _():
        x1_ref[...] = xp_ref[...] + y

    @pl.when(i >= n_prompt_blocks)
    def _():
        x1_ref[...] = xs_ref[...] + y


def _out(t, w_out, xp, xs, tm=512, tn=D_MODEL):
    m = t.shape[0]
    npb = xp.shape[0] // tm
    return pl.pallas_call(
        functools.partial(_out_kernel, npb),
        grid=(m // tm, D_MODEL // tn),
        in_specs=[
            pl.BlockSpec((tm, D_MODEL), lambda i, j: (i, 0)),
            pl.BlockSpec((D_MODEL, tn), lambda i, j: (0, j)),
            pl.BlockSpec((tm, tn), lambda i, j: (jnp.minimum(i, npb - 1), j)),
            pl.BlockSpec((tm, tn), lambda i, j: (jnp.maximum(i - npb, 0), j)),
        ],
        out_specs=pl.BlockSpec((tm, tn), lambda i, j: (i, j)),
        out_shape=jax.ShapeDtypeStruct((m, D_MODEL), F32),
        compiler_params=_params(("arbitrary", "arbitrary")),
        name="out_proj",
    )(t, w_out, xp, xs)


def _ffn_kernel(x1_ref, ng_ref, wg_ref, wu_ref, wd_ref, fg_ref, y_ref, h_ref):
    f = pl.program_id(1)

    @pl.when(f == 0)
    def _():
        x1 = x1_ref[...]
        h_ref[...] = _rms(x1, ng_ref[...]).astype(h_ref.dtype)
        y_ref[...] = x1

    h = h_ref[...]
    gate = jnp.dot(h, wg_ref[...], preferred_element_type=F32)
    up = jnp.dot(h, wu_ref[...], preferred_element_type=F32)
    hid = (jax.nn.silu(gate) * up).astype(BF16)
    y_ref[...] += jnp.dot(hid, wd_ref[...], preferred_element_type=F32)

    @pl.when(f == pl.num_programs(1) - 1)
    def _():
        y_ref[...] = _rms(y_ref[...], fg_ref[...])


def _ffn(x1, row_block_offset, n_row_blocks, norm_g, wg, wu, wd, final_g, tm=512, tf=512):
    d_ff = wg.shape[1]
    return pl.pallas_call(
        _ffn_kernel,
        grid=(n_row_blocks, d_ff // tf),
        in_specs=[
            pl.BlockSpec((tm, D_MODEL), lambda i, f: (i + row_block_offset, 0)),
            pl.BlockSpec((1, D_MODEL), lambda i, f: (0, 0)),
            pl.BlockSpec((D_MODEL, tf), lambda i, f: (0, f)),
            pl.BlockSpec((D_MODEL, tf), lambda i, f: (0, f)),
            pl.BlockSpec((tf, D_MODEL), lambda i, f: (f, 0)),
            pl.BlockSpec((1, D_MODEL), lambda i, f: (0, 0)),
        ],
        out_specs=pl.BlockSpec((tm, D_MODEL), lambda i, f: (i, 0)),
        out_shape=jax.ShapeDtypeStruct((n_row_blocks * tm, D_MODEL), F32),
        scratch_shapes=[pltpu.VMEM((tm, D_MODEL), BF16)],
        compiler_params=_params(("arbitrary", "arbitrary")),
        name="ffn",
    )(x1, norm_g, wg, wu, wd, final_g)


def _rope_tables(seq, past_len, dec_batch, dec_seq):
    pos = jnp.concatenate([jnp.arange(seq, dtype=F32),
                           jnp.tile(past_len + jnp.arange(dec_seq, dtype=F32), dec_batch)])
    inv_freq = jnp.float32(ROPE_THETA) ** (-(jnp.arange(ROT_HALF, dtype=F32) * 2.0 / ROT_DIM))
    ang = pos[:, None] * inv_freq[None, :]
    cos, sin = jnp.cos(ang), jnp.sin(ang)
    n = pos.shape[0]
    pad = jnp.zeros((n, HEAD_DIM - ROT_DIM), F32)
    zero = jnp.zeros((n, ROT_HALF), F32)
    cos_h = jnp.concatenate([cos, cos, pad + 1.0], axis=1)
    sa_h = jnp.concatenate([zero, sin, pad], axis=1)
    sb_h = jnp.concatenate([-sin, zero, pad], axis=1)
    rep = LANES // HEAD_DIM
    return jnp.tile(cos_h, (1, rep)), jnp.tile(sa_h, (1, rep)), jnp.tile(sb_h, (1, rep))


def kernel(x_prompt, x_sample, cache_swa_k, cache_swa_v, norm_mix_g, w_in, gmlp_ln_g, gmlp_ln_b,
           gmlp_ws, gmlp_bs, attn_sinks, w_gate, b_gate, w_branch_a, w_branch_b, w_out,
           norm_ffn_g, w_ffn_gate, w_ffn_up, w_ffn_down, final_norm_g):
    batch, seq, _ = x_prompt.shape
    dec_batch, dec_seq, _ = x_sample.shape
    depth = w_in.shape[0]
    past_len = 2048
    assert depth == 1 and cache_swa_k.shape[2] == WINDOW and dec_seq == CHUNK
    mp, ms = batch * seq, dec_batch * dec_seq

    xp = x_prompt.reshape(mp, D_MODEL)
    xs = x_sample.reshape(ms, D_MODEL)
    cos_t, sa_t, sb_t = _rope_tables(seq, past_len, dec_batch, dec_seq)
    row = lambda p: p.reshape(1, -1)

    l = 0
    xn = _norm(xp, xs, row(norm_mix_g[l]))
    g, w_in_b = _proj_gate(xn, w_gate[l], row(b_gate[l]), [w_in[l]])
    uv, q, k, va, wg, wu = _proj_in(xn, w_in_b, cos_t, sa_t, sb_t, mp, seq,
                                    [w_ffn_gate[l], w_ffn_up[l]])
    a, vn_s = _gmlp(uv, row(gmlp_ln_g[l]), row(gmlp_ln_b[l]), gmlp_ws[l], gmlp_bs[l].T, mp)
    o, wa_b, wb_b, wo_b, wd = _attn_prompt(
        attn_sinks[l], q, k, va, batch, seq,
        [w_branch_a[l], w_branch_b[l], w_out[l], w_ffn_down[l]])
    o = _attn_sample(attn_sinks[l], q, k, va,
                     cache_swa_k[l].reshape(dec_batch, WINDOW, KV_WIDTH),
                     cache_swa_v[l].reshape(dec_batch, WINDOW, KV_WIDTH),
                     o, mp, dec_batch, dec_seq)
    t = _branch(a, o, wa_b, wb_b, g)
    x1 = _out(t, wo_b, xp, xs)
    ffn_tm = 1024
    ffn = functools.partial(_ffn, norm_g=row(norm_ffn_g[l]), wg=wg, wu=wu, wd=wd,
                            final_g=row(final_norm_g), tm=ffn_tm)
    y_prompt = ffn(x1, 0, mp // ffn_tm)
    y_sample = ffn(x1, mp // ffn_tm, ms // ffn_tm)

    keep = min(WINDOW, seq)
    tail = lambda z: jnp.stack([z[(b + 1) * seq - keep:(b + 1) * seq] for b in range(batch)]).reshape(
        batch, keep, N_KV_HEADS, HEAD_DIM)
    kp, vp = tail(k), tail(va)
    return (
        y_prompt.reshape(batch, seq, D_MODEL),
        y_sample.reshape(dec_batch, dec_seq, D_MODEL),
        kp[None],
        vp[None],
        k[mp:].reshape(1, dec_batch, dec_seq, N_KV_HEADS, HEAD_DIM),
        va[mp:].reshape(1, dec_batch, dec_seq, N_KV_HEADS, HEAD_DIM),
        vn_s.reshape(1, dec_batch, dec_seq, D_MODEL),
    )
```

```python
import functools

import jax
import jax.numpy as jnp
from jax import lax
from jax.experimental import pallas as pl
from jax.experimental.pallas import tpu as pltpu

D_MODEL = 2048
CHUNK = 64
GMLP_CHUNK = 128
GMLP_GROUPS = 8
GMLP_GROUP_DIM = D_MODEL // GMLP_GROUPS
N_HEADS = 32
N_KV_HEADS = 4
HEAD_DIM = 64
Q_REP = N_HEADS // N_KV_HEADS
WINDOW = 128
ROPE_THETA = 500000.0
ROT_DIM = HEAD_DIM // 4
ROT_HALF = ROT_DIM // 2
Q_WIDTH = N_HEADS * HEAD_DIM
KV_WIDTH = N_KV_HEADS * HEAD_DIM
EPS = 1e-6
NEG = -1e30
LANES = 128
LOG2E = 1.4426950408889634
Q_SCALE = HEAD_DIM ** -0.5 * LOG2E

F32 = jnp.float32
BF16 = jnp.bfloat16

VMEM_LIMIT = 56 * 1024 * 1024


def _params(semantics):
    return pltpu.CompilerParams(dimension_semantics=semantics, vmem_limit_bytes=VMEM_LIMIT)


def _rms(x, g):
    return x * lax.rsqrt(jnp.mean(x * x, axis=-1, keepdims=True) + EPS) * g


BF16_SUBLANES = 16


def _rider_specs(weights, n_steps, step_of):
    counts, in_specs, out_specs, out_shapes = [], [], [], []
    for w in weights:
        rows, cols = w.shape
        nb = max(n for n in range(1, n_steps + 1)
                 if rows % n == 0 and (rows // n) % BF16_SUBLANES == 0)
        idx = lambda *g, nb=nb: (jnp.minimum(step_of(*g), nb - 1), 0)
        counts.append(nb)
        in_specs.append(pl.BlockSpec((rows // nb, cols), idx))
        out_specs.append(pl.BlockSpec((rows // nb, cols), idx))
        out_shapes.append(jax.ShapeDtypeStruct(w.shape, BF16))
    return counts, in_specs, out_specs, out_shapes


def _rider_cast(step, counts, src_refs, dst_refs):
    for nb, src, dst in zip(counts, src_refs, dst_refs):
        @pl.when(step < nb)
        def _():
            dst[...] = src[...].astype(dst.dtype)


def _norm_kernel(n_prompt_blocks, xp_ref, xs_ref, g_ref, o_ref):
    i = pl.program_id(0)

    @pl.when(i < n_prompt_blocks)
    def _():
        o_ref[...] = _rms(xp_ref[...], g_ref[...]).astype(BF16)

    @pl.when(i >= n_prompt_blocks)
    def _():
        o_ref[...] = _rms(xs_ref[...], g_ref[...]).astype(BF16)


def _norm(xp, xs, g, tr=512):
    mp, ms = xp.shape[0], xs.shape[0]
    npb, nsb = mp // tr, ms // tr
    return pl.pallas_call(
        functools.partial(_norm_kernel, npb),
        grid=(npb + nsb,),
        in_specs=[
            pl.BlockSpec((tr, D_MODEL), lambda i: (jnp.minimum(i, npb - 1), 0)),
            pl.BlockSpec((tr, D_MODEL), lambda i: (jnp.maximum(i - npb, 0), 0)),
            pl.BlockSpec((1, D_MODEL), lambda i: (0, 0)),
        ],
        out_specs=pl.BlockSpec((tr, D_MODEL), lambda i: (i, 0)),
        out_shape=jax.ShapeDtypeStruct((mp + ms, D_MODEL), BF16),
        compiler_params=_params(("arbitrary",)),
        name="norm",
    )(xp, xs, g)


def _rope(h, cos, sa, sb):
    return h * cos + pltpu.roll(h, ROT_HALF, 1) * sa + pltpu.roll(h, LANES - ROT_HALF, 1) * sb


def _proj_uv_kernel(counts, xn_ref, w_ref, *refs):
    n = len(counts)
    riders, uv_ref, cast = refs[:n], refs[n], refs[n + 1:]
    _rider_cast(pl.program_id(0) * pl.num_programs(1) + pl.program_id(1), counts, riders, cast)
    h = jnp.dot(xn_ref[...], w_ref[...], preferred_element_type=F32)
    uv_ref[...] = jax.nn.gelu(h).astype(uv_ref.dtype)


def _proj_uv(xn, w_in, riders, tm=1024, tn=1024):
    m = xn.shape[0]
    nj = (2 * D_MODEL) // tn
    counts, r_in, r_out, r_shapes = _rider_specs(riders, (m // tm) * nj, lambda i, j: i * nj + j)
    return pl.pallas_call(
        functools.partial(_proj_uv_kernel, counts),
        grid=(m // tm, nj),
        in_specs=[
            pl.BlockSpec((tm, D_MODEL), lambda i, j: (i, 0)),
            pl.BlockSpec((D_MODEL, tn), lambda i, j: (0, j)),
            *r_in,
        ],
        out_specs=(pl.BlockSpec((tm, tn), lambda i, j: (i, j)), *r_out),
        out_shape=(jax.ShapeDtypeStruct((m, 2 * D_MODEL), BF16), *r_shapes),
        compiler_params=_params(("arbitrary", "arbitrary")),
        name="proj_uv",
    )(xn, w_in, *riders)


def _proj_qkv_kernel(xn_ref, wq_ref, wkv_ref, cos_ref, sa_ref, sb_ref, q_ref, k_ref, va_ref):
    j = pl.program_id(1)
    xn = xn_ref[...]

    def rope_cols(h, scale):
        cos, sa, sb = cos_ref[...], sa_ref[...], sb_ref[...]
        parts = [_rope(h[:, c:c + LANES], cos, sa, sb) * scale for c in range(0, h.shape[1], LANES)]
        return jnp.concatenate(parts, axis=1)

    @pl.when(j == 0)
    def _():
        h = jnp.dot(xn, wkv_ref[...], preferred_element_type=F32)
        k_ref[...] = rope_cols(h[:, :KV_WIDTH], 1.0)
        va_ref[...] = h[:, KV_WIDTH:]

    @pl.when(j > 0)
    def _():
        h = jnp.dot(xn, wq_ref[...], preferred_element_type=F32)
        q_ref[...] = rope_cols(h, Q_SCALE).astype(q_ref.dtype)


def _proj_qkv(xn, w_in, cos_t, sa_t, sb_t, n_prompt_rows, seq, tm=1024, tn=1024):
    m = xn.shape[0]
    nq = Q_WIDTH // tn
    q_block0 = (2 * D_MODEL) // tn
    kv_block = (2 * D_MODEL + Q_WIDTH) // (2 * KV_WIDTH)
    npb = n_prompt_rows // tm
    blocks_per_seq = seq // tm

    def tab_idx(i, j):
        return (jnp.where(i < npb, i % blocks_per_seq, blocks_per_seq), 0)

    tab_spec = pl.BlockSpec((tm, LANES), tab_idx)
    qcol = lambda j: jnp.maximum(j - 1, 0)
    return pl.pallas_call(
        _proj_qkv_kernel,
        grid=(m // tm, nq + 1),
        in_specs=[
            pl.BlockSpec((tm, D_MODEL), lambda i, j: (i, 0)),
            pl.BlockSpec((D_MODEL, tn), lambda i, j: (0, q_block0 + qcol(j))),
            pl.BlockSpec((D_MODEL, 2 * KV_WIDTH), lambda i, j: (0, kv_block)),
            tab_spec, tab_spec, tab_spec,
        ],
        out_specs=(
            pl.BlockSpec((tm, tn), lambda i, j: (i, qcol(j))),
            pl.BlockSpec((tm, KV_WIDTH), lambda i, j: (i, 0)),
            pl.BlockSpec((tm, KV_WIDTH), lambda i, j: (i, 0)),
        ),
        out_shape=(
            jax.ShapeDtypeStruct((m, Q_WIDTH), BF16),
            jax.ShapeDtypeStruct((m, KV_WIDTH), F32),
            jax.ShapeDtypeStruct((m, KV_WIDTH), F32),
        ),
        compiler_params=_params(("arbitrary", "arbitrary")),
        name="proj_qkv",
    )(xn, w_in, w_in, cos_t, sa_t, sb_t)


def _proj_gate_kernel(counts, xn_ref, w_ref, b_ref, *refs):
    n = len(counts)
    riders, g_ref, cast = refs[:n], refs[n], refs[n + 1:]
    h = jnp.dot(xn_ref[...], w_ref[...].astype(BF16), preferred_element_type=F32)
    g_ref[...] = jax.nn.sigmoid(h + b_ref[...]).astype(g_ref.dtype)
    _rider_cast(pl.program_id(0) * pl.num_programs(1) + pl.program_id(1), counts, riders, cast)


def _proj_gate(xn, w_gate, b_gate, riders, tm=1536, tn=1024):
    m = xn.shape[0]
    n = w_gate.shape[1]
    nj = n // tn
    counts, r_in, r_out, r_shapes = _rider_specs(riders, (m // tm) * nj, lambda i, j: i * nj + j)
    return pl.pallas_call(
        functools.partial(_proj_gate_kernel, counts),
        grid=(m // tm, nj),
        in_specs=[
            pl.BlockSpec((tm, D_MODEL), lambda i, j: (i, 0)),
            pl.BlockSpec((D_MODEL, tn), lambda i, j: (0, j)),
            pl.BlockSpec((1, tn), lambda i, j: (0, j)),
            *r_in,
        ],
        out_specs=(pl.BlockSpec((tm, tn), lambda i, j: (i, j)), *r_out),
        out_shape=(jax.ShapeDtypeStruct((m, n), BF16), *r_shapes),
        compiler_params=_params(("arbitrary", "arbitrary")),
        name="proj_gate",
    )(xn, w_gate, b_gate, *riders)


def _gmlp_kernel(n_prompt_blocks, tr, u_ref, v_ref, lng_ref, lnb_ref, ws_ref, bst_ref,
                 a_ref, vn_ref):
    i = pl.program_id(0)
    v = v_ref[...].astype(F32)
    mu = jnp.mean(v, axis=-1, keepdims=True)
    vc = v - mu
    var = jnp.mean(vc * vc, axis=-1, keepdims=True)
    vn = vc * lax.rsqrt(var + EPS) * lng_ref[...] + lnb_ref[...]
    vnb = vn.astype(BF16)

    def mix(length):
        r = lax.broadcasted_iota(jnp.int32, (length, length), 0) // CHUNK
        c = lax.broadcasted_iota(jnp.int32, (length, length), 1) // CHUNK
        for g in range(GMLP_GROUPS):
            w = jnp.where(c <= r, ws_ref[g, :length, :length], 0.0).astype(BF16)
            bias = bst_ref[:length, g:g + 1]
            cols = slice(g * GMLP_GROUP_DIM, (g + 1) * GMLP_GROUP_DIM)
            for b in range(tr // length):
                rows = slice(b * length, (b + 1) * length)
                s = jnp.dot(w, vnb[rows, cols], preferred_element_type=F32) + bias
                a_ref[rows, cols] = (u_ref[rows, cols].astype(F32) * s).astype(a_ref.dtype)

    @pl.when(i < n_prompt_blocks)
    def _():
        mix(GMLP_CHUNK)

    @pl.when(i >= n_prompt_blocks)
    def _():
        vn_ref[...] = vn
        mix(CHUNK)


def _gmlp(uv, ln_g, ln_b, ws, bs_t, n_prompt_rows, tr=512):
    m = uv.shape[0]
    npb = n_prompt_rows // tr
    return pl.pallas_call(
        functools.partial(_gmlp_kernel, npb, tr),
        grid=(m // tr,),
        in_specs=[
            pl.BlockSpec((tr, D_MODEL), lambda i: (i, 0)),
            pl.BlockSpec((tr, D_MODEL), lambda i: (i, 1)),
            pl.BlockSpec((1, D_MODEL), lambda i: (0, 0)),
            pl.BlockSpec((1, D_MODEL), lambda i: (0, 0)),
            pl.BlockSpec((GMLP_GROUPS, GMLP_CHUNK, GMLP_CHUNK), lambda i: (0, 0, 0)),
            pl.BlockSpec((GMLP_CHUNK, GMLP_GROUPS), lambda i: (0, 0)),
        ],
        out_specs=(
            pl.BlockSpec((tr, D_MODEL), lambda i: (i, 0)),
            pl.BlockSpec((tr, D_MODEL), lambda i: (jnp.maximum(i - npb, 0), 0)),
        ),
        out_shape=(
            jax.ShapeDtypeStruct((m, D_MODEL), BF16),
            jax.ShapeDtypeStruct((m - n_prompt_rows, D_MODEL), F32),
        ),
        compiler_params=_params(("arbitrary",)),
        name="gmlp",
    )(uv, uv, ln_g, ln_b, ws, bs_t)


def _dup_head(pair, pair_swapped, low_half, odd):
    if odd:
        return jnp.where(low_half, pair_swapped, pair).astype(BF16)
    return jnp.where(low_half, pair, pair_swapped).astype(BF16)


def _attend(q_ref, kwin, vwin, mask, sink_ref, o_ref):
    rows, keys = q_ref.shape[0], kwin.shape[0]
    ones = jnp.ones((keys, LANES), BF16)
    low_q = lax.broadcasted_iota(jnp.int32, (rows, LANES), 1) < HEAD_DIM
    low_k = lax.broadcasted_iota(jnp.int32, (keys, LANES), 1) < HEAD_DIM
    heads_per_pair = LANES // HEAD_DIM
    for c in range(N_KV_HEADS // heads_per_pair):
        kpair = kwin[:, c * LANES:(c + 1) * LANES]
        vpair = vwin[:, c * LANES:(c + 1) * LANES]
        kswap = pltpu.roll(kpair, HEAD_DIM, 1)
        vswap = pltpu.roll(vpair, HEAD_DIM, 1)
        for odd in range(heads_per_pair):
            hk = c * heads_per_pair + odd
            kdup = _dup_head(kpair, kswap, low_k, odd)
            vaug = jnp.concatenate([_dup_head(vpair, vswap, low_k, odd), ones], axis=1)
            parts = []
            for r in range(Q_REP):
                h = hk * Q_REP + r
                qpair = q_ref[:, (h // 2) * LANES:(h // 2 + 1) * LANES]
                keep = low_q if h % 2 == 0 else jnp.logical_not(low_q)
                parts.append(jnp.where(keep, qpair, jnp.zeros_like(qpair)))
            s_all = lax.dot_general(jnp.concatenate(parts, axis=0), kdup, (((1,), (1,)), ((), ())),
                                    preferred_element_type=F32)
            ps, sinks, mxs = [], [], []
            for r in range(Q_REP):
                s = s_all[r * rows:(r + 1) * rows]
                if mask is not None:
                    s = mask(s)
                sink = sink_ref[hk * Q_REP + r] * LOG2E
                mx = jnp.maximum(jnp.max(s, axis=-1, keepdims=True), sink)
                ps.append(jnp.exp2(s - mx).astype(BF16))
                sinks.append(sink)
                mxs.append(mx)
            o_all = jnp.dot(jnp.concatenate(ps, axis=0), vaug, preferred_element_type=F32)
            for r in range(0, Q_REP, 2):
                h = hk * Q_REP + r
                even, oddh = o_all[r * rows:(r + 1) * rows], o_all[(r + 1) * rows:(r + 2) * rows]
                esink = jnp.exp2(jnp.where(low_q, sinks[r], sinks[r + 1])
                                 - jnp.where(low_q, mxs[r], mxs[r + 1]))
                den = jnp.where(low_q, even[:, LANES:], oddh[:, LANES:]) + esink
                num = jnp.where(low_q, even[:, :LANES], oddh[:, :LANES])
                o_ref[:, (h // 2) * LANES:(h // 2 + 1) * LANES] = (num / den).astype(o_ref.dtype)


def _attn_prompt_kernel(tq, counts, sink_ref, q_ref, kp_ref, kc_ref, vp_ref, vc_ref, *refs):
    n = len(counts)
    riders, o_ref, cast = refs[:n], refs[n], refs[n + 1:]
    t = pl.program_id(1)
    _rider_cast(pl.program_id(0) * pl.num_programs(1) + t, counts, riders, cast)
    kwin = jnp.concatenate([kp_ref[...], kc_ref[...]], axis=0)
    vwin = jnp.concatenate([vp_ref[...], vc_ref[...]], axis=0)
    row = lax.broadcasted_iota(jnp.int32, (tq, tq), 0)
    col = lax.broadcasted_iota(jnp.int32, (tq, tq), 1)
    valid_prev = ((row < CHUNK) | (col >= CHUNK)) & (t > 0)
    valid_cur_top = lax.broadcasted_iota(jnp.int32, (CHUNK, tq), 1) < CHUNK

    def mask(s):
        prev = jnp.where(valid_prev, s[:, :tq], NEG)
        cur_top = jnp.where(valid_cur_top, s[:CHUNK, tq:], NEG)
        cur = jnp.concatenate([cur_top, s[CHUNK:, tq:]], axis=0)
        return jnp.concatenate([prev, cur], axis=1)

    _attend(q_ref, kwin, vwin, mask, sink_ref, o_ref)


def _attn_prompt(sinks, q, k, va, batch, seq, riders, tq=128):
    m = q.shape[0]
    nt = seq // tq
    assert tq == WINDOW
    cur = lambda b, t: (b * nt + t, 0)
    prev = lambda b, t: (b * nt + jnp.maximum(t - 1, 0), 0)
    counts, r_in, r_out, r_shapes = _rider_specs(riders, batch * nt, lambda b, t: b * nt + t)
    return pl.pallas_call(
        functools.partial(_attn_prompt_kernel, tq, counts),
        grid=(batch, nt),
        in_specs=[
            pl.BlockSpec(memory_space=pltpu.SMEM),
            pl.BlockSpec((tq, Q_WIDTH), cur),
            pl.BlockSpec((tq, KV_WIDTH), prev),
            pl.BlockSpec((tq, KV_WIDTH), cur),
            pl.BlockSpec((tq, KV_WIDTH), prev),
            pl.BlockSpec((tq, KV_WIDTH), cur),
            *r_in,
        ],
        out_specs=(pl.BlockSpec((tq, Q_WIDTH), cur), *r_out),
        out_shape=(jax.ShapeDtypeStruct((m, Q_WIDTH), BF16), *r_shapes),
        compiler_params=_params(("arbitrary", "arbitrary")),
        name="attn_prompt",
    )(sinks, q, k, k, va, va, *riders)


def _attn_sample_kernel(sink_ref, q_ref, ck_ref, kn_ref, cv_ref, vn_ref, o_in_ref, o_ref):
    del o_in_ref
    kwin = jnp.concatenate([ck_ref[...], kn_ref[...]], axis=0)
    vwin = jnp.concatenate([cv_ref[...], vn_ref[...]], axis=0)
    _attend(q_ref, kwin, vwin, None, sink_ref, o_ref)


def _attn_sample(sinks, q, k, va, cache_k, cache_v, o, n_prompt_rows, dec_batch, dec_seq):
    first = n_prompt_rows // dec_seq
    new = lambda b: (first + b, 0)
    cached = lambda b: (b, 0, 0)
    cache_len = cache_k.shape[1]
    return pl.pallas_call(
        _attn_sample_kernel,
        grid=(dec_batch,),
        in_specs=[
            pl.BlockSpec(memory_space=pltpu.SMEM),
            pl.BlockSpec((dec_seq, Q_WIDTH), new),
            pl.BlockSpec((None, cache_len, KV_WIDTH), cached),
            pl.BlockSpec((dec_seq, KV_WIDTH), new),
            pl.BlockSpec((None, cache_len, KV_WIDTH), cached),
            pl.BlockSpec((dec_seq, KV_WIDTH), new),
            pl.BlockSpec(memory_space=pl.ANY),
        ],
        out_specs=pl.BlockSpec((dec_seq, Q_WIDTH), new),
        out_shape=jax.ShapeDtypeStruct(o.shape, o.dtype),
        input_output_aliases={6: 0},
        compiler_params=_params(("arbitrary",)),
        name="attn_sample",
    )(sinks, q, cache_k, k, cache_v, va, o)


def _branch_kernel(a_ref, o_ref, wa_ref, wb_ref, ga_ref, gb_ref, t_ref):
    ya = jnp.dot(a_ref[...], wa_ref[...], preferred_element_type=F32)
    yb = jnp.dot(o_ref[...], wb_ref[...], preferred_element_type=F32)
    t = ga_ref[...].astype(F32) * ya + gb_ref[...].astype(F32) * yb
    t_ref[...] = t.astype(t_ref.dtype)


def _branch(a, o, wa, wb, g, tm=1536, tn=512):
    m = a.shape[0]
    nj = D_MODEL // tn
    return pl.pallas_call(
        _branch_kernel,
        grid=(m // tm, nj),
        in_specs=[
            pl.BlockSpec((tm, D_MODEL), lambda i, j: (i, 0)),
            pl.BlockSpec((tm, Q_WIDTH), lambda i, j: (i, 0)),
            pl.BlockSpec((D_MODEL, tn), lambda i, j: (0, j)),
            pl.BlockSpec((Q_WIDTH, tn), lambda i, j: (0, j)),
            pl.BlockSpec((tm, tn), lambda i, j: (i, j)),
            pl.BlockSpec((tm, tn), lambda i, j: (i, j + nj)),
        ],
        out_specs=pl.BlockSpec((tm, tn), lambda i, j: (i, j)),
        out_shape=jax.ShapeDtypeStruct((m, D_MODEL), BF16),
        compiler_params=_params(("arbitrary", "arbitrary")),
        name="branch",
    )(a, o, wa, wb, g, g)


def _out_kernel(n_prompt_blocks, t_ref, w_ref, xp_ref, xs_ref, x1_ref):
    i = pl.program_id(0)
    y = jnp.dot(t_ref[...], w_ref[...], preferred_element_type=F32)

    @pl.when(i < n_prompt_blocks)
    def _():
        x1_ref[...] = xp_ref[...] + y

    @pl.when(i >= n_prompt_blocks)
    def _():
        x1_ref[...] = xs_ref[...] + y


def _out(t, w_out, xp, xs, tm=512, tn=D_MODEL):
    m = t.shape[0]
    npb = xp.shape[0] // tm
    return pl.pallas_call(
        functools.partial(_out_kernel, npb),
        grid=(m // tm, D_MODEL // tn),
        in_specs=[
            pl.BlockSpec((tm, D_MODEL), lambda i, j: (i, 0)),
            pl.BlockSpec((D_MODEL, tn), lambda i, j: (0, j)),
            pl.BlockSpec((tm, tn), lambda i, j: (jnp.minimum(i, npb - 1), j)),
            pl.BlockSpec((tm, tn), lambda i, j: (jnp.maximum(i - npb, 0), j)),
        ],
        out_specs=pl.BlockSpec((tm, tn), lambda i, j: (i, j)),
        out_shape=jax.ShapeDtypeStruct((m, D_MODEL), F32),
        compiler_params=_params(("arbitrary", "arbitrary")),
        name="out_proj",
    )(t, w_out, xp, xs)


def _ffn_kernel(x1_ref, ng_ref, wg_ref, wu_ref, wd_ref, fg_ref, y_ref, h_ref):
    f = pl.program_id(1)

    @pl.when(f == 0)
    def _():
        x1 = x1_ref[...]
        h_ref[...] = _rms(x1, ng_ref[...]).astype(h_ref.dtype)
        y_ref[...] = x1

    h = h_ref[...]
    gate = jnp.dot(h, wg_ref[...], preferred_element_type=F32)
    up = jnp.dot(h, wu_ref[...], preferred_element_type=F32)
    hid = (jax.nn.silu(gate) * up).astype(BF16)
    y_ref[...] += jnp.dot(hid, wd_ref[...], preferred_element_type=F32)

    @pl.when(f == pl.num_programs(1) - 1)
    def _():
        y_ref[...] = _rms(y_ref[...], fg_ref[...])


def _ffn(x1, row_block_offset, n_row_blocks, norm_g, wg, wu, wd, final_g, tm=512, tf=512):
    d_ff = wg.shape[1]
    return pl.pallas_call(
        _ffn_kernel,
        grid=(n_row_blocks, d_ff // tf),
        in_specs=[
            pl.BlockSpec((tm, D_MODEL), lambda i, f: (i + row_block_offset, 0)),
            pl.BlockSpec((1, D_MODEL), lambda i, f: (0, 0)),
            pl.BlockSpec((D_MODEL, tf), lambda i, f: (0, f)),
            pl.BlockSpec((D_MODEL, tf), lambda i, f: (0, f)),
            pl.BlockSpec((tf, D_MODEL), lambda i, f: (f, 0)),
            pl.BlockSpec((1, D_MODEL), lambda i, f: (0, 0)),
        ],
        out_specs=pl.BlockSpec((tm, D_MODEL), lambda i, f: (i, 0)),
        out_shape=jax.ShapeDtypeStruct((n_row_blocks * tm, D_MODEL), F32),
        scratch_shapes=[pltpu.VMEM((tm, D_MODEL), BF16)],
        compiler_params=_params(("arbitrary", "arbitrary")),
        name="ffn",
    )(x1, norm_g, wg, wu, wd, final_g)


def _rope_tables(seq, past_len, dec_batch, dec_seq):
    pos = jnp.concatenate([jnp.arange(seq, dtype=F32),
                           jnp.tile(past_len + jnp.arange(dec_seq, dtype=F32), dec_batch)])
    inv_freq = jnp.float32(ROPE_THETA) ** (-(jnp.arange(ROT_HALF, dtype=F32) * 2.0 / ROT_DIM))
    ang = pos[:, None] * inv_freq[None, :]
    cos, sin = jnp.cos(ang), jnp.sin(ang)
    n = pos.shape[0]
    pad = jnp.zeros((n, HEAD_DIM - ROT_DIM), F32)
    zero = jnp.zeros((n, ROT_HALF), F32)
    cos_h = jnp.concatenate([cos, cos, pad + 1.0], axis=1)
    sa_h = jnp.concatenate([zero, sin, pad], axis=1)
    sb_h = jnp.concatenate([-sin, zero, pad], axis=1)
    rep = LANES // HEAD_DIM
    return jnp.tile(cos_h, (1, rep)), jnp.tile(sa_h, (1, rep)), jnp.tile(sb_h, (1, rep))


def kernel(x_prompt, x_sample, cache_swa_k, cache_swa_v, norm_mix_g, w_in, gmlp_ln_g, gmlp_ln_b,
           gmlp_ws, gmlp_bs, attn_sinks, w_gate, b_gate, w_branch_a, w_branch_b, w_out,
           norm_ffn_g, w_ffn_gate, w_ffn_up, w_ffn_down, final_norm_g):
    batch, seq, _ = x_prompt.shape
    dec_batch, dec_seq, _ = x_sample.shape
    depth = w_in.shape[0]
    past_len = 2048
    assert depth == 1 and cache_swa_k.shape[2] == WINDOW and dec_seq == CHUNK
    mp, ms = batch * seq, dec_batch * dec_seq

    xp = x_prompt.reshape(mp, D_MODEL)
    xs = x_sample.reshape(ms, D_MODEL)
    cos_t, sa_t, sb_t = _rope_tables(seq, past_len, dec_batch, dec_seq)
    row = lambda p: p.reshape(1, -1)

    l = 0
    xn = _norm(xp, xs, row(norm_mix_g[l]))
    g, w_in_b = _proj_gate(xn, w_gate[l], row(b_gate[l]), [w_in[l]])
    q, k, va = _proj_qkv(xn, w_in_b, cos_t, sa_t, sb_t, mp, seq)
    uv, wg, wu = _proj_uv(xn, w_in_b, [w_ffn_gate[l], w_ffn_up[l]])
    a, vn_s = _gmlp(uv, row(gmlp_ln_g[l]), row(gmlp_ln_b[l]), gmlp_ws[l], gmlp_bs[l].T, mp)
    o, wa_b, wb_b, wo_b, wd = _attn_prompt(
        attn_sinks[l], q, k, va, batch, seq,
        [w_branch_a[l], w_branch_b[l], w_out[l], w_ffn_down[l]])
    o = _attn_sample(attn_sinks[l], q, k, va,
                     cache_swa_k[l].reshape(dec_batch, WINDOW, KV_WIDTH),
                     cache_swa_v[l].reshape(dec_batch, WINDOW, KV_WIDTH),
                     o, mp, dec_batch, dec_seq)
    t = _branch(a, o, wa_b, wb_b, g)
    x1 = _out(t, wo_b, xp, xs)
    ffn_tm = 1024
    ffn = functools.partial(_ffn, norm_g=row(norm_ffn_g[l]), wg=wg, wu=wu, wd=wd,
                            final_g=row(final_norm_g), tm=ffn_tm)
    y_prompt = ffn(x1, 0, mp // ffn_tm)
    y_sample = ffn(x1, mp // ffn_tm, ms // ffn_tm)

    keep = min(WINDOW, seq)
    tail = lambda z: jnp.stack([z[(b + 1) * seq - keep:(b + 1) * seq] for b in range(batch)]).reshape(
        batch, keep, N_KV_HEADS, HEAD_DIM)
    kp, vp = tail(k), tail(va)
    return (
        y_prompt.reshape(batch, seq, D_MODEL),
        y_sample.reshape(dec_batch, dec_seq, D_MODEL),
        kp[None],
        vp[None],
        k[mp:].reshape(1, dec_batch, dec_seq, N_KV_HEADS, HEAD_DIM),
        va[mp:].reshape(1, dec_batch, dec_seq, N_KV_HEADS, HEAD_DIM),
        vn_s.reshape(1, dec_batch, dec_seq, D_MODEL),
    )
```

```python
import functools

import jax
import jax.numpy as jnp
from jax import lax
from jax.experimental import pallas as pl
from jax.experimental.pallas import tpu as pltpu

D_MODEL = 2048
CHUNK = 64
GMLP_CHUNK = 128
GMLP_GROUPS = 8
GMLP_GROUP_DIM = D_MODEL // GMLP_GROUPS
N_HEADS = 32
N_KV_HEADS = 4
HEAD_DIM = 64
Q_REP = N_HEADS // N_KV_HEADS
WINDOW = 128
ROPE_THETA = 500000.0
ROT_DIM = HEAD_DIM // 4
ROT_HALF = ROT_DIM // 2
Q_WIDTH = N_HEADS * HEAD_DIM
KV_WIDTH = N_KV_HEADS * HEAD_DIM
EPS = 1e-6
NEG = -1e30
LANES = 128
LOG2E = 1.4426950408889634
Q_SCALE = HEAD_DIM ** -0.5 * LOG2E

F32 = jnp.float32
BF16 = jnp.bfloat16

VMEM_LIMIT = 56 * 1024 * 1024


def _params(semantics):
    return pltpu.CompilerParams(dimension_semantics=semantics, vmem_limit_bytes=VMEM_LIMIT)


def _rms(x, g):
    return x * lax.rsqrt(jnp.mean(x * x, axis=-1, keepdims=True) + EPS) * g


BF16_SUBLANES = 16


def _rider_specs(weights, n_steps, step_of):
    counts, in_specs, out_specs, out_shapes = [], [], [], []
    for w in weights:
        rows, cols = w.shape
        nb = max(n for n in range(1, n_steps + 1)
                 if rows % n == 0 and (rows // n) % BF16_SUBLANES == 0)
        idx = lambda *g, nb=nb: (jnp.minimum(step_of(*g), nb - 1), 0)
        counts.append(nb)
        in_specs.append(pl.BlockSpec((rows // nb, cols), idx))
        out_specs.append(pl.BlockSpec((rows // nb, cols), idx))
        out_shapes.append(jax.ShapeDtypeStruct(w.shape, BF16))
    return counts, in_specs, out_specs, out_shapes


def _rider_cast(step, counts, src_refs, dst_refs):
    for nb, src, dst in zip(counts, src_refs, dst_refs):
        @pl.when(step < nb)
        def _():
            dst[...] = src[...].astype(dst.dtype)


def _norm_kernel(n_prompt_blocks, xp_ref, xs_ref, g_ref, o_ref):
    i = pl.program_id(0)

    @pl.when(i < n_prompt_blocks)
    def _():
        o_ref[...] = _rms(xp_ref[...], g_ref[...]).astype(BF16)

    @pl.when(i >= n_prompt_blocks)
    def _():
        o_ref[...] = _rms(xs_ref[...], g_ref[...]).astype(BF16)


def _norm(xp, xs, g, tr=512):
    mp, ms = xp.shape[0], xs.shape[0]
    npb, nsb = mp // tr, ms // tr
    return pl.pallas_call(
        functools.partial(_norm_kernel, npb),
        grid=(npb + nsb,),
        in_specs=[
            pl.BlockSpec((tr, D_MODEL), lambda i: (jnp.minimum(i, npb - 1), 0)),
            pl.BlockSpec((tr, D_MODEL), lambda i: (jnp.maximum(i - npb, 0), 0)),
            pl.BlockSpec((1, D_MODEL), lambda i: (0, 0)),
        ],
        out_specs=pl.BlockSpec((tr, D_MODEL), lambda i: (i, 0)),
        out_shape=jax.ShapeDtypeStruct((mp + ms, D_MODEL), BF16),
        compiler_params=_params(("arbitrary",)),
        name="norm",
    )(xp, xs, g)


def _rope(h, cos, sa, sb):
    return h * cos + pltpu.roll(h, ROT_HALF, 1) * sa + pltpu.roll(h, LANES - ROT_HALF, 1) * sb


def _proj_uv_kernel(counts, xn_ref, w_ref, *refs):
    n = len(counts)
    riders, uv_ref, cast = refs[:n], refs[n], refs[n + 1:]
    _rider_cast(pl.program_id(0) * pl.num_programs(1) + pl.program_id(1), counts, riders, cast)
    h = jnp.dot(xn_ref[...], w_ref[...], preferred_element_type=F32)
    uv_ref[...] = jax.nn.gelu(h).astype(uv_ref.dtype)


def _proj_uv(xn, w_in, riders, tm=1024, tn=1024):
    m = xn.shape[0]
    nj = (2 * D_MODEL) // tn
    counts, r_in, r_out, r_shapes = _rider_specs(riders, (m // tm) * nj, lambda i, j: i * nj + j)
    return pl.pallas_call(
        functools.partial(_proj_uv_kernel, counts),
        grid=(m // tm, nj),
        in_specs=[
            pl.BlockSpec((tm, D_MODEL), lambda i, j: (i, 0)),
            pl.BlockSpec((D_MODEL, tn), lambda i, j: (0, j)),
            *r_in,
        ],
        out_specs=(pl.BlockSpec((tm, tn), lambda i, j: (i, j)), *r_out),
        out_shape=(jax.ShapeDtypeStruct((m, 2 * D_MODEL), BF16), *r_shapes),
        compiler_params=_params(("arbitrary", "arbitrary")),
        name="proj_uv",
    )(xn, w_in, *riders)


def _proj_qkv_kernel(xn_ref, wq_ref, wkv_ref, cos_ref, sa_ref, sb_ref, q_ref, k_ref, va_ref):
    j = pl.program_id(1)
    xn = xn_ref[...]

    def rope_cols(h, scale):
        cos, sa, sb = cos_ref[...], sa_ref[...], sb_ref[...]
        parts = [_rope(h[:, c:c + LANES], cos, sa, sb) * scale for c in range(0, h.shape[1], LANES)]
        return jnp.concatenate(parts, axis=1)

    @pl.when(j == 0)
    def _():
        h = jnp.dot(xn, wkv_ref[...], preferred_element_type=F32)
        k_ref[...] = rope_cols(h[:, :KV_WIDTH], 1.0)
        va_ref[...] = h[:, KV_WIDTH:]

    @pl.when(j > 0)
    def _():
        h = jnp.dot(xn, wq_ref[...], preferred_element_type=F32)
        q_ref[...] = rope_cols(h, Q_SCALE).astype(q_ref.dtype)


def _proj_qkv(xn, w_in, cos_t, sa_t, sb_t, n_prompt_rows, seq, tm=1024, tn=Q_WIDTH):
    m = xn.shape[0]
    nq = Q_WIDTH // tn
    q_block0 = (2 * D_MODEL) // tn
    kv_block = (2 * D_MODEL + Q_WIDTH) // (2 * KV_WIDTH)
    npb = n_prompt_rows // tm
    blocks_per_seq = seq // tm

    def tab_idx(i, j):
        return (jnp.where(i < npb, i % blocks_per_seq, blocks_per_seq), 0)

    tab_spec = pl.BlockSpec((tm, LANES), tab_idx)
    qcol = lambda j: jnp.maximum(j - 1, 0)
    return pl.pallas_call(
        _proj_qkv_kernel,
        grid=(m // tm, nq + 1),
        in_specs=[
            pl.BlockSpec((tm, D_MODEL), lambda i, j: (i, 0)),
            pl.BlockSpec((D_MODEL, tn), lambda i, j: (0, q_block0 + qcol(j))),
            pl.BlockSpec((D_MODEL, 2 * KV_WIDTH), lambda i, j: (0, kv_block)),
            tab_spec, tab_spec, tab_spec,
        ],
        out_specs=(
            pl.BlockSpec((tm, tn), lambda i, j: (i, qcol(j))),
            pl.BlockSpec((tm, KV_WIDTH), lambda i, j: (i, 0)),
            pl.BlockSpec((tm, KV_WIDTH), lambda i, j: (i, 0)),
        ),
        out_shape=(
            jax.ShapeDtypeStruct((m, Q_WIDTH), BF16),
            jax.ShapeDtypeStruct((m, KV_WIDTH), F32),
            jax.ShapeDtypeStruct((m, KV_WIDTH), F32),
        ),
        compiler_params=_params(("arbitrary", "arbitrary")),
        name="proj_qkv",
    )(xn, w_in, w_in, cos_t, sa_t, sb_t)


def _proj_gate_kernel(counts, xn_ref, w_ref, b_ref, *refs):
    n = len(counts)
    riders, g_ref, cast = refs[:n], refs[n], refs[n + 1:]
    h = jnp.dot(xn_ref[...], w_ref[...].astype(BF16), preferred_element_type=F32)
    g_ref[...] = jax.nn.sigmoid(h + b_ref[...]).astype(g_ref.dtype)
    _rider_cast(pl.program_id(0) * pl.num_programs(1) + pl.program_id(1), counts, riders, cast)


def _proj_gate(xn, w_gate, b_gate, riders, tm=1536, tn=1024):
    m = xn.shape[0]
    n = w_gate.shape[1]
    nj = n // tn
    counts, r_in, r_out, r_shapes = _rider_specs(riders, (m // tm) * nj, lambda i, j: i * nj + j)
    return pl.pallas_call(
        functools.partial(_proj_gate_kernel, counts),
        grid=(m // tm, nj),
        in_specs=[
            pl.BlockSpec((tm, D_MODEL), lambda i, j: (i, 0)),
            pl.BlockSpec((D_MODEL, tn), lambda i, j: (0, j)),
            pl.BlockSpec((1, tn), lambda i, j: (0, j)),
            *r_in,
        ],
        out_specs=(pl.BlockSpec((tm, tn), lambda i, j: (i, j)), *r_out),
        out_shape=(jax.ShapeDtypeStruct((m, n), BF16), *r_shapes),
        compiler_params=_params(("arbitrary", "arbitrary")),
        name="proj_gate",
    )(xn, w_gate, b_gate, *riders)


def _gmlp_kernel(n_prompt_blocks, tr, u_ref, v_ref, lng_ref, lnb_ref, ws_ref, bst_ref,
                 a_ref, vn_ref):
    i = pl.program_id(0)
    v = v_ref[...].astype(F32)
    mu = jnp.mean(v, axis=-1, keepdims=True)
    vc = v - mu
    var = jnp.mean(vc * vc, axis=-1, keepdims=True)
    vn = vc * lax.rsqrt(var + EPS) * lng_ref[...] + lnb_ref[...]
    vnb = vn.astype(BF16)

    def mix(length):
        r = lax.broadcasted_iota(jnp.int32, (length, length), 0) // CHUNK
        c = lax.broadcasted_iota(jnp.int32, (length, length), 1) // CHUNK
        for g in range(GMLP_GROUPS):
            w = jnp.where(c <= r, ws_ref[g, :length, :length], 0.0).astype(BF16)
            bias = bst_ref[:length, g:g + 1]
            cols = slice(g * GMLP_GROUP_DIM, (g + 1) * GMLP_GROUP_DIM)
            for b in range(tr // length):
                rows = slice(b * length, (b + 1) * length)
                s = jnp.dot(w, vnb[rows, cols], preferred_element_type=F32) + bias
                a_ref[rows, cols] = (u_ref[rows, cols].astype(F32) * s).astype(a_ref.dtype)

    @pl.when(i < n_prompt_blocks)
    def _():
        mix(GMLP_CHUNK)

    @pl.when(i >= n_prompt_blocks)
    def _():
        vn_ref[...] = vn
        mix(CHUNK)


def _gmlp(uv, ln_g, ln_b, ws, bs_t, n_prompt_rows, tr=512):
    m = uv.shape[0]
    npb = n_prompt_rows // tr
    return pl.pallas_call(
        functools.partial(_gmlp_kernel, npb, tr),
        grid=(m // tr,),
        in_specs=[
            pl.BlockSpec((tr, D_MODEL), lambda i: (i, 0)),
            pl.BlockSpec((tr, D_MODEL), lambda i: (i, 1)),
            pl.BlockSpec((1, D_MODEL), lambda i: (0, 0)),
            pl.BlockSpec((1, D_MODEL), lambda i: (0, 0)),
            pl.BlockSpec((GMLP_GROUPS, GMLP_CHUNK, GMLP_CHUNK), lambda i: (0, 0, 0)),
            pl.BlockSpec((GMLP_CHUNK, GMLP_GROUPS), lambda i: (0, 0)),
        ],
        out_specs=(
            pl.BlockSpec((tr, D_MODEL), lambda i: (i, 0)),
            pl.BlockSpec((tr, D_MODEL), lambda i: (jnp.maximum(i - npb, 0), 0)),
        ),
        out_shape=(
            jax.ShapeDtypeStruct((m, D_MODEL), BF16),
            jax.ShapeDtypeStruct((m - n_prompt_rows, D_MODEL), F32),
        ),
        compiler_params=_params(("arbitrary",)),
        name="gmlp",
    )(uv, uv, ln_g, ln_b, ws, bs_t)


def _dup_head(pair, pair_swapped, low_half, odd):
    if odd:
        return jnp.where(low_half, pair_swapped, pair).astype(BF16)
    return jnp.where(low_half, pair, pair_swapped).astype(BF16)


def _attend(q_ref, kwin, vwin, mask, sink_ref, o_ref):
    rows, keys = q_ref.shape[0], kwin.shape[0]
    ones = jnp.ones((keys, LANES), BF16)
    low_q = lax.broadcasted_iota(jnp.int32, (rows, LANES), 1) < HEAD_DIM
    low_k = lax.broadcasted_iota(jnp.int32, (keys, LANES), 1) < HEAD_DIM
    heads_per_pair = LANES // HEAD_DIM
    for c in range(N_KV_HEADS // heads_per_pair):
        kpair = kwin[:, c * LANES:(c + 1) * LANES]
        vpair = vwin[:, c * LANES:(c + 1) * LANES]
        kswap = pltpu.roll(kpair, HEAD_DIM, 1)
        vswap = pltpu.roll(vpair, HEAD_DIM, 1)
        for odd in range(heads_per_pair):
            hk = c * heads_per_pair + odd
            kdup = _dup_head(kpair, kswap, low_k, odd)
            vaug = jnp.concatenate([_dup_head(vpair, vswap, low_k, odd), ones], axis=1)
            parts = []
            for r in range(Q_REP):
                h = hk * Q_REP + r
                qpair = q_ref[:, (h // 2) * LANES:(h // 2 + 1) * LANES]
                keep = low_q if h % 2 == 0 else jnp.logical_not(low_q)
                parts.append(jnp.where(keep, qpair, jnp.zeros_like(qpair)))
            s_all = lax.dot_general(jnp.concatenate(parts, axis=0), kdup, (((1,), (1,)), ((), ())),
                                    preferred_element_type=F32)
            ps, sinks, mxs = [], [], []
            for r in range(Q_REP):
                s = s_all[r * rows:(r + 1) * rows]
                if mask is not None:
                    s = mask(s)
                sink = sink_ref[hk * Q_REP + r] * LOG2E
                mx = jnp.maximum(jnp.max(s, axis=-1, keepdims=True), sink)
                ps.append(jnp.exp2(s - mx).astype(BF16))
                sinks.append(sink)
                mxs.append(mx)
            o_all = jnp.dot(jnp.concatenate(ps, axis=0), vaug, preferred_element_type=F32)
            for r in range(0, Q_REP, 2):
                h = hk * Q_REP + r
                even, oddh = o_all[r * rows:(r + 1) * rows], o_all[(r + 1) * rows:(r + 2) * rows]
                esink = jnp.exp2(jnp.where(low_q, sinks[r], sinks[r + 1])
                                 - jnp.where(low_q, mxs[r], mxs[r + 1]))
                den = jnp.where(low_q, even[:, LANES:], oddh[:, LANES:]) + esink
                num = jnp.where(low_q, even[:, :LANES], oddh[:, :LANES])
                o_ref[:, (h // 2) * LANES:(h // 2 + 1) * LANES] = (num / den).astype(o_ref.dtype)


def _attn_prompt_kernel(tq, counts, sink_ref, q_ref, kp_ref, kc_ref, vp_ref, vc_ref, *refs):
    n = len(counts)
    riders, o_ref, cast = refs[:n], refs[n], refs[n + 1:]
    t = pl.program_id(1)
    _rider_cast(pl.program_id(0) * pl.num_programs(1) + t, counts, riders, cast)
    kwin = jnp.concatenate([kp_ref[...], kc_ref[...]], axis=0)
    vwin = jnp.concatenate([vp_ref[...], vc_ref[...]], axis=0)
    row = lax.broadcasted_iota(jnp.int32, (tq, tq), 0)
    col = lax.broadcasted_iota(jnp.int32, (tq, tq), 1)
    valid_prev = ((row < CHUNK) | (col >= CHUNK)) & (t > 0)
    valid_cur_top = lax.broadcasted_iota(jnp.int32, (CHUNK, tq), 1) < CHUNK

    def mask(s):
        prev = jnp.where(valid_prev, s[:, :tq], NEG)
        cur_top = jnp.where(valid_cur_top, s[:CHUNK, tq:], NEG)
        cur = jnp.concatenate([cur_top, s[CHUNK:, tq:]], axis=0)
        return jnp.concatenate([prev, cur], axis=1)

    _attend(q_ref, kwin, vwin, mask, sink_ref, o_ref)


def _attn_prompt(sinks, q, k, va, batch, seq, riders, tq=128):
    m = q.shape[0]
    nt = seq // tq
    assert tq == WINDOW
    cur = lambda b, t: (b * nt + t, 0)
    prev = lambda b, t: (b * nt + jnp.maximum(t - 1, 0), 0)
    counts, r_in, r_out, r_shapes = _rider_specs(riders, batch * nt, lambda b, t: b * nt + t)
    return pl.pallas_call(
        functools.partial(_attn_prompt_kernel, tq, counts),
        grid=(batch, nt),
        in_specs=[
            pl.BlockSpec(memory_space=pltpu.SMEM),
            pl.BlockSpec((tq, Q_WIDTH), cur),
            pl.BlockSpec((tq, KV_WIDTH), prev),
            pl.BlockSpec((tq, KV_WIDTH), cur),
            pl.BlockSpec((tq, KV_WIDTH), prev),
            pl.BlockSpec((tq, KV_WIDTH), cur),
            *r_in,
        ],
        out_specs=(pl.BlockSpec((tq, Q_WIDTH), cur), *r_out),
        out_shape=(jax.ShapeDtypeStruct((m, Q_WIDTH), BF16), *r_shapes),
        compiler_params=_params(("arbitrary", "arbitrary")),
        name="attn_prompt",
    )(sinks, q, k, k, va, va, *riders)


def _attn_sample_kernel(sink_ref, q_ref, ck_ref, kn_ref, cv_ref, vn_ref, o_in_ref, o_ref):
    del o_in_ref
    kwin = jnp.concatenate([ck_ref[...], kn_ref[...]], axis=0)
    vwin = jnp.concatenate([cv_ref[...], vn_ref[...]], axis=0)
    _attend(q_ref, kwin, vwin, None, sink_ref, o_ref)


def _attn_sample(sinks, q, k, va, cache_k, cache_v, o, n_prompt_rows, dec_batch, dec_seq):
    first = n_prompt_rows // dec_seq
    new = lambda b: (first + b, 0)
    cached = lambda b: (b, 0, 0)
    cache_len = cache_k.shape[1]
    return pl.pallas_call(
        _attn_sample_kernel,
        grid=(dec_batch,),
        in_specs=[
            pl.BlockSpec(memory_space=pltpu.SMEM),
            pl.BlockSpec((dec_seq, Q_WIDTH), new),
            pl.BlockSpec((None, cache_len, KV_WIDTH), cached),
            pl.BlockSpec((dec_seq, KV_WIDTH), new),
            pl.BlockSpec((None, cache_len, KV_WIDTH), cached),
            pl.BlockSpec((dec_seq, KV_WIDTH), new),
            pl.BlockSpec(memory_space=pl.ANY),
        ],
        out_specs=pl.BlockSpec((dec_seq, Q_WIDTH), new),
        out_shape=jax.ShapeDtypeStruct(o.shape, o.dtype),
        input_output_aliases={6: 0},
        compiler_params=_params(("arbitrary",)),
        name="attn_sample",
    )(sinks, q, cache_k, k, cache_v, va, o)


def _branch_kernel(a_ref, o_ref, wa_ref, wb_ref, ga_ref, gb_ref, t_ref):
    ya = jnp.dot(a_ref[...], wa_ref[...], preferred_element_type=F32)
    yb = jnp.dot(o_ref[...], wb_ref[...], preferred_element_type=F32)
    t = ga_ref[...].astype(F32) * ya + gb_ref[...].astype(F32) * yb
    t_ref[...] = t.astype(t_ref.dtype)


def _branch(a, o, wa, wb, g, tm=1536, tn=512):
    m = a.shape[0]
    nj = D_MODEL // tn
    return pl.pallas_call(
        _branch_kernel,
        grid=(m // tm, nj),
        in_specs=[
            pl.BlockSpec((tm, D_MODEL), lambda i, j: (i, 0)),
            pl.BlockSpec((tm, Q_WIDTH), lambda i, j: (i, 0)),
            pl.BlockSpec((D_MODEL, tn), lambda i, j: (0, j)),
            pl.BlockSpec((Q_WIDTH, tn), lambda i, j: (0, j)),
            pl.BlockSpec((tm, tn), lambda i, j: (i, j)),
            pl.BlockSpec((tm, tn), lambda i, j: (i, j + nj)),
        ],
        out_specs=pl.BlockSpec((tm, tn), lambda i, j: (i, j)),
        out_shape=jax.ShapeDtypeStruct((m, D_MODEL), BF16),
        compiler_params=_params(("arbitrary", "arbitrary")),
        name="branch",
    )(a, o, wa, wb, g, g)


def _out_kernel(n_prompt_blocks, t_ref, w_ref, xp_ref, xs_ref, x1_ref):
    i = pl.program_id(0)
    y = jnp.dot(t_ref[...], w_ref[...], preferred_element_type=F32)

    @pl.when(i < n_prompt_blocks)
    def _():
        x1_ref[...] = xp_ref[...] + y

    @pl.when(i >= n_prompt_blocks)
    def _():
        x1_ref[...] = xs_ref[...] + y


def _out(t, w_out, xp, xs, tm=512, tn=D_MODEL):
    m = t.shape[0]
    npb = xp.shape[0] // tm
    return pl.pallas_call(
        functools.partial(_out_kernel, npb),
        grid=(m // tm, D_MODEL // tn),
        in_specs=[
            pl.BlockSpec((tm, D_MODEL), lambda i, j: (i, 0)),
            pl.BlockSpec((D_MODEL, tn), lambda i, j: (0, j)),
            pl.BlockSpec((tm, tn), lambda i, j: (jnp.minimum(i, npb - 1), j)),
            pl.BlockSpec((tm, tn), lambda i, j: (jnp.maximum(i - npb, 0), j)),
        ],
        out_specs=pl.BlockSpec((tm, tn), lambda i, j: (i, j)),
        out_shape=jax.ShapeDtypeStruct((m, D_MODEL), F32),
        compiler_params=_params(("arbitrary", "arbitrary")),
        name="out_proj",
    )(t, w_out, xp, xs)


def _ffn_kernel(x1_ref, ng_ref, wg_ref, wu_ref, wd_ref, fg_ref, y_ref, h_ref):
    f = pl.program_id(1)

    @pl.when(f == 0)
    def _():
        x1 = x1_ref[...]
        h_ref[...] = _rms(x1, ng_ref[...]).astype(h_ref.dtype)
        y_ref[...] = x1

    h = h_ref[...]
    gate = jnp.dot(h, wg_ref[...], preferred_element_type=F32)
    up = jnp.dot(h, wu_ref[...], preferred_element_type=F32)
    hid = (jax.nn.silu(gate) * up).astype(BF16)
    y_ref[...] += jnp.dot(hid, wd_ref[...], preferred_element_type=F32)

    @pl.when(f == pl.num_programs(1) - 1)
    def _():
        y_ref[...] = _rms(y_ref[...], fg_ref[...])


def _ffn(x1, row_block_offset, n_row_blocks, norm_g, wg, wu, wd, final_g, tm=512, tf=512):
    d_ff = wg.shape[1]
    return pl.pallas_call(
        _ffn_kernel,
        grid=(n_row_blocks, d_ff // tf),
        in_specs=[
            pl.BlockSpec((tm, D_MODEL), lambda i, f: (i + row_block_offset, 0)),
            pl.BlockSpec((1, D_MODEL), lambda i, f: (0, 0)),
            pl.BlockSpec((D_MODEL, tf), lambda i, f: (0, f)),
            pl.BlockSpec((D_MODEL, tf), lambda i, f: (0, f)),
            pl.BlockSpec((tf, D_MODEL), lambda i, f: (f, 0)),
            pl.BlockSpec((1, D_MODEL), lambda i, f: (0, 0)),
        ],
        out_specs=pl.BlockSpec((tm, D_MODEL), lambda i, f: (i, 0)),
        out_shape=jax.ShapeDtypeStruct((n_row_blocks * tm, D_MODEL), F32),
        scratch_shapes=[pltpu.VMEM((tm, D_MODEL), BF16)],
        compiler_params=_params(("arbitrary", "arbitrary")),
        name="ffn",
    )(x1, norm_g, wg, wu, wd, final_g)


def _rope_tables(seq, past_len, dec_batch, dec_seq):
    pos = jnp.concatenate([jnp.arange(seq, dtype=F32),
                           jnp.tile(past_len + jnp.arange(dec_seq, dtype=F32), dec_batch)])
    inv_freq = jnp.float32(ROPE_THETA) ** (-(jnp.arange(ROT_HALF, dtype=F32) * 2.0 / ROT_DIM))
    ang = pos[:, None] * inv_freq[None, :]
    cos, sin = jnp.cos(ang), jnp.sin(ang)
    n = pos.shape[0]
    pad = jnp.zeros((n, HEAD_DIM - ROT_DIM), F32)
    zero = jnp.zeros((n, ROT_HALF), F32)
    cos_h = jnp.concatenate([cos, cos, pad + 1.0], axis=1)
    sa_h = jnp.concatenate([zero, sin, pad], axis=1)
    sb_h = jnp.concatenate([-sin, zero, pad], axis=1)
    rep = LANES // HEAD_DIM
    return jnp.tile(cos_h, (1, rep)), jnp.tile(sa_h, (1, rep)), jnp.tile(sb_h, (1, rep))


def kernel(x_prompt, x_sample, cache_swa_k, cache_swa_v, norm_mix_g, w_in, gmlp_ln_g, gmlp_ln_b,
           gmlp_ws, gmlp_bs, attn_sinks, w_gate, b_gate, w_branch_a, w_branch_b, w_out,
           norm_ffn_g, w_ffn_gate, w_ffn_up, w_ffn_down, final_norm_g):
    batch, seq, _ = x_prompt.shape
    dec_batch, dec_seq, _ = x_sample.shape
    depth = w_in.shape[0]
    past_len = 2048
    assert depth == 1 and cache_swa_k.shape[2] == WINDOW and dec_seq == CHUNK
    mp, ms = batch * seq, dec_batch * dec_seq

    xp = x_prompt.reshape(mp, D_MODEL)
    xs = x_sample.reshape(ms, D_MODEL)
    cos_t, sa_t, sb_t = _rope_tables(seq, past_len, dec_batch, dec_seq)
    row = lambda p: p.reshape(1, -1)

    l = 0
    xn = _norm(xp, xs, row(norm_mix_g[l]))
    g, w_in_b = _proj_gate(xn, w_gate[l], row(b_gate[l]), [w_in[l]])
    q, k, va = _proj_qkv(xn, w_in_b, cos_t, sa_t, sb_t, mp, seq)
    uv, wg, wu = _proj_uv(xn, w_in_b, [w_ffn_gate[l], w_ffn_up[l]])
    a, vn_s = _gmlp(uv, row(gmlp_ln_g[l]), row(gmlp_ln_b[l]), gmlp_ws[l], gmlp_bs[l].T, mp)
    o, wa_b, wb_b, wo_b, wd = _attn_prompt(
        attn_sinks[l], q, k, va, batch, seq,
        [w_branch_a[l], w_branch_b[l], w_out[l], w_ffn_down[l]])
    o = _attn_sample(attn_sinks[l], q, k, va,
                     cache_swa_k[l].reshape(dec_batch, WINDOW, KV_WIDTH),
                     cache_swa_v[l].reshape(dec_batch, WINDOW, KV_WIDTH),
                     o, mp, dec_batch, dec_seq)
    t = _branch(a, o, wa_b, wb_b, g)
    x1 = _out(t, wo_b, xp, xs)
    ffn_tm = 1024
    ffn = functools.partial(_ffn, norm_g=row(norm_ffn_g[l]), wg=wg, wu=wu, wd=wd,
                            final_g=row(final_norm_g), tm=ffn_tm)
    y_prompt = ffn(x1, 0, mp // ffn_tm)
    y_sample = ffn(x1, mp // ffn_tm, ms // ffn_tm)

    keep = min(WINDOW, seq)
    tail = lambda z: jnp.stack([z[(b + 1) * seq - keep:(b + 1) * seq] for b in range(batch)]).reshape(
        batch, keep, N_KV_HEADS, HEAD_DIM)
    kp, vp = tail(k), tail(va)
    return (
        y_prompt.reshape(batch, seq, D_MODEL),
        y_sample.reshape(dec_batch, dec_seq, D_MODEL),
        kp[None],
        vp[None],
        k[mp:].reshape(1, dec_batch, dec_seq, N_KV_HEADS, HEAD_DIM),
        va[mp:].reshape(1, dec_batch, dec_seq, N_KV_HEADS, HEAD_DIM),
        vn_s.reshape(1, dec_batch, dec_seq, D_MODEL),
    )
```

```python
import functools

import jax
import jax.numpy as jnp
from jax import lax
from jax.experimental import pallas as pl
from jax.experimental.pallas import tpu as pltpu

D_MODEL = 2048
CHUNK = 64
GMLP_CHUNK = 128
GMLP_GROUPS = 8
GMLP_GROUP_DIM = D_MODEL // GMLP_GROUPS
N_HEADS = 32
N_KV_HEADS = 4
HEAD_DIM = 64
Q_REP = N_HEADS // N_KV_HEADS
WINDOW = 128
PAST_LEN = 2048
ROPE_THETA = 500000.0
ROT_DIM = HEAD_DIM // 4
ROT_HALF = ROT_DIM // 2
Q_WIDTH = N_HEADS * HEAD_DIM
KV_WIDTH = N_KV_HEADS * HEAD_DIM
EPS = 1e-6
NEG = -1e30
LANES = 128
LOG2E = 1.4426950408889634
Q_SCALE = HEAD_DIM ** -0.5 * LOG2E

F32 = jnp.float32
BF16 = jnp.bfloat16

VMEM_LIMIT = 56 * 1024 * 1024


def _params(semantics):
    return pltpu.CompilerParams(dimension_semantics=semantics, vmem_limit_bytes=VMEM_LIMIT)


def _rms(x, g):
    return x * lax.rsqrt(jnp.mean(x * x, axis=-1, keepdims=True) + EPS) * g


BF16_SUBLANES = 16


def _rider_specs(weights, n_steps, step_of):
    counts, in_specs, out_specs, out_shapes = [], [], [], []
    for w in weights:
        rows, cols = w.shape
        nb = max(n for n in range(1, n_steps + 1)
                 if rows % n == 0 and (rows // n) % BF16_SUBLANES == 0)
        idx = lambda *g, nb=nb: (jnp.minimum(step_of(*g), nb - 1), 0)
        counts.append(nb)
        in_specs.append(pl.BlockSpec((rows // nb, cols), idx))
        out_specs.append(pl.BlockSpec((rows // nb, cols), idx))
        out_shapes.append(jax.ShapeDtypeStruct(w.shape, BF16))
    return counts, in_specs, out_specs, out_shapes


def _rider_cast(step, counts, src_refs, dst_refs):
    for nb, src, dst in zip(counts, src_refs, dst_refs):
        @pl.when(step < nb)
        def _():
            dst[...] = src[...].astype(dst.dtype)


def _norm_kernel(n_prompt_blocks, xp_ref, xs_ref, g_ref, o_ref):
    i = pl.program_id(0)

    @pl.when(i < n_prompt_blocks)
    def _():
        o_ref[...] = _rms(xp_ref[...], g_ref[...]).astype(BF16)

    @pl.when(i >= n_prompt_blocks)
    def _():
        o_ref[...] = _rms(xs_ref[...], g_ref[...]).astype(BF16)


def _norm(xp, xs, g, tr=512):
    mp, ms = xp.shape[0], xs.shape[0]
    npb, nsb = mp // tr, ms // tr
    return pl.pallas_call(
        functools.partial(_norm_kernel, npb),
        grid=(npb + nsb,),
        in_specs=[
            pl.BlockSpec((tr, D_MODEL), lambda i: (jnp.minimum(i, npb - 1), 0)),
            pl.BlockSpec((tr, D_MODEL), lambda i: (jnp.maximum(i - npb, 0), 0)),
            pl.BlockSpec((1, D_MODEL), lambda i: (0, 0)),
        ],
        out_specs=pl.BlockSpec((tr, D_MODEL), lambda i: (i, 0)),
        out_shape=jax.ShapeDtypeStruct((mp + ms, D_MODEL), BF16),
        compiler_params=_params(("arbitrary",)),
        name="norm",
    )(xp, xs, g)


def _rope(h, cos, sa, sb):
    return h * cos + pltpu.roll(h, ROT_HALF, 1) * sa + pltpu.roll(h, LANES - ROT_HALF, 1) * sb


def _proj_uv_kernel(counts, xn_ref, w_ref, *refs):
    n = len(counts)
    riders, uv_ref, cast = refs[:n], refs[n], refs[n + 1:]
    _rider_cast(pl.program_id(0) * pl.num_programs(1) + pl.program_id(1), counts, riders, cast)
    h = jnp.dot(xn_ref[...], w_ref[...], preferred_element_type=F32)
    uv_ref[...] = jax.nn.gelu(h).astype(uv_ref.dtype)


def _proj_uv(xn, w_in, riders, tm=1024, tn=1024):
    m = xn.shape[0]
    nj = (2 * D_MODEL) // tn
    counts, r_in, r_out, r_shapes = _rider_specs(riders, (m // tm) * nj, lambda i, j: i * nj + j)
    return pl.pallas_call(
        functools.partial(_proj_uv_kernel, counts),
        grid=(m // tm, nj),
        in_specs=[
            pl.BlockSpec((tm, D_MODEL), lambda i, j: (i, 0)),
            pl.BlockSpec((D_MODEL, tn), lambda i, j: (0, j)),
            *r_in,
        ],
        out_specs=(pl.BlockSpec((tm, tn), lambda i, j: (i, j)), *r_out),
        out_shape=(jax.ShapeDtypeStruct((m, 2 * D_MODEL), BF16), *r_shapes),
        compiler_params=_params(("arbitrary", "arbitrary")),
        name="proj_uv",
    )(xn, w_in, *riders)


def _proj_qkv_kernel(xn_ref, wq_ref, wkv_ref, cos_ref, sa_ref, sb_ref, q_ref, k_ref, va_ref):
    j = pl.program_id(1)
    xn = xn_ref[...]

    def rope_cols(h, scale):
        cos, sa, sb = cos_ref[...], sa_ref[...], sb_ref[...]
        parts = [_rope(h[:, c:c + LANES], cos, sa, sb) * scale for c in range(0, h.shape[1], LANES)]
        return jnp.concatenate(parts, axis=1)

    @pl.when(j == 0)
    def _():
        h = jnp.dot(xn, wkv_ref[...], preferred_element_type=F32)
        k_ref[...] = rope_cols(h[:, :KV_WIDTH], 1.0)
        va_ref[...] = h[:, KV_WIDTH:]

    @pl.when(j > 0)
    def _():
        h = jnp.dot(xn, wq_ref[...], preferred_element_type=F32)
        q_ref[...] = rope_cols(h, Q_SCALE).astype(q_ref.dtype)


def _proj_qkv(xn, w_in, cos_t, sa_t, sb_t, n_prompt_rows, seq, tm=1024, tn=Q_WIDTH):
    m = xn.shape[0]
    nq = Q_WIDTH // tn
    q_block0 = (2 * D_MODEL) // tn
    kv_block = (2 * D_MODEL + Q_WIDTH) // (2 * KV_WIDTH)
    npb = n_prompt_rows // tm
    blocks_per_seq = seq // tm

    def tab_idx(i, j):
        return (jnp.where(i < npb, i % blocks_per_seq, blocks_per_seq), 0)

    tab_spec = pl.BlockSpec((tm, LANES), tab_idx)
    qcol = lambda j: jnp.maximum(j - 1, 0)
    return pl.pallas_call(
        _proj_qkv_kernel,
        grid=(m // tm, nq + 1),
        in_specs=[
            pl.BlockSpec((tm, D_MODEL), lambda i, j: (i, 0)),
            pl.BlockSpec((D_MODEL, tn), lambda i, j: (0, q_block0 + qcol(j))),
            pl.BlockSpec((D_MODEL, 2 * KV_WIDTH), lambda i, j: (0, kv_block)),
            tab_spec, tab_spec, tab_spec,
        ],
        out_specs=(
            pl.BlockSpec((tm, tn), lambda i, j: (i, qcol(j))),
            pl.BlockSpec((tm, KV_WIDTH), lambda i, j: (i, 0)),
            pl.BlockSpec((tm, KV_WIDTH), lambda i, j: (i, 0)),
        ),
        out_shape=(
            jax.ShapeDtypeStruct((m, Q_WIDTH), BF16),
            jax.ShapeDtypeStruct((m, KV_WIDTH), F32),
            jax.ShapeDtypeStruct((m, KV_WIDTH), F32),
        ),
        compiler_params=_params(("arbitrary", "arbitrary")),
        name="proj_qkv",
    )(xn, w_in, w_in, cos_t, sa_t, sb_t)


def _proj_gate_kernel(counts, xn_ref, w_ref, b_ref, *refs):
    n = len(counts)
    riders, g_ref, cast, wb_ref = refs[:n], refs[n], refs[n + 1:2 * n + 1], refs[2 * n + 1]
    i = pl.program_id(1)

    @pl.when(i == 0)
    def _():
        wb_ref[...] = w_ref[...].astype(wb_ref.dtype)

    h = jnp.dot(xn_ref[...], wb_ref[...], preferred_element_type=F32)
    g_ref[...] = jax.nn.sigmoid(h + b_ref[...]).astype(g_ref.dtype)
    _rider_cast(pl.program_id(0) * pl.num_programs(1) + i, counts, riders, cast)


def _proj_gate(xn, w_gate, b_gate, riders, tm=1024, tn=1024):
    m = xn.shape[0]
    n = w_gate.shape[1]
    ni = m // tm
    counts, r_in, r_out, r_shapes = _rider_specs(riders, (n // tn) * ni, lambda j, i: j * ni + i)
    return pl.pallas_call(
        functools.partial(_proj_gate_kernel, counts),
        grid=(n // tn, ni),
        in_specs=[
            pl.BlockSpec((tm, D_MODEL), lambda j, i: (i, 0)),
            pl.BlockSpec((D_MODEL, tn), lambda j, i: (0, j)),
            pl.BlockSpec((1, tn), lambda j, i: (0, j)),
            *r_in,
        ],
        out_specs=(pl.BlockSpec((tm, tn), lambda j, i: (i, j)), *r_out),
        out_shape=(jax.ShapeDtypeStruct((m, n), BF16), *r_shapes),
        scratch_shapes=[pltpu.VMEM((D_MODEL, tn), BF16)],
        compiler_params=_params(("arbitrary", "arbitrary")),
        name="proj_gate",
    )(xn, w_gate, b_gate, *riders)


def _gmlp_kernel(n_prompt_blocks, tr, u_ref, v_ref, lng_ref, lnb_ref, ws_ref, bst_ref,
                 a_ref, vn_ref):
    i = pl.program_id(0)
    v = v_ref[...].astype(F32)
    mu = jnp.mean(v, axis=-1, keepdims=True)
    vc = v - mu
    var = jnp.mean(vc * vc, axis=-1, keepdims=True)
    vn = vc * lax.rsqrt(var + EPS) * lng_ref[...] + lnb_ref[...]
    vnb = vn.astype(BF16)

    def mix(length):
        r = lax.broadcasted_iota(jnp.int32, (length, length), 0) // CHUNK
        c = lax.broadcasted_iota(jnp.int32, (length, length), 1) // CHUNK
        for g in range(GMLP_GROUPS):
            w = jnp.where(c <= r, ws_ref[g, :length, :length], 0.0).astype(BF16)
            bias = bst_ref[:length, g:g + 1]
            cols = slice(g * GMLP_GROUP_DIM, (g + 1) * GMLP_GROUP_DIM)
            for b in range(tr // length):
                rows = slice(b * length, (b + 1) * length)
                s = jnp.dot(w, vnb[rows, cols], preferred_element_type=F32) + bias
                a_ref[rows, cols] = (u_ref[rows, cols].astype(F32) * s).astype(a_ref.dtype)

    @pl.when(i < n_prompt_blocks)
    def _():
        mix(GMLP_CHUNK)

    @pl.when(i >= n_prompt_blocks)
    def _():
        vn_ref[...] = vn
        mix(CHUNK)


def _gmlp(uv, ln_g, ln_b, ws, bs_t, n_prompt_rows, tr=512):
    m = uv.shape[0]
    npb = n_prompt_rows // tr
    return pl.pallas_call(
        functools.partial(_gmlp_kernel, npb, tr),
        grid=(m // tr,),
        in_specs=[
            pl.BlockSpec((tr, D_MODEL), lambda i: (i, 0)),
            pl.BlockSpec((tr, D_MODEL), lambda i: (i, 1)),
            pl.BlockSpec((1, D_MODEL), lambda i: (0, 0)),
            pl.BlockSpec((1, D_MODEL), lambda i: (0, 0)),
            pl.BlockSpec((GMLP_GROUPS, GMLP_CHUNK, GMLP_CHUNK), lambda i: (0, 0, 0)),
            pl.BlockSpec((GMLP_CHUNK, GMLP_GROUPS), lambda i: (0, 0)),
        ],
        out_specs=(
            pl.BlockSpec((tr, D_MODEL), lambda i: (i, 0)),
            pl.BlockSpec((tr, D_MODEL), lambda i: (jnp.maximum(i - npb, 0), 0)),
        ),
        out_shape=(
            jax.ShapeDtypeStruct((m, D_MODEL), BF16),
            jax.ShapeDtypeStruct((m - n_prompt_rows, D_MODEL), F32),
        ),
        compiler_params=_params(("arbitrary",)),
        name="gmlp",
    )(uv, uv, ln_g, ln_b, ws, bs_t)


def _dup_head(pair, pair_swapped, low_half, odd):
    if odd:
        return jnp.where(low_half, pair_swapped, pair).astype(BF16)
    return jnp.where(low_half, pair, pair_swapped).astype(BF16)


def _attend(q_ref, kwin, vwin, mask, sink_ref, o_ref):
    rows, keys = q_ref.shape[0], kwin.shape[0]
    ones = jnp.ones((keys, LANES), BF16)
    low_q = lax.broadcasted_iota(jnp.int32, (rows, LANES), 1) < HEAD_DIM
    low_k = lax.broadcasted_iota(jnp.int32, (keys, LANES), 1) < HEAD_DIM
    heads_per_pair = LANES // HEAD_DIM
    for c in range(N_KV_HEADS // heads_per_pair):
        kpair = kwin[:, c * LANES:(c + 1) * LANES]
        vpair = vwin[:, c * LANES:(c + 1) * LANES]
        kswap = pltpu.roll(kpair, HEAD_DIM, 1)
        vswap = pltpu.roll(vpair, HEAD_DIM, 1)
        for odd in range(heads_per_pair):
            hk = c * heads_per_pair + odd
            kdup = _dup_head(kpair, kswap, low_k, odd)
            vaug = jnp.concatenate([_dup_head(vpair, vswap, low_k, odd), ones], axis=1)
            parts = []
            for r in range(Q_REP):
                h = hk * Q_REP + r
                qpair = q_ref[:, (h // 2) * LANES:(h // 2 + 1) * LANES]
                keep = low_q if h % 2 == 0 else jnp.logical_not(low_q)
                parts.append(jnp.where(keep, qpair, jnp.zeros_like(qpair)))
            s_all = lax.dot_general(jnp.concatenate(parts, axis=0), kdup, (((1,), (1,)), ((), ())),
                                    preferred_element_type=F32)
            ps, sinks, mxs = [], [], []
            for r in range(Q_REP):
                s = s_all[r * rows:(r + 1) * rows]
                if mask is not None:
                    s = mask(s)
                sink = sink_ref[hk * Q_REP + r] * LOG2E
                mx = jnp.maximum(jnp.max(s, axis=-1, keepdims=True), sink)
                ps.append(jnp.exp2(s - mx).astype(BF16))
                sinks.append(sink)
                mxs.append(mx)
            o_all = jnp.dot(jnp.concatenate(ps, axis=0), vaug, preferred_element_type=F32)
            for r in range(0, Q_REP, 2):
                h = hk * Q_REP + r
                even, oddh = o_all[r * rows:(r + 1) * rows], o_all[(r + 1) * rows:(r + 2) * rows]
                esink = jnp.exp2(jnp.where(low_q, sinks[r], sinks[r + 1])
                                 - jnp.where(low_q, mxs[r], mxs[r + 1]))
                den = jnp.where(low_q, even[:, LANES:], oddh[:, LANES:]) + esink
                num = jnp.where(low_q, even[:, :LANES], oddh[:, :LANES])
                o_ref[:, (h // 2) * LANES:(h // 2 + 1) * LANES] = (num / den).astype(o_ref.dtype)


def _attn_prompt_kernel(tq, counts, sink_ref, q_ref, kp_ref, kc_ref, vp_ref, vc_ref, *refs):
    n = len(counts)
    riders, o_ref, cast = refs[:n], refs[n], refs[n + 1:]
    t = pl.program_id(1)
    _rider_cast(pl.program_id(0) * pl.num_programs(1) + t, counts, riders, cast)
    kwin = jnp.concatenate([kp_ref[...], kc_ref[...]], axis=0)
    vwin = jnp.concatenate([vp_ref[...], vc_ref[...]], axis=0)
    row = lax.broadcasted_iota(jnp.int32, (tq, tq), 0)
    col = lax.broadcasted_iota(jnp.int32, (tq, tq), 1)
    valid_prev = ((row < CHUNK) | (col >= CHUNK)) & (t > 0)
    valid_cur_top = lax.broadcasted_iota(jnp.int32, (CHUNK, tq), 1) < CHUNK

    def mask(s):
        prev = jnp.where(valid_prev, s[:, :tq], NEG)
        cur_top = jnp.where(valid_cur_top, s[:CHUNK, tq:], NEG)
        cur = jnp.concatenate([cur_top, s[CHUNK:, tq:]], axis=0)
        return jnp.concatenate([prev, cur], axis=1)

    _attend(q_ref, kwin, vwin, mask, sink_ref, o_ref)


def _attn_prompt(sinks, q, k, va, batch, seq, riders, tq=128):
    m = q.shape[0]
    nt = seq // tq
    assert tq == WINDOW
    cur = lambda b, t: (b * nt + t, 0)
    prev = lambda b, t: (b * nt + jnp.maximum(t - 1, 0), 0)
    counts, r_in, r_out, r_shapes = _rider_specs(riders, batch * nt, lambda b, t: b * nt + t)
    return pl.pallas_call(
        functools.partial(_attn_prompt_kernel, tq, counts),
        grid=(batch, nt),
        in_specs=[
            pl.BlockSpec(memory_space=pltpu.SMEM),
            pl.BlockSpec((tq, Q_WIDTH), cur),
            pl.BlockSpec((tq, KV_WIDTH), prev),
            pl.BlockSpec((tq, KV_WIDTH), cur),
            pl.BlockSpec((tq, KV_WIDTH), prev),
            pl.BlockSpec((tq, KV_WIDTH), cur),
            *r_in,
        ],
        out_specs=(pl.BlockSpec((tq, Q_WIDTH), cur), *r_out),
        out_shape=(jax.ShapeDtypeStruct((m, Q_WIDTH), BF16), *r_shapes),
        compiler_params=_params(("arbitrary", "arbitrary")),
        name="attn_prompt",
    )(sinks, q, k, k, va, va, *riders)


def _attn_sample_kernel(sink_ref, q_ref, ck_ref, kn_ref, cv_ref, vn_ref, o_in_ref, o_ref):
    del o_in_ref
    kwin = jnp.concatenate([ck_ref[...], kn_ref[...]], axis=0)
    vwin = jnp.concatenate([cv_ref[...], vn_ref[...]], axis=0)
    _attend(q_ref, kwin, vwin, None, sink_ref, o_ref)


def _attn_sample(sinks, q, k, va, cache_k, cache_v, o, n_prompt_rows, dec_batch, dec_seq):
    first = n_prompt_rows // dec_seq
    new = lambda b: (first + b, 0)
    cached = lambda b: (b, 0, 0)
    cache_len = cache_k.shape[1]
    return pl.pallas_call(
        _attn_sample_kernel,
        grid=(dec_batch,),
        in_specs=[
            pl.BlockSpec(memory_space=pltpu.SMEM),
            pl.BlockSpec((dec_seq, Q_WIDTH), new),
            pl.BlockSpec((None, cache_len, KV_WIDTH), cached),
            pl.BlockSpec((dec_seq, KV_WIDTH), new),
            pl.BlockSpec((None, cache_len, KV_WIDTH), cached),
            pl.BlockSpec((dec_seq, KV_WIDTH), new),
            pl.BlockSpec(memory_space=pl.ANY),
        ],
        out_specs=pl.BlockSpec((dec_seq, Q_WIDTH), new),
        out_shape=jax.ShapeDtypeStruct(o.shape, o.dtype),
        input_output_aliases={6: 0},
        compiler_params=_params(("arbitrary",)),
        name="attn_sample",
    )(sinks, q, cache_k, k, cache_v, va, o)


def _branch_kernel(a_ref, o_ref, wa_ref, wb_ref, ga_ref, gb_ref, t_ref):
    ya = jnp.dot(a_ref[...], wa_ref[...], preferred_element_type=F32)
    yb = jnp.dot(o_ref[...], wb_ref[...], preferred_element_type=F32)
    t = ga_ref[...].astype(F32) * ya + gb_ref[...].astype(F32) * yb
    t_ref[...] = t.astype(t_ref.dtype)


def _branch(a, o, wa, wb, g, tm=1536, tn=512):
    m = a.shape[0]
    nj = D_MODEL // tn
    return pl.pallas_call(
        _branch_kernel,
        grid=(m // tm, nj),
        in_specs=[
            pl.BlockSpec((tm, D_MODEL), lambda i, j: (i, 0)),
            pl.BlockSpec((tm, Q_WIDTH), lambda i, j: (i, 0)),
            pl.BlockSpec((D_MODEL, tn), lambda i, j: (0, j)),
            pl.BlockSpec((Q_WIDTH, tn), lambda i, j: (0, j)),
            pl.BlockSpec((tm, tn), lambda i, j: (i, j)),
            pl.BlockSpec((tm, tn), lambda i, j: (i, j + nj)),
        ],
        out_specs=pl.BlockSpec((tm, tn), lambda i, j: (i, j)),
        out_shape=jax.ShapeDtypeStruct((m, D_MODEL), BF16),
        compiler_params=_params(("arbitrary", "arbitrary")),
        name="branch",
    )(a, o, wa, wb, g, g)


def _out_kernel(n_prompt_blocks, t_ref, w_ref, xp_ref, xs_ref, x1_ref):
    i = pl.program_id(0)
    y = jnp.dot(t_ref[...], w_ref[...], preferred_element_type=F32)

    @pl.when(i < n_prompt_blocks)
    def _():
        x1_ref[...] = xp_ref[...] + y

    @pl.when(i >= n_prompt_blocks)
    def _():
        x1_ref[...] = xs_ref[...] + y


def _out(t, w_out, xp, xs, tm=512, tn=D_MODEL):
    m = t.shape[0]
    npb = xp.shape[0] // tm
    return pl.pallas_call(
        functools.partial(_out_kernel, npb),
        grid=(m // tm, D_MODEL // tn),
        in_specs=[
            pl.BlockSpec((tm, D_MODEL), lambda i, j: (i, 0)),
            pl.BlockSpec((D_MODEL, tn), lambda i, j: (0, j)),
            pl.BlockSpec((tm, tn), lambda i, j: (jnp.minimum(i, npb - 1), j)),
            pl.BlockSpec((tm, tn), lambda i, j: (jnp.maximum(i - npb, 0), j)),
        ],
        out_specs=pl.BlockSpec((tm, tn), lambda i, j: (i, j)),
        out_shape=jax.ShapeDtypeStruct((m, D_MODEL), F32),
        compiler_params=_params(("arbitrary", "arbitrary")),
        name="out_proj",
    )(t, w_out, xp, xs)


def _ffn_kernel(x1_ref, ng_ref, wg_ref, wu_ref, wd_ref, fg_ref, y_ref, h_ref):
    f = pl.program_id(1)

    @pl.when(f == 0)
    def _():
        x1 = x1_ref[...]
        h_ref[...] = _rms(x1, ng_ref[...]).astype(h_ref.dtype)
        y_ref[...] = x1

    h = h_ref[...]
    gate = jnp.dot(h, wg_ref[...], preferred_element_type=F32)
    up = jnp.dot(h, wu_ref[...], preferred_element_type=F32)
    hid = (jax.nn.silu(gate) * up).astype(BF16)
    y_ref[...] += jnp.dot(hid, wd_ref[...], preferred_element_type=F32)

    @pl.when(f == pl.num_programs(1) - 1)
    def _():
        y_ref[...] = _rms(y_ref[...], fg_ref[...])


def _ffn(x1, row_block_offset, n_row_blocks, norm_g, wg, wu, wd, final_g, tm=512, tf=512):
    d_ff = wg.shape[1]
    return pl.pallas_call(
        _ffn_kernel,
        grid=(n_row_blocks, d_ff // tf),
        in_specs=[
            pl.BlockSpec((tm, D_MODEL), lambda i, f: (i + row_block_offset, 0)),
            pl.BlockSpec((1, D_MODEL), lambda i, f: (0, 0)),
            pl.BlockSpec((D_MODEL, tf), lambda i, f: (0, f)),
            pl.BlockSpec((D_MODEL, tf), lambda i, f: (0, f)),
            pl.BlockSpec((tf, D_MODEL), lambda i, f: (f, 0)),
            pl.BlockSpec((1, D_MODEL), lambda i, f: (0, 0)),
        ],
        out_specs=pl.BlockSpec((tm, D_MODEL), lambda i, f: (i, 0)),
        out_shape=jax.ShapeDtypeStruct((n_row_blocks * tm, D_MODEL), F32),
        scratch_shapes=[pltpu.VMEM((tm, D_MODEL), BF16)],
        compiler_params=_params(("arbitrary", "arbitrary")),
        name="ffn",
    )(x1, norm_g, wg, wu, wd, final_g)


def _rope_tables(seq, past_len, dec_batch, dec_seq):
    pos = jnp.concatenate([jnp.arange(seq, dtype=F32),
                           jnp.tile(past_len + jnp.arange(dec_seq, dtype=F32), dec_batch)])
    inv_freq = jnp.float32(ROPE_THETA) ** (-(jnp.arange(ROT_HALF, dtype=F32) * 2.0 / ROT_DIM))
    ang = pos[:, None] * inv_freq[None, :]
    cos, sin = jnp.cos(ang), jnp.sin(ang)
    n = pos.shape[0]
    pad = jnp.zeros((n, HEAD_DIM - ROT_DIM), F32)
    zero = jnp.zeros((n, ROT_HALF), F32)
    cos_h = jnp.concatenate([cos, cos, pad + 1.0], axis=1)
    sa_h = jnp.concatenate([zero, sin, pad], axis=1)
    sb_h = jnp.concatenate([-sin, zero, pad], axis=1)
    rep = LANES // HEAD_DIM
    return jnp.tile(cos_h, (1, rep)), jnp.tile(sa_h, (1, rep)), jnp.tile(sb_h, (1, rep))


def kernel(x_prompt, x_sample, cache_swa_k, cache_swa_v, norm_mix_g, w_in, gmlp_ln_g, gmlp_ln_b,
           gmlp_ws, gmlp_bs, attn_sinks, w_gate, b_gate, w_branch_a, w_branch_b, w_out,
           norm_ffn_g, w_ffn_gate, w_ffn_up, w_ffn_down, final_norm_g):
    batch, seq, _ = x_prompt.shape
    dec_batch, dec_seq, _ = x_sample.shape
    depth = w_in.shape[0]
    assert depth == 1 and cache_swa_k.shape[2] == WINDOW and dec_seq == CHUNK
    mp, ms = batch * seq, dec_batch * dec_seq

    xp = x_prompt.reshape(mp, D_MODEL)
    xs = x_sample.reshape(ms, D_MODEL)
    cos_t, sa_t, sb_t = _rope_tables(seq, PAST_LEN, dec_batch, dec_seq)
    row = lambda p: p.reshape(1, -1)

    l = 0
    xn = _norm(xp, xs, row(norm_mix_g[l]))
    g, w_in_b = _proj_gate(xn, w_gate[l], row(b_gate[l]), [w_in[l]])
    q, k, va = _proj_qkv(xn, w_in_b, cos_t, sa_t, sb_t, mp, seq)
    uv, wg, wu = _proj_uv(xn, w_in_b, [w_ffn_gate[l], w_ffn_up[l]])
    a, vn_s = _gmlp(uv, row(gmlp_ln_g[l]), row(gmlp_ln_b[l]), gmlp_ws[l], gmlp_bs[l].T, mp)
    o, wa_b, wb_b, wo_b, wd = _attn_prompt(
        attn_sinks[l], q, k, va, batch, seq,
        [w_branch_a[l], w_branch_b[l], w_out[l], w_ffn_down[l]])
    o = _attn_sample(attn_sinks[l], q, k, va,
                     cache_swa_k[l].reshape(dec_batch, WINDOW, KV_WIDTH),
                     cache_swa_v[l].reshape(dec_batch, WINDOW, KV_WIDTH),
                     o, mp, dec_batch, dec_seq)
    t = _branch(a, o, wa_b, wb_b, g)
    x1 = _out(t, wo_b, xp, xs)
    ffn_tm = 1024
    ffn = functools.partial(_ffn, norm_g=row(norm_ffn_g[l]), wg=wg, wu=wu, wd=wd,
                            final_g=row(final_norm_g), tm=ffn_tm)
    y_prompt = ffn(x1, 0, mp // ffn_tm)
    y_sample = ffn(x1, mp // ffn_tm, ms // ffn_tm)

    keep = min(WINDOW, seq)
    tail = lambda z: jnp.stack([z[(b + 1) * seq - keep:(b + 1) * seq] for b in range(batch)]).reshape(
        batch, keep, N_KV_HEADS, HEAD_DIM)
    kp, vp = tail(k), tail(va)
    return (
        y_prompt.reshape(batch, seq, D_MODEL),
        y_sample.reshape(dec_batch, dec_seq, D_MODEL),
        kp[None],
        vp[None],
        k[mp:].reshape(1, dec_batch, dec_seq, N_KV_HEADS, HEAD_DIM),
        va[mp:].reshape(1, dec_batch, dec_seq, N_KV_HEADS, HEAD_DIM),
        vn_s.reshape(1, dec_batch, dec_seq, D_MODEL),
    )
```

```python
import functools

import jax
import jax.numpy as jnp
from jax import lax
from jax.experimental import pallas as pl
from jax.experimental.pallas import tpu as pltpu

D_MODEL = 2048
CHUNK = 64
GMLP_CHUNK = 128
GMLP_GROUPS = 8
GMLP_GROUP_DIM = D_MODEL // GMLP_GROUPS
N_HEADS = 32
N_KV_HEADS = 4
HEAD_DIM = 64
Q_REP = N_HEADS // N_KV_HEADS
WINDOW = 128
PAST_LEN = 2048
ROPE_THETA = 500000.0
ROT_DIM = HEAD_DIM // 4
ROT_HALF = ROT_DIM // 2
Q_WIDTH = N_HEADS * HEAD_DIM
KV_WIDTH = N_KV_HEADS * HEAD_DIM
EPS = 1e-6
NEG = -1e30
LANES = 128
LOG2E = 1.4426950408889634
Q_SCALE = HEAD_DIM ** -0.5 * LOG2E

F32 = jnp.float32
BF16 = jnp.bfloat16

VMEM_LIMIT = 56 * 1024 * 1024


def _params(semantics):
    return pltpu.CompilerParams(dimension_semantics=semantics, vmem_limit_bytes=VMEM_LIMIT)


def _rms(x, g):
    return x * lax.rsqrt(jnp.mean(x * x, axis=-1, keepdims=True) + EPS) * g


BF16_SUBLANES = 16


def _rider_specs(weights, n_steps, step_of):
    counts, in_specs, out_specs, out_shapes = [], [], [], []
    for w in weights:
        rows, cols = w.shape
        nb = max(n for n in range(1, n_steps + 1)
                 if rows % n == 0 and (rows // n) % BF16_SUBLANES == 0)
        idx = lambda *g, nb=nb: (jnp.minimum(step_of(*g), nb - 1), 0)
        counts.append(nb)
        in_specs.append(pl.BlockSpec((rows // nb, cols), idx))
        out_specs.append(pl.BlockSpec((rows // nb, cols), idx))
        out_shapes.append(jax.ShapeDtypeStruct(w.shape, BF16))
    return counts, in_specs, out_specs, out_shapes


def _rider_cast(step, counts, src_refs, dst_refs):
    for nb, src, dst in zip(counts, src_refs, dst_refs):
        @pl.when(step < nb)
        def _():
            dst[...] = src[...].astype(dst.dtype)


def _norm_kernel(n_prompt_blocks, xp_ref, xs_ref, g_ref, o_ref):
    i = pl.program_id(0)

    @pl.when(i < n_prompt_blocks)
    def _():
        o_ref[...] = _rms(xp_ref[...], g_ref[...]).astype(BF16)

    @pl.when(i >= n_prompt_blocks)
    def _():
        o_ref[...] = _rms(xs_ref[...], g_ref[...]).astype(BF16)


def _norm(xp, xs, g, tr=1024):
    mp, ms = xp.shape[0], xs.shape[0]
    npb, nsb = mp // tr, ms // tr
    return pl.pallas_call(
        functools.partial(_norm_kernel, npb),
        grid=(npb + nsb,),
        in_specs=[
            pl.BlockSpec((tr, D_MODEL), lambda i: (jnp.minimum(i, npb - 1), 0)),
            pl.BlockSpec((tr, D_MODEL), lambda i: (jnp.maximum(i - npb, 0), 0)),
            pl.BlockSpec((1, D_MODEL), lambda i: (0, 0)),
        ],
        out_specs=pl.BlockSpec((tr, D_MODEL), lambda i: (i, 0)),
        out_shape=jax.ShapeDtypeStruct((mp + ms, D_MODEL), BF16),
        compiler_params=_params(("arbitrary",)),
        name="norm",
    )(xp, xs, g)


def _rope(h, cos, sa, sb):
    return h * cos + pltpu.roll(h, ROT_HALF, 1) * sa + pltpu.roll(h, LANES - ROT_HALF, 1) * sb


def _proj_uv_kernel(counts, xn_ref, w_ref, *refs):
    n = len(counts)
    riders, uv_ref, cast = refs[:n], refs[n], refs[n + 1:]
    _rider_cast(pl.program_id(0) * pl.num_programs(1) + pl.program_id(1), counts, riders, cast)
    h = jnp.dot(xn_ref[...], w_ref[...], preferred_element_type=F32)
    uv_ref[...] = jax.nn.gelu(h).astype(uv_ref.dtype)


def _proj_uv(xn, w_in, riders, tm=1024, tn=1024):
    m = xn.shape[0]
    nj = (2 * D_MODEL) // tn
    counts, r_in, r_out, r_shapes = _rider_specs(riders, (m // tm) * nj, lambda i, j: i * nj + j)
    return pl.pallas_call(
        functools.partial(_proj_uv_kernel, counts),
        grid=(m // tm, nj),
        in_specs=[
            pl.BlockSpec((tm, D_MODEL), lambda i, j: (i, 0)),
            pl.BlockSpec((D_MODEL, tn), lambda i, j: (0, j)),
            *r_in,
        ],
        out_specs=(pl.BlockSpec((tm, tn), lambda i, j: (i, j)), *r_out),
        out_shape=(jax.ShapeDtypeStruct((m, 2 * D_MODEL), BF16), *r_shapes),
        compiler_params=_params(("arbitrary", "arbitrary")),
        name="proj_uv",
    )(xn, w_in, *riders)


def _proj_qkv_kernel(xn_ref, wq_ref, wkv_ref, cos_ref, sa_ref, sb_ref, q_ref, k_ref, va_ref):
    j = pl.program_id(1)
    xn = xn_ref[...]

    def rope_cols(h, scale):
        cos, sa, sb = cos_ref[...], sa_ref[...], sb_ref[...]
        parts = [_rope(h[:, c:c + LANES], cos, sa, sb) * scale for c in range(0, h.shape[1], LANES)]
        return jnp.concatenate(parts, axis=1)

    @pl.when(j == 0)
    def _():
        h = jnp.dot(xn, wkv_ref[...], preferred_element_type=F32)
        k_ref[...] = rope_cols(h[:, :KV_WIDTH], 1.0)
        va_ref[...] = h[:, KV_WIDTH:]

    @pl.when(j > 0)
    def _():
        h = jnp.dot(xn, wq_ref[...], preferred_element_type=F32)
        q_ref[...] = rope_cols(h, Q_SCALE).astype(q_ref.dtype)


def _proj_qkv(xn, w_in, cos_t, sa_t, sb_t, n_prompt_rows, seq, tm=1024, tn=Q_WIDTH):
    m = xn.shape[0]
    nq = Q_WIDTH // tn
    q_block0 = (2 * D_MODEL) // tn
    kv_block = (2 * D_MODEL + Q_WIDTH) // (2 * KV_WIDTH)
    npb = n_prompt_rows // tm
    blocks_per_seq = seq // tm

    def tab_idx(i, j):
        return (jnp.where(i < npb, i % blocks_per_seq, blocks_per_seq), 0)

    tab_spec = pl.BlockSpec((tm, LANES), tab_idx)
    qcol = lambda j: jnp.maximum(j - 1, 0)
    return pl.pallas_call(
        _proj_qkv_kernel,
        grid=(m // tm, nq + 1),
        in_specs=[
            pl.BlockSpec((tm, D_MODEL), lambda i, j: (i, 0)),
            pl.BlockSpec((D_MODEL, tn), lambda i, j: (0, q_block0 + qcol(j))),
            pl.BlockSpec((D_MODEL, 2 * KV_WIDTH), lambda i, j: (0, kv_block)),
            tab_spec, tab_spec, tab_spec,
        ],
        out_specs=(
            pl.BlockSpec((tm, tn), lambda i, j: (i, qcol(j))),
            pl.BlockSpec((tm, KV_WIDTH), lambda i, j: (i, 0)),
            pl.BlockSpec((tm, KV_WIDTH), lambda i, j: (i, 0)),
        ),
        out_shape=(
            jax.ShapeDtypeStruct((m, Q_WIDTH), BF16),
            jax.ShapeDtypeStruct((m, KV_WIDTH), F32),
            jax.ShapeDtypeStruct((m, KV_WIDTH), F32),
        ),
        compiler_params=_params(("arbitrary", "arbitrary")),
        name="proj_qkv",
    )(xn, w_in, w_in, cos_t, sa_t, sb_t)


def _proj_gate_kernel(counts, xn_ref, w_ref, b_ref, *refs):
    n = len(counts)
    riders, g_ref, cast = refs[:n], refs[n], refs[n + 1:]
    h = jnp.dot(xn_ref[...], w_ref[...].astype(BF16), preferred_element_type=F32)
    g_ref[...] = jax.nn.sigmoid(h + b_ref[...]).astype(g_ref.dtype)
    _rider_cast(pl.program_id(0) * pl.num_programs(1) + pl.program_id(1), counts, riders, cast)


def _proj_gate(xn, w_gate, b_gate, riders, tm=1536, tn=1024):
    m = xn.shape[0]
    n = w_gate.shape[1]
    nj = n // tn
    counts, r_in, r_out, r_shapes = _rider_specs(riders, (m // tm) * nj, lambda i, j: i * nj + j)
    return pl.pallas_call(
        functools.partial(_proj_gate_kernel, counts),
        grid=(m // tm, nj),
        in_specs=[
            pl.BlockSpec((tm, D_MODEL), lambda i, j: (i, 0)),
            pl.BlockSpec((D_MODEL, tn), lambda i, j: (0, j)),
            pl.BlockSpec((1, tn), lambda i, j: (0, j)),
            *r_in,
        ],
        out_specs=(pl.BlockSpec((tm, tn), lambda i, j: (i, j)), *r_out),
        out_shape=(jax.ShapeDtypeStruct((m, n), BF16), *r_shapes),
        compiler_params=_params(("arbitrary", "arbitrary")),
        name="proj_gate",
    )(xn, w_gate, b_gate, *riders)


def _gmlp_kernel(n_prompt_blocks, tr, u_ref, v_ref, lng_ref, lnb_ref, ws_ref, bst_ref,
                 a_ref, vn_ref):
    i = pl.program_id(0)
    v = v_ref[...].astype(F32)
    mu = jnp.mean(v, axis=-1, keepdims=True)
    vc = v - mu
    var = jnp.mean(vc * vc, axis=-1, keepdims=True)
    vn = vc * lax.rsqrt(var + EPS) * lng_ref[...] + lnb_ref[...]
    vnb = vn.astype(BF16)

    def mix(length):
        r = lax.broadcasted_iota(jnp.int32, (length, length), 0) // CHUNK
        c = lax.broadcasted_iota(jnp.int32, (length, length), 1) // CHUNK
        for g in range(GMLP_GROUPS):
            w = jnp.where(c <= r, ws_ref[g, :length, :length], 0.0).astype(BF16)
            bias = bst_ref[:length, g:g + 1]
            cols = slice(g * GMLP_GROUP_DIM, (g + 1) * GMLP_GROUP_DIM)
            for b in range(tr // length):
                rows = slice(b * length, (b + 1) * length)
                s = jnp.dot(w, vnb[rows, cols], preferred_element_type=F32) + bias
                a_ref[rows, cols] = (u_ref[rows, cols].astype(F32) * s).astype(a_ref.dtype)

    @pl.when(i < n_prompt_blocks)
    def _():
        mix(GMLP_CHUNK)

    @pl.when(i >= n_prompt_blocks)
    def _():
        vn_ref[...] = vn
        mix(CHUNK)


def _gmlp(uv, ln_g, ln_b, ws, bs_t, n_prompt_rows, tr=1024):
    m = uv.shape[0]
    npb = n_prompt_rows // tr
    return pl.pallas_call(
        functools.partial(_gmlp_kernel, npb, tr),
        grid=(m // tr,),
        in_specs=[
            pl.BlockSpec((tr, D_MODEL), lambda i: (i, 0)),
            pl.BlockSpec((tr, D_MODEL), lambda i: (i, 1)),
            pl.BlockSpec((1, D_MODEL), lambda i: (0, 0)),
            pl.BlockSpec((1, D_MODEL), lambda i: (0, 0)),
            pl.BlockSpec((GMLP_GROUPS, GMLP_CHUNK, GMLP_CHUNK), lambda i: (0, 0, 0)),
            pl.BlockSpec((GMLP_CHUNK, GMLP_GROUPS), lambda i: (0, 0)),
        ],
        out_specs=(
            pl.BlockSpec((tr, D_MODEL), lambda i: (i, 0)),
            pl.BlockSpec((tr, D_MODEL), lambda i: (jnp.maximum(i - npb, 0), 0)),
        ),
        out_shape=(
            jax.ShapeDtypeStruct((m, D_MODEL), BF16),
            jax.ShapeDtypeStruct((m - n_prompt_rows, D_MODEL), F32),
        ),
        compiler_params=_params(("arbitrary",)),
        name="gmlp",
    )(uv, uv, ln_g, ln_b, ws, bs_t)


def _dup_head(pair, pair_swapped, low_half, odd):
    if odd:
        return jnp.where(low_half, pair_swapped, pair).astype(BF16)
    return jnp.where(low_half, pair, pair_swapped).astype(BF16)


def _attend(q_ref, kwin, vwin, mask, sink_ref, o_ref):
    rows, keys = q_ref.shape[0], kwin.shape[0]
    ones = jnp.ones((keys, LANES), BF16)
    low_q = lax.broadcasted_iota(jnp.int32, (rows, LANES), 1) < HEAD_DIM
    low_k = lax.broadcasted_iota(jnp.int32, (keys, LANES), 1) < HEAD_DIM
    heads_per_pair = LANES // HEAD_DIM
    for c in range(N_KV_HEADS // heads_per_pair):
        kpair = kwin[:, c * LANES:(c + 1) * LANES]
        vpair = vwin[:, c * LANES:(c + 1) * LANES]
        kswap = pltpu.roll(kpair, HEAD_DIM, 1)
        vswap = pltpu.roll(vpair, HEAD_DIM, 1)
        for odd in range(heads_per_pair):
            hk = c * heads_per_pair + odd
            kdup = _dup_head(kpair, kswap, low_k, odd)
            vaug = jnp.concatenate([_dup_head(vpair, vswap, low_k, odd), ones], axis=1)
            parts = []
            for r in range(Q_REP):
                h = hk * Q_REP + r
                qpair = q_ref[:, (h // 2) * LANES:(h // 2 + 1) * LANES]
                keep = low_q if h % 2 == 0 else jnp.logical_not(low_q)
                parts.append(jnp.where(keep, qpair, jnp.zeros_like(qpair)))
            s_all = lax.dot_general(jnp.concatenate(parts, axis=0), kdup, (((1,), (1,)), ((), ())),
                                    preferred_element_type=F32)
            ps, sinks, mxs = [], [], []
            for r in range(Q_REP):
                s = s_all[r * rows:(r + 1) * rows]
                if mask is not None:
                    s = mask(s)
                sink = sink_ref[hk * Q_REP + r] * LOG2E
                mx = jnp.maximum(jnp.max(s, axis=-1, keepdims=True), sink)
                ps.append(jnp.exp2(s - mx).astype(BF16))
                sinks.append(sink)
                mxs.append(mx)
            o_all = jnp.dot(jnp.concatenate(ps, axis=0), vaug, preferred_element_type=F32)
            for r in range(0, Q_REP, 2):
                h = hk * Q_REP + r
                even, oddh = o_all[r * rows:(r + 1) * rows], o_all[(r + 1) * rows:(r + 2) * rows]
                esink = jnp.exp2(jnp.where(low_q, sinks[r], sinks[r + 1])
                                 - jnp.where(low_q, mxs[r], mxs[r + 1]))
                den = jnp.where(low_q, even[:, LANES:], oddh[:, LANES:]) + esink
                num = jnp.where(low_q, even[:, :LANES], oddh[:, :LANES])
                o_ref[:, (h // 2) * LANES:(h // 2 + 1) * LANES] = (num / den).astype(o_ref.dtype)


def _attn_prompt_kernel(tq, counts, sink_ref, q_ref, kp_ref, kc_ref, vp_ref, vc_ref, *refs):
    n = len(counts)
    riders, o_ref, cast = refs[:n], refs[n], refs[n + 1:]
    t = pl.program_id(1)
    _rider_cast(pl.program_id(0) * pl.num_programs(1) + t, counts, riders, cast)
    kwin = jnp.concatenate([kp_ref[...], kc_ref[...]], axis=0)
    vwin = jnp.concatenate([vp_ref[...], vc_ref[...]], axis=0)
    row = lax.broadcasted_iota(jnp.int32, (tq, tq), 0)
    col = lax.broadcasted_iota(jnp.int32, (tq, tq), 1)
    valid_prev = ((row < CHUNK) | (col >= CHUNK)) & (t > 0)
    valid_cur_top = lax.broadcasted_iota(jnp.int32, (CHUNK, tq), 1) < CHUNK

    def mask(s):
        prev = jnp.where(valid_prev, s[:, :tq], NEG)
        cur_top = jnp.where(valid_cur_top, s[:CHUNK, tq:], NEG)
        cur = jnp.concatenate([cur_top, s[CHUNK:, tq:]], axis=0)
        return jnp.concatenate([prev, cur], axis=1)

    _attend(q_ref, kwin, vwin, mask, sink_ref, o_ref)


def _attn_prompt(sinks, q, k, va, batch, seq, riders, tq=128):
    m = q.shape[0]
    nt = seq // tq
    assert tq == WINDOW
    cur = lambda b, t: (b * nt + t, 0)
    prev = lambda b, t: (b * nt + jnp.maximum(t - 1, 0), 0)
    counts, r_in, r_out, r_shapes = _rider_specs(riders, batch * nt, lambda b, t: b * nt + t)
    return pl.pallas_call(
        functools.partial(_attn_prompt_kernel, tq, counts),
        grid=(batch, nt),
        in_specs=[
            pl.BlockSpec(memory_space=pltpu.SMEM),
            pl.BlockSpec((tq, Q_WIDTH), cur),
            pl.BlockSpec((tq, KV_WIDTH), prev),
            pl.BlockSpec((tq, KV_WIDTH), cur),
            pl.BlockSpec((tq, KV_WIDTH), prev),
            pl.BlockSpec((tq, KV_WIDTH), cur),
            *r_in,
        ],
        out_specs=(pl.BlockSpec((tq, Q_WIDTH), cur), *r_out),
        out_shape=(jax.ShapeDtypeStruct((m, Q_WIDTH), BF16), *r_shapes),
        compiler_params=_params(("arbitrary", "arbitrary")),
        name="attn_prompt",
    )(sinks, q, k, k, va, va, *riders)


def _attn_sample_kernel(per_step, dec_seq, sink_ref, q_ref, ck_ref, kn_ref, cv_ref, vn_ref,
                        o_in_ref, o_ref):
    del o_in_ref
    kn, vn = kn_ref[...], vn_ref[...]
    for b in range(per_step):
        rows = pl.ds(b * dec_seq, dec_seq)
        new = slice(b * dec_seq, (b + 1) * dec_seq)
        kwin = jnp.concatenate([ck_ref[b], kn[new]], axis=0)
        vwin = jnp.concatenate([cv_ref[b], vn[new]], axis=0)
        _attend(q_ref.at[rows], kwin, vwin, None, sink_ref, o_ref.at[rows])


def _attn_sample(sinks, q, k, va, cache_k, cache_v, o, n_prompt_rows, dec_batch, dec_seq, per_step=2):
    rows = per_step * dec_seq
    first = n_prompt_rows // rows
    new = lambda s: (first + s, 0)
    cached = lambda s: (s, 0, 0)
    cache_len = cache_k.shape[1]
    return pl.pallas_call(
        functools.partial(_attn_sample_kernel, per_step, dec_seq),
        grid=(dec_batch // per_step,),
        in_specs=[
            pl.BlockSpec(memory_space=pltpu.SMEM),
            pl.BlockSpec((rows, Q_WIDTH), new),
            pl.BlockSpec((per_step, cache_len, KV_WIDTH), cached),
            pl.BlockSpec((rows, KV_WIDTH), new),
            pl.BlockSpec((per_step, cache_len, KV_WIDTH), cached),
            pl.BlockSpec((rows, KV_WIDTH), new),
            pl.BlockSpec(memory_space=pl.ANY),
        ],
        out_specs=pl.BlockSpec((rows, Q_WIDTH), new),
        out_shape=jax.ShapeDtypeStruct(o.shape, o.dtype),
        input_output_aliases={6: 0},
        compiler_params=_params(("arbitrary",)),
        name="attn_sample",
    )(sinks, q, cache_k, k, cache_v, va, o)


def _branch_kernel(a_ref, o_ref, wa_ref, wb_ref, ga_ref, gb_ref, t_ref):
    ya = jnp.dot(a_ref[...], wa_ref[...], preferred_element_type=F32)
    yb = jnp.dot(o_ref[...], wb_ref[...], preferred_element_type=F32)
    t = ga_ref[...].astype(F32) * ya + gb_ref[...].astype(F32) * yb
    t_ref[...] = t.astype(t_ref.dtype)


def _branch(a, o, wa, wb, g, tm=1536, tn=512):
    m = a.shape[0]
    nj = D_MODEL // tn
    return pl.pallas_call(
        _branch_kernel,
        grid=(m // tm, nj),
        in_specs=[
            pl.BlockSpec((tm, D_MODEL), lambda i, j: (i, 0)),
            pl.BlockSpec((tm, Q_WIDTH), lambda i, j: (i, 0)),
            pl.BlockSpec((D_MODEL, tn), lambda i, j: (0, j)),
            pl.BlockSpec((Q_WIDTH, tn), lambda i, j: (0, j)),
            pl.BlockSpec((tm, tn), lambda i, j: (i, j)),
            pl.BlockSpec((tm, tn), lambda i, j: (i, j + nj)),
        ],
        out_specs=pl.BlockSpec((tm, tn), lambda i, j: (i, j)),
        out_shape=jax.ShapeDtypeStruct((m, D_MODEL), BF16),
        compiler_params=_params(("arbitrary", "arbitrary")),
        name="branch",
    )(a, o, wa, wb, g, g)


def _out_kernel(n_prompt_blocks, t_ref, w_ref, xp_ref, xs_ref, x1_ref):
    i = pl.program_id(0)
    y = jnp.dot(t_ref[...], w_ref[...], preferred_element_type=F32)

    @pl.when(i < n_prompt_blocks)
    def _():
        x1_ref[...] = xp_ref[...] + y

    @pl.when(i >= n_prompt_blocks)
    def _():
        x1_ref[...] = xs_ref[...] + y


def _out(t, w_out, xp, xs, tm=512, tn=D_MODEL):
    m = t.shape[0]
    npb = xp.shape[0] // tm
    return pl.pallas_call(
        functools.partial(_out_kernel, npb),
        grid=(m // tm, D_MODEL // tn),
        in_specs=[
            pl.BlockSpec((tm, D_MODEL), lambda i, j: (i, 0)),
            pl.BlockSpec((D_MODEL, tn), lambda i, j: (0, j)),
            pl.BlockSpec((tm, tn), lambda i, j: (jnp.minimum(i, npb - 1), j)),
            pl.BlockSpec((tm, tn), lambda i, j: (jnp.maximum(i - npb, 0), j)),
        ],
        out_specs=pl.BlockSpec((tm, tn), lambda i, j: (i, j)),
        out_shape=jax.ShapeDtypeStruct((m, D_MODEL), F32),
        compiler_params=_params(("arbitrary", "arbitrary")),
        name="out_proj",
    )(t, w_out, xp, xs)


def _ffn_kernel(x1_ref, ng_ref, wg_ref, wu_ref, wd_ref, fg_ref, y_ref, h_ref):
    f = pl.program_id(1)

    @pl.when(f == 0)
    def _():
        x1 = x1_ref[...]
        h_ref[...] = _rms(x1, ng_ref[...]).astype(h_ref.dtype)
        y_ref[...] = x1

    h = h_ref[...]
    gate = jnp.dot(h, wg_ref[...], preferred_element_type=F32)
    up = jnp.dot(h, wu_ref[...], preferred_element_type=F32)
    hid = (jax.nn.silu(gate) * up).astype(BF16)
    y_ref[...] += jnp.dot(hid, wd_ref[...], preferred_element_type=F32)

    @pl.when(f == pl.num_programs(1) - 1)
    def _():
        y_ref[...] = _rms(y_ref[...], fg_ref[...])


def _ffn(x1, row_block_offset, n_row_blocks, norm_g, wg, wu, wd, final_g, tm=512, tf=512):
    d_ff = wg.shape[1]
    return pl.pallas_call(
        _ffn_kernel,
        grid=(n_row_blocks, d_ff // tf),
        in_specs=[
            pl.BlockSpec((tm, D_MODEL), lambda i, f: (i + row_block_offset, 0)),
            pl.BlockSpec((1, D_MODEL), lambda i, f: (0, 0)),
            pl.BlockSpec((D_MODEL, tf), lambda i, f: (0, f)),
            pl.BlockSpec((D_MODEL, tf), lambda i, f: (0, f)),
            pl.BlockSpec((tf, D_MODEL), lambda i, f: (f, 0)),
            pl.BlockSpec((1, D_MODEL), lambda i, f: (0, 0)),
        ],
        out_specs=pl.BlockSpec((tm, D_MODEL), lambda i, f: (i, 0)),
        out_shape=jax.ShapeDtypeStruct((n_row_blocks * tm, D_MODEL), F32),
        scratch_shapes=[pltpu.VMEM((tm, D_MODEL), BF16)],
        compiler_params=_params(("arbitrary", "arbitrary")),
        name="ffn",
    )(x1, norm_g, wg, wu, wd, final_g)


def _rope_tables(seq, past_len, dec_batch, dec_seq):
    pos = jnp.concatenate([jnp.arange(seq, dtype=F32),
                           jnp.tile(past_len + jnp.arange(dec_seq, dtype=F32), dec_batch)])
    inv_freq = jnp.float32(ROPE_THETA) ** (-(jnp.arange(ROT_HALF, dtype=F32) * 2.0 / ROT_DIM))
    ang = pos[:, None] * inv_freq[None, :]
    cos, sin = jnp.cos(ang), jnp.sin(ang)
    n = pos.shape[0]
    pad = jnp.zeros((n, HEAD_DIM - ROT_DIM), F32)
    zero = jnp.zeros((n, ROT_HALF), F32)
    cos_h = jnp.concatenate([cos, cos, pad + 1.0], axis=1)
    sa_h = jnp.concatenate([zero, sin, pad], axis=1)
    sb_h = jnp.concatenate([-sin, zero, pad], axis=1)
    rep = LANES // HEAD_DIM
    return jnp.tile(cos_h, (1, rep)), jnp.tile(sa_h, (1, rep)), jnp.tile(sb_h, (1, rep))


def kernel(x_prompt, x_sample, cache_swa_k, cache_swa_v, norm_mix_g, w_in, gmlp_ln_g, gmlp_ln_b,
           gmlp_ws, gmlp_bs, attn_sinks, w_gate, b_gate, w_branch_a, w_branch_b, w_out,
           norm_ffn_g, w_ffn_gate, w_ffn_up, w_ffn_down, final_norm_g):
    batch, seq, _ = x_prompt.shape
    dec_batch, dec_seq, _ = x_sample.shape
    depth = w_in.shape[0]
    assert depth == 1 and cache_swa_k.shape[2] == WINDOW and dec_seq == CHUNK
    mp, ms = batch * seq, dec_batch * dec_seq

    xp = x_prompt.reshape(mp, D_MODEL)
    xs = x_sample.reshape(ms, D_MODEL)
    cos_t, sa_t, sb_t = _rope_tables(seq, PAST_LEN, dec_batch, dec_seq)
    row = lambda p: p.reshape(1, -1)

    l = 0
    xn = _norm(xp, xs, row(norm_mix_g[l]))
    g, w_in_b = _proj_gate(xn, w_gate[l], row(b_gate[l]), [w_in[l]])
    q, k, va = _proj_qkv(xn, w_in_b, cos_t, sa_t, sb_t, mp, seq)
    uv, wg, wu = _proj_uv(xn, w_in_b, [w_ffn_gate[l], w_ffn_up[l]])
    a, vn_s = _gmlp(uv, row(gmlp_ln_g[l]), row(gmlp_ln_b[l]), gmlp_ws[l], gmlp_bs[l].T, mp)
    o, wa_b, wb_b, wo_b, wd = _attn_prompt(
        attn_sinks[l], q, k, va, batch, seq,
        [w_branch_a[l], w_branch_b[l], w_out[l], w_ffn_down[l]])
    o = _attn_sample(attn_sinks[l], q, k, va,
                     cache_swa_k[l].reshape(dec_batch, WINDOW, KV_WIDTH),
                     cache_swa_v[l].reshape(dec_batch, WINDOW, KV_WIDTH),
                     o, mp, dec_batch, dec_seq)
    t = _branch(a, o, wa_b, wb_b, g)
    x1 = _out(t, wo_b, xp, xs)
    ffn_tm = 1024
    ffn = functools.partial(_ffn, norm_g=row(norm_ffn_g[l]), wg=wg, wu=wu, wd=wd,
                            final_g=row(final_norm_g), tm=ffn_tm)
    y_prompt = ffn(x1, 0, mp // ffn_tm)
    y_sample = ffn(x1, mp // ffn_tm, ms // ffn_tm)

    keep = min(WINDOW, seq)
    tail = lambda z: jnp.stack([z[(b + 1) * seq - keep:(b + 1) * seq] for b in range(batch)]).reshape(
        batch, keep, N_KV_HEADS, HEAD_DIM)
    kp, vp = tail(k), tail(va)
    return (
        y_prompt.reshape(batch, seq, D_MODEL),
        y_sample.reshape(dec_batch, dec_seq, D_MODEL),
        kp[None],
        vp[None],
        k[mp:].reshape(1, dec_batch, dec_seq, N_KV_HEADS, HEAD_DIM),
        va[mp:].reshape(1, dec_batch, dec_seq, N_KV_HEADS, HEAD_DIM),
        vn_s.reshape(1, dec_batch, dec_seq, D_MODEL),
    )
```

```python
import functools

import jax
import jax.numpy as jnp
from jax import lax
from jax.experimental import pallas as pl
from jax.experimental.pallas import tpu as pltpu

D_MODEL = 2048
CHUNK = 64
GMLP_CHUNK = 128
GMLP_GROUPS = 8
GMLP_GROUP_DIM = D_MODEL // GMLP_GROUPS
N_HEADS = 32
N_KV_HEADS = 4
HEAD_DIM = 64
Q_REP = N_HEADS // N_KV_HEADS
WINDOW = 128
PAST_LEN = 2048
ROPE_THETA = 500000.0
ROT_DIM = HEAD_DIM // 4
ROT_HALF = ROT_DIM // 2
Q_WIDTH = N_HEADS * HEAD_DIM
KV_WIDTH = N_KV_HEADS * HEAD_DIM
EPS = 1e-6
NEG = -1e30
LANES = 128
LOG2E = 1.4426950408889634
Q_SCALE = HEAD_DIM ** -0.5 * LOG2E

F32 = jnp.float32
BF16 = jnp.bfloat16

VMEM_LIMIT = 56 * 1024 * 1024


def _params(semantics):
    return pltpu.CompilerParams(dimension_semantics=semantics, vmem_limit_bytes=VMEM_LIMIT)


def _rms(x, g):
    return x * lax.rsqrt(jnp.mean(x * x, axis=-1, keepdims=True) + EPS) * g


BF16_SUBLANES = 16


def _rider_specs(weights, n_steps, step_of):
    counts, in_specs, out_specs, out_shapes = [], [], [], []
    for w in weights:
        rows, cols = w.shape
        nb = max(n for n in range(1, n_steps + 1)
                 if rows % n == 0 and (rows // n) % BF16_SUBLANES == 0)
        idx = lambda *g, nb=nb: (jnp.minimum(step_of(*g), nb - 1), 0)
        counts.append(nb)
        in_specs.append(pl.BlockSpec((rows // nb, cols), idx))
        out_specs.append(pl.BlockSpec((rows // nb, cols), idx))
        out_shapes.append(jax.ShapeDtypeStruct(w.shape, BF16))
    return counts, in_specs, out_specs, out_shapes


def _rider_cast(step, counts, src_refs, dst_refs):
    for nb, src, dst in zip(counts, src_refs, dst_refs):
        @pl.when(step < nb)
        def _():
            dst[...] = src[...].astype(dst.dtype)


def _norm_kernel(n_prompt_blocks, xp_ref, xs_ref, g_ref, o_ref):
    i = pl.program_id(0)

    @pl.when(i < n_prompt_blocks)
    def _():
        o_ref[...] = _rms(xp_ref[...], g_ref[...]).astype(BF16)

    @pl.when(i >= n_prompt_blocks)
    def _():
        o_ref[...] = _rms(xs_ref[...], g_ref[...]).astype(BF16)


def _norm(xp, xs, g, tr=1024):
    mp, ms = xp.shape[0], xs.shape[0]
    npb, nsb = mp // tr, ms // tr
    return pl.pallas_call(
        functools.partial(_norm_kernel, npb),
        grid=(npb + nsb,),
        in_specs=[
            pl.BlockSpec((tr, D_MODEL), lambda i: (jnp.minimum(i, npb - 1), 0)),
            pl.BlockSpec((tr, D_MODEL), lambda i: (jnp.maximum(i - npb, 0), 0)),
            pl.BlockSpec((1, D_MODEL), lambda i: (0, 0)),
        ],
        out_specs=pl.BlockSpec((tr, D_MODEL), lambda i: (i, 0)),
        out_shape=jax.ShapeDtypeStruct((mp + ms, D_MODEL), BF16),
        compiler_params=_params(("arbitrary",)),
        name="norm",
    )(xp, xs, g)


def _rope(h, cos, sa, sb):
    return h * cos + pltpu.roll(h, ROT_HALF, 1) * sa + pltpu.roll(h, LANES - ROT_HALF, 1) * sb


GELU_C1 = 0.7978845608028654
GELU_C3 = 0.035677408136300125


def _gelu_tanh(x):
    return (0.5 * x) * (1.0 + jnp.tanh(x * (GELU_C1 + GELU_C3 * (x * x))))


def _proj_uv_kernel(counts, xn_ref, w_ref, *refs):
    n = len(counts)
    riders, uv_ref, cast = refs[:n], refs[n], refs[n + 1:]
    _rider_cast(pl.program_id(0) * pl.num_programs(1) + pl.program_id(1), counts, riders, cast)
    h = jnp.dot(xn_ref[...], w_ref[...], preferred_element_type=F32)
    uv_ref[...] = _gelu_tanh(h).astype(uv_ref.dtype)


def _proj_uv(xn, w_in, riders, tm=1024, tn=1024):
    m = xn.shape[0]
    nj = (2 * D_MODEL) // tn
    counts, r_in, r_out, r_shapes = _rider_specs(riders, (m // tm) * nj, lambda i, j: i * nj + j)
    return pl.pallas_call(
        functools.partial(_proj_uv_kernel, counts),
        grid=(m // tm, nj),
        in_specs=[
            pl.BlockSpec((tm, D_MODEL), lambda i, j: (i, 0)),
            pl.BlockSpec((D_MODEL, tn), lambda i, j: (0, j)),
            *r_in,
        ],
        out_specs=(pl.BlockSpec((tm, tn), lambda i, j: (i, j)), *r_out),
        out_shape=(jax.ShapeDtypeStruct((m, 2 * D_MODEL), BF16), *r_shapes),
        compiler_params=_params(("arbitrary", "arbitrary")),
        name="proj_uv",
    )(xn, w_in, *riders)


def _proj_qkv_kernel(xn_ref, wq_ref, wkv_ref, cos_ref, sa_ref, sb_ref, q_ref, k_ref, va_ref):
    j = pl.program_id(1)
    xn = xn_ref[...]

    def rope_cols(h, scale):
        cos, sa, sb = cos_ref[...], sa_ref[...], sb_ref[...]
        parts = [_rope(h[:, c:c + LANES], cos, sa, sb) * scale for c in range(0, h.shape[1], LANES)]
        return jnp.concatenate(parts, axis=1)

    @pl.when(j == 0)
    def _():
        h = jnp.dot(xn, wkv_ref[...], preferred_element_type=F32)
        k_ref[...] = rope_cols(h[:, :KV_WIDTH], 1.0)
        va_ref[...] = h[:, KV_WIDTH:]

    @pl.when(j > 0)
    def _():
        h = jnp.dot(xn, wq_ref[...], preferred_element_type=F32)
        q_ref[...] = rope_cols(h, Q_SCALE).astype(q_ref.dtype)


def _proj_qkv(xn, w_in, cos_t, sa_t, sb_t, n_prompt_rows, seq, tm=1024, tn=Q_WIDTH):
    m = xn.shape[0]
    nq = Q_WIDTH // tn
    q_block0 = (2 * D_MODEL) // tn
    kv_block = (2 * D_MODEL + Q_WIDTH) // (2 * KV_WIDTH)
    npb = n_prompt_rows // tm
    blocks_per_seq = seq // tm

    def tab_idx(i, j):
        return (jnp.where(i < npb, i % blocks_per_seq, blocks_per_seq), 0)

    tab_spec = pl.BlockSpec((tm, LANES), tab_idx)
    qcol = lambda j: jnp.maximum(j - 1, 0)
    return pl.pallas_call(
        _proj_qkv_kernel,
        grid=(m // tm, nq + 1),
        in_specs=[
            pl.BlockSpec((tm, D_MODEL), lambda i, j: (i, 0)),
            pl.BlockSpec((D_MODEL, tn), lambda i, j: (0, q_block0 + qcol(j))),
            pl.BlockSpec((D_MODEL, 2 * KV_WIDTH), lambda i, j: (0, kv_block)),
            tab_spec, tab_spec, tab_spec,
        ],
        out_specs=(
            pl.BlockSpec((tm, tn), lambda i, j: (i, qcol(j))),
            pl.BlockSpec((tm, KV_WIDTH), lambda i, j: (i, 0)),
            pl.BlockSpec((tm, KV_WIDTH), lambda i, j: (i, 0)),
        ),
        out_shape=(
            jax.ShapeDtypeStruct((m, Q_WIDTH), BF16),
            jax.ShapeDtypeStruct((m, KV_WIDTH), F32),
            jax.ShapeDtypeStruct((m, KV_WIDTH), F32),
        ),
        compiler_params=_params(("arbitrary", "arbitrary")),
        name="proj_qkv",
    )(xn, w_in, w_in, cos_t, sa_t, sb_t)


def _proj_gate_kernel(counts, xn_ref, w_ref, b_ref, *refs):
    n = len(counts)
    riders, g_ref, cast = refs[:n], refs[n], refs[n + 1:]
    h = jnp.dot(xn_ref[...], w_ref[...].astype(BF16), preferred_element_type=F32)
    g_ref[...] = jax.nn.sigmoid(h + b_ref[...]).astype(g_ref.dtype)
    _rider_cast(pl.program_id(0) * pl.num_programs(1) + pl.program_id(1), counts, riders, cast)


def _proj_gate(xn, w_gate, b_gate, riders, tm=1536, tn=1024):
    m = xn.shape[0]
    n = w_gate.shape[1]
    nj = n // tn
    counts, r_in, r_out, r_shapes = _rider_specs(riders, (m // tm) * nj, lambda i, j: i * nj + j)
    return pl.pallas_call(
        functools.partial(_proj_gate_kernel, counts),
        grid=(m // tm, nj),
        in_specs=[
            pl.BlockSpec((tm, D_MODEL), lambda i, j: (i, 0)),
            pl.BlockSpec((D_MODEL, tn), lambda i, j: (0, j)),
            pl.BlockSpec((1, tn), lambda i, j: (0, j)),
            *r_in,
        ],
        out_specs=(pl.BlockSpec((tm, tn), lambda i, j: (i, j)), *r_out),
        out_shape=(jax.ShapeDtypeStruct((m, n), BF16), *r_shapes),
        compiler_params=_params(("arbitrary", "arbitrary")),
        name="proj_gate",
    )(xn, w_gate, b_gate, *riders)


def _gmlp_kernel(n_prompt_blocks, tr, u_ref, v_ref, lng_ref, lnb_ref, ws_ref, bst_ref,
                 a_ref, vn_ref):
    i = pl.program_id(0)
    v = v_ref[...].astype(F32)
    mu = jnp.mean(v, axis=-1, keepdims=True)
    vc = v - mu
    var = jnp.mean(vc * vc, axis=-1, keepdims=True)
    vn = vc * lax.rsqrt(var + EPS) * lng_ref[...] + lnb_ref[...]
    vnb = vn.astype(BF16)

    def mix(length):
        r = lax.broadcasted_iota(jnp.int32, (length, length), 0) // CHUNK
        c = lax.broadcasted_iota(jnp.int32, (length, length), 1) // CHUNK
        for g in range(GMLP_GROUPS):
            w = jnp.where(c <= r, ws_ref[g, :length, :length], 0.0).astype(BF16)
            bias = bst_ref[:length, g:g + 1]
            cols = slice(g * GMLP_GROUP_DIM, (g + 1) * GMLP_GROUP_DIM)
            for b in range(tr // length):
                rows = slice(b * length, (b + 1) * length)
                s = jnp.dot(w, vnb[rows, cols], preferred_element_type=F32) + bias
                a_ref[rows, cols] = (u_ref[rows, cols].astype(F32) * s).astype(a_ref.dtype)

    @pl.when(i < n_prompt_blocks)
    def _():
        mix(GMLP_CHUNK)

    @pl.when(i >= n_prompt_blocks)
    def _():
        vn_ref[...] = vn
        mix(CHUNK)


def _gmlp(uv, ln_g, ln_b, ws, bs_t, n_prompt_rows, tr=1024):
    m = uv.shape[0]
    npb = n_prompt_rows // tr
    return pl.pallas_call(
        functools.partial(_gmlp_kernel, npb, tr),
        grid=(m // tr,),
        in_specs=[
            pl.BlockSpec((tr, D_MODEL), lambda i: (i, 0)),
            pl.BlockSpec((tr, D_MODEL), lambda i: (i, 1)),
            pl.BlockSpec((1, D_MODEL), lambda i: (0, 0)),
            pl.BlockSpec((1, D_MODEL), lambda i: (0, 0)),
            pl.BlockSpec((GMLP_GROUPS, GMLP_CHUNK, GMLP_CHUNK), lambda i: (0, 0, 0)),
            pl.BlockSpec((GMLP_CHUNK, GMLP_GROUPS), lambda i: (0, 0)),
        ],
        out_specs=(
            pl.BlockSpec((tr, D_MODEL), lambda i: (i, 0)),
            pl.BlockSpec((tr, D_MODEL), lambda i: (jnp.maximum(i - npb, 0), 0)),
        ),
        out_shape=(
            jax.ShapeDtypeStruct((m, D_MODEL), BF16),
            jax.ShapeDtypeStruct((m - n_prompt_rows, D_MODEL), F32),
        ),
        compiler_params=_params(("arbitrary",)),
        name="gmlp",
    )(uv, uv, ln_g, ln_b, ws, bs_t)


def _dup_head(pair, pair_swapped, low_half, odd):
    if odd:
        return jnp.where(low_half, pair_swapped, pair).astype(BF16)
    return jnp.where(low_half, pair, pair_swapped).astype(BF16)


def _attend(q_ref, kwin, vwin, mask, sink_ref, o_ref):
    rows, keys = q_ref.shape[0], kwin.shape[0]
    ones = jnp.ones((keys, LANES), BF16)
    low_q = lax.broadcasted_iota(jnp.int32, (rows, LANES), 1) < HEAD_DIM
    low_k = lax.broadcasted_iota(jnp.int32, (keys, LANES), 1) < HEAD_DIM
    heads_per_pair = LANES // HEAD_DIM
    for c in range(N_KV_HEADS // heads_per_pair):
        kpair = kwin[:, c * LANES:(c + 1) * LANES]
        vpair = vwin[:, c * LANES:(c + 1) * LANES]
        kswap = pltpu.roll(kpair, HEAD_DIM, 1)
        vswap = pltpu.roll(vpair, HEAD_DIM, 1)
        for odd in range(heads_per_pair):
            hk = c * heads_per_pair + odd
            kdup = _dup_head(kpair, kswap, low_k, odd)
            vaug = jnp.concatenate([_dup_head(vpair, vswap, low_k, odd), ones], axis=1)
            parts = []
            for r in range(Q_REP):
                h = hk * Q_REP + r
                qpair = q_ref[:, (h // 2) * LANES:(h // 2 + 1) * LANES]
                keep = low_q if h % 2 == 0 else jnp.logical_not(low_q)
                parts.append(jnp.where(keep, qpair, jnp.zeros_like(qpair)))
            s_all = lax.dot_general(jnp.concatenate(parts, axis=0), kdup, (((1,), (1,)), ((), ())),
                                    preferred_element_type=F32)
            ps, sinks, mxs = [], [], []
            for r in range(Q_REP):
                s = s_all[r * rows:(r + 1) * rows]
                if mask is not None:
                    s = mask(s)
                sink = sink_ref[hk * Q_REP + r] * LOG2E
                mx = jnp.maximum(jnp.max(s, axis=-1, keepdims=True), sink)
                ps.append(jnp.exp2(s - mx).astype(BF16))
                sinks.append(sink)
                mxs.append(mx)
            o_all = jnp.dot(jnp.concatenate(ps, axis=0), vaug, preferred_element_type=F32)
            for r in range(0, Q_REP, 2):
                h = hk * Q_REP + r
                even, oddh = o_all[r * rows:(r + 1) * rows], o_all[(r + 1) * rows:(r + 2) * rows]
                esink = jnp.exp2(jnp.where(low_q, sinks[r], sinks[r + 1])
                                 - jnp.where(low_q, mxs[r], mxs[r + 1]))
                den = jnp.where(low_q, even[:, LANES:], oddh[:, LANES:]) + esink
                num = jnp.where(low_q, even[:, :LANES], oddh[:, :LANES])
                o_ref[:, (h // 2) * LANES:(h // 2 + 1) * LANES] = (num / den).astype(o_ref.dtype)


def _attn_prompt_kernel(tq, counts, sink_ref, q_ref, kp_ref, kc_ref, vp_ref, vc_ref, *refs):
    n = len(counts)
    riders, o_ref, cast = refs[:n], refs[n], refs[n + 1:]
    t = pl.program_id(1)
    _rider_cast(pl.program_id(0) * pl.num_programs(1) + t, counts, riders, cast)
    kwin = jnp.concatenate([kp_ref[...], kc_ref[...]], axis=0)
    vwin = jnp.concatenate([vp_ref[...], vc_ref[...]], axis=0)
    row = lax.broadcasted_iota(jnp.int32, (tq, tq), 0)
    col = lax.broadcasted_iota(jnp.int32, (tq, tq), 1)
    valid_prev = ((row < CHUNK) | (col >= CHUNK)) & (t > 0)
    valid_cur_top = lax.broadcasted_iota(jnp.int32, (CHUNK, tq), 1) < CHUNK

    def mask(s):
        prev = jnp.where(valid_prev, s[:, :tq], NEG)
        cur_top = jnp.where(valid_cur_top, s[:CHUNK, tq:], NEG)
        cur = jnp.concatenate([cur_top, s[CHUNK:, tq:]], axis=0)
        return jnp.concatenate([prev, cur], axis=1)

    _attend(q_ref, kwin, vwin, mask, sink_ref, o_ref)


def _attn_prompt(sinks, q, k, va, batch, seq, riders, tq=128):
    m = q.shape[0]
    nt = seq // tq
    assert tq == WINDOW
    cur = lambda b, t: (b * nt + t, 0)
    prev = lambda b, t: (b * nt + jnp.maximum(t - 1, 0), 0)
    counts, r_in, r_out, r_shapes = _rider_specs(riders, batch * nt, lambda b, t: b * nt + t)
    return pl.pallas_call(
        functools.partial(_attn_prompt_kernel, tq, counts),
        grid=(batch, nt),
        in_specs=[
            pl.BlockSpec(memory_space=pltpu.SMEM),
            pl.BlockSpec((tq, Q_WIDTH), cur),
            pl.BlockSpec((tq, KV_WIDTH), prev),
            pl.BlockSpec((tq, KV_WIDTH), cur),
            pl.BlockSpec((tq, KV_WIDTH), prev),
            pl.BlockSpec((tq, KV_WIDTH), cur),
            *r_in,
        ],
        out_specs=(pl.BlockSpec((tq, Q_WIDTH), cur), *r_out),
        out_shape=(jax.ShapeDtypeStruct((m, Q_WIDTH), BF16), *r_shapes),
        compiler_params=_params(("arbitrary", "arbitrary")),
        name="attn_prompt",
    )(sinks, q, k, k, va, va, *riders)


def _attn_sample_kernel(per_step, dec_seq, sink_ref, q_ref, ck_ref, kn_ref, cv_ref, vn_ref,
                        o_in_ref, o_ref):
    del o_in_ref
    kn, vn = kn_ref[...], vn_ref[...]
    for b in range(per_step):
        rows = pl.ds(b * dec_seq, dec_seq)
        new = slice(b * dec_seq, (b + 1) * dec_seq)
        kwin = jnp.concatenate([ck_ref[b], kn[new]], axis=0)
        vwin = jnp.concatenate([cv_ref[b], vn[new]], axis=0)
        _attend(q_ref.at[rows], kwin, vwin, None, sink_ref, o_ref.at[rows])


def _attn_sample(sinks, q, k, va, cache_k, cache_v, o, n_prompt_rows, dec_batch, dec_seq, per_step=4):
    rows = per_step * dec_seq
    first = n_prompt_rows // rows
    new = lambda s: (first + s, 0)
    cached = lambda s: (s, 0, 0)
    cache_len = cache_k.shape[1]
    return pl.pallas_call(
        functools.partial(_attn_sample_kernel, per_step, dec_seq),
        grid=(dec_batch // per_step,),
        in_specs=[
            pl.BlockSpec(memory_space=pltpu.SMEM),
            pl.BlockSpec((rows, Q_WIDTH), new),
            pl.BlockSpec((per_step, cache_len, KV_WIDTH), cached),
            pl.BlockSpec((rows, KV_WIDTH), new),
            pl.BlockSpec((per_step, cache_len, KV_WIDTH), cached),
            pl.BlockSpec((rows, KV_WIDTH), new),
            pl.BlockSpec(memory_space=pl.ANY),
        ],
        out_specs=pl.BlockSpec((rows, Q_WIDTH), new),
        out_shape=jax.ShapeDtypeStruct(o.shape, o.dtype),
        input_output_aliases={6: 0},
        compiler_params=_params(("arbitrary",)),
        name="attn_sample",
    )(sinks, q, cache_k, k, cache_v, va, o)


def _branch_kernel(counts, a_ref, o_ref, wa_ref, wb_ref, ga_ref, gb_ref, *refs):
    n = len(counts)
    riders, t_ref, cast = refs[:n], refs[n], refs[n + 1:]
    _rider_cast(pl.program_id(0) * pl.num_programs(1) + pl.program_id(1), counts, riders, cast)
    ya = jnp.dot(a_ref[...], wa_ref[...], preferred_element_type=F32)
    yb = jnp.dot(o_ref[...], wb_ref[...], preferred_element_type=F32)
    t = ga_ref[...].astype(F32) * ya + gb_ref[...].astype(F32) * yb
    t_ref[...] = t.astype(t_ref.dtype)


def _branch(a, o, wa, wb, g, riders, tm=1536, tn=512):
    m = a.shape[0]
    nj = D_MODEL // tn
    counts, r_in, r_out, r_shapes = _rider_specs(riders, (m // tm) * nj, lambda i, j: i * nj + j)
    return pl.pallas_call(
        functools.partial(_branch_kernel, counts),
        grid=(m // tm, nj),
        in_specs=[
            pl.BlockSpec((tm, D_MODEL), lambda i, j: (i, 0)),
            pl.BlockSpec((tm, Q_WIDTH), lambda i, j: (i, 0)),
            pl.BlockSpec((D_MODEL, tn), lambda i, j: (0, j)),
            pl.BlockSpec((Q_WIDTH, tn), lambda i, j: (0, j)),
            pl.BlockSpec((tm, tn), lambda i, j: (i, j)),
            pl.BlockSpec((tm, tn), lambda i, j: (i, j + nj)),
            *r_in,
        ],
        out_specs=(pl.BlockSpec((tm, tn), lambda i, j: (i, j)), *r_out),
        out_shape=(jax.ShapeDtypeStruct((m, D_MODEL), BF16), *r_shapes),
        compiler_params=_params(("arbitrary", "arbitrary")),
        name="branch",
    )(a, o, wa, wb, g, g, *riders)


def _out_kernel(n_prompt_blocks, t_ref, w_ref, xp_ref, xs_ref, x1_ref):
    i = pl.program_id(0)
    y = jnp.dot(t_ref[...], w_ref[...], preferred_element_type=F32)

    @pl.when(i < n_prompt_blocks)
    def _():
        x1_ref[...] = xp_ref[...] + y

    @pl.when(i >= n_prompt_blocks)
    def _():
        x1_ref[...] = xs_ref[...] + y


def _out(t, w_out, xp, xs, tm=512, tn=D_MODEL):
    m = t.shape[0]
    npb = xp.shape[0] // tm
    return pl.pallas_call(
        functools.partial(_out_kernel, npb),
        grid=(m // tm, D_MODEL // tn),
        in_specs=[
            pl.BlockSpec((tm, D_MODEL), lambda i, j: (i, 0)),
            pl.BlockSpec((D_MODEL, tn), lambda i, j: (0, j)),
            pl.BlockSpec((tm, tn), lambda i, j: (jnp.minimum(i, npb - 1), j)),
            pl.BlockSpec((tm, tn), lambda i, j: (jnp.maximum(i - npb, 0), j)),
        ],
        out_specs=pl.BlockSpec((tm, tn), lambda i, j: (i, j)),
        out_shape=jax.ShapeDtypeStruct((m, D_MODEL), F32),
        compiler_params=_params(("arbitrary", "arbitrary")),
        name="out_proj",
    )(t, w_out, xp, xs)


def _ffn_kernel(x1_ref, ng_ref, wg_ref, wu_ref, wd_ref, fg_ref, y_ref, h_ref):
    f = pl.program_id(1)

    @pl.when(f == 0)
    def _():
        x1 = x1_ref[...]
        h_ref[...] = _rms(x1, ng_ref[...]).astype(h_ref.dtype)
        y_ref[...] = x1

    h = h_ref[...]
    gate = jnp.dot(h, wg_ref[...], preferred_element_type=F32)
    up = jnp.dot(h, wu_ref[...], preferred_element_type=F32)
    hid = (jax.nn.silu(gate) * up).astype(BF16)
    y_ref[...] += jnp.dot(hid, wd_ref[...], preferred_element_type=F32)

    @pl.when(f == pl.num_programs(1) - 1)
    def _():
        y_ref[...] = _rms(y_ref[...], fg_ref[...])


def _ffn(x1, row_block_offset, n_row_blocks, norm_g, wg, wu, wd, final_g, tm=512, tf=512):
    d_ff = wg.shape[1]
    return pl.pallas_call(
        _ffn_kernel,
        grid=(n_row_blocks, d_ff // tf),
        in_specs=[
            pl.BlockSpec((tm, D_MODEL), lambda i, f: (i + row_block_offset, 0)),
            pl.BlockSpec((1, D_MODEL), lambda i, f: (0, 0)),
            pl.BlockSpec((D_MODEL, tf), lambda i, f: (0, f)),
            pl.BlockSpec((D_MODEL, tf), lambda i, f: (0, f)),
            pl.BlockSpec((tf, D_MODEL), lambda i, f: (f, 0)),
            pl.BlockSpec((1, D_MODEL), lambda i, f: (0, 0)),
        ],
        out_specs=pl.BlockSpec((tm, D_MODEL), lambda i, f: (i, 0)),
        out_shape=jax.ShapeDtypeStruct((n_row_blocks * tm, D_MODEL), F32),
        scratch_shapes=[pltpu.VMEM((tm, D_MODEL), BF16)],
        compiler_params=_params(("arbitrary", "arbitrary")),
        name="ffn",
    )(x1, norm_g, wg, wu, wd, final_g)


def _rope_tables(seq, past_len, dec_batch, dec_seq):
    pos = jnp.concatenate([jnp.arange(seq, dtype=F32),
                           jnp.tile(past_len + jnp.arange(dec_seq, dtype=F32), dec_batch)])
    inv_freq = jnp.float32(ROPE_THETA) ** (-(jnp.arange(ROT_HALF, dtype=F32) * 2.0 / ROT_DIM))
    ang = pos[:, None] * inv_freq[None, :]
    cos, sin = jnp.cos(ang), jnp.sin(ang)
    n = pos.shape[0]
    pad = jnp.zeros((n, HEAD_DIM - ROT_DIM), F32)
    zero = jnp.zeros((n, ROT_HALF), F32)
    cos_h = jnp.concatenate([cos, cos, pad + 1.0], axis=1)
    sa_h = jnp.concatenate([zero, sin, pad], axis=1)
    sb_h = jnp.concatenate([-sin, zero, pad], axis=1)
    rep = LANES // HEAD_DIM
    return jnp.tile(cos_h, (1, rep)), jnp.tile(sa_h, (1, rep)), jnp.tile(sb_h, (1, rep))


def kernel(x_prompt, x_sample, cache_swa_k, cache_swa_v, norm_mix_g, w_in, gmlp_ln_g, gmlp_ln_b,
           gmlp_ws, gmlp_bs, attn_sinks, w_gate, b_gate, w_branch_a, w_branch_b, w_out,
           norm_ffn_g, w_ffn_gate, w_ffn_up, w_ffn_down, final_norm_g):
    batch, seq, _ = x_prompt.shape
    dec_batch, dec_seq, _ = x_sample.shape
    depth = w_in.shape[0]
    assert depth == 1 and cache_swa_k.shape[2] == WINDOW and dec_seq == CHUNK
    mp, ms = batch * seq, dec_batch * dec_seq

    xp = x_prompt.reshape(mp, D_MODEL)
    xs = x_sample.reshape(ms, D_MODEL)
    cos_t, sa_t, sb_t = _rope_tables(seq, PAST_LEN, dec_batch, dec_seq)
    row = lambda p: p.reshape(1, -1)

    l = 0
    xn = _norm(xp, xs, row(norm_mix_g[l]))
    g, w_in_b = _proj_gate(xn, w_gate[l], row(b_gate[l]), [w_in[l]])
    q, k, va = _proj_qkv(xn, w_in_b, cos_t, sa_t, sb_t, mp, seq)
    uv, wg, wu = _proj_uv(xn, w_in_b, [w_ffn_gate[l], w_ffn_up[l]])
    a, vn_s = _gmlp(uv, row(gmlp_ln_g[l]), row(gmlp_ln_b[l]), gmlp_ws[l], gmlp_bs[l].T, mp)
    o, wa_b, wb_b, wo_b = _attn_prompt(
        attn_sinks[l], q, k, va, batch, seq, [w_branch_a[l], w_branch_b[l], w_out[l]])
    o = _attn_sample(attn_sinks[l], q, k, va,
                     cache_swa_k[l].reshape(dec_batch, WINDOW, KV_WIDTH),
                     cache_swa_v[l].reshape(dec_batch, WINDOW, KV_WIDTH),
                     o, mp, dec_batch, dec_seq)
    t, wd = _branch(a, o, wa_b, wb_b, g, [w_ffn_down[l]])
    x1 = _out(t, wo_b, xp, xs)
    ffn_tm = 1024
    ffn = functools.partial(_ffn, norm_g=row(norm_ffn_g[l]), wg=wg, wu=wu, wd=wd,
                            final_g=row(final_norm_g), tm=ffn_tm)
    y_prompt = ffn(x1, 0, mp // ffn_tm)
    y_sample = ffn(x1, mp // ffn_tm, ms // ffn_tm)

    keep = min(WINDOW, seq)
    tail = lambda z: jnp.stack([z[(b + 1) * seq - keep:(b + 1) * seq] for b in range(batch)]).reshape(
        batch, keep, N_KV_HEADS, HEAD_DIM)
    kp, vp = tail(k), tail(va)
    return (
        y_prompt.reshape(batch, seq, D_MODEL),
        y_sample.reshape(dec_batch, dec_seq, D_MODEL),
        kp[None],
        vp[None],
        k[mp:].reshape(1, dec_batch, dec_seq, N_KV_HEADS, HEAD_DIM),
        va[mp:].reshape(1, dec_batch, dec_seq, N_KV_HEADS, HEAD_DIM),
        vn_s.reshape(1, dec_batch, dec_seq, D_MODEL),
    )
```

```python
import functools

import jax
import jax.numpy as jnp
from jax import lax
from jax.experimental import pallas as pl
from jax.experimental.pallas import tpu as pltpu

D_MODEL = 2048
CHUNK = 64
GMLP_CHUNK = 128
GMLP_GROUPS = 8
GMLP_GROUP_DIM = D_MODEL // GMLP_GROUPS
N_HEADS = 32
N_KV_HEADS = 4
HEAD_DIM = 64
Q_REP = N_HEADS // N_KV_HEADS
WINDOW = 128
PAST_LEN = 2048
ROPE_THETA = 500000.0
ROT_DIM = HEAD_DIM // 4
ROT_HALF = ROT_DIM // 2
Q_WIDTH = N_HEADS * HEAD_DIM
KV_WIDTH = N_KV_HEADS * HEAD_DIM
EPS = 1e-6
NEG = -1e30
LANES = 128
LOG2E = 1.4426950408889634
Q_SCALE = HEAD_DIM ** -0.5 * LOG2E

F32 = jnp.float32
BF16 = jnp.bfloat16

VMEM_LIMIT = 56 * 1024 * 1024


def _params(semantics):
    return pltpu.CompilerParams(dimension_semantics=semantics, vmem_limit_bytes=VMEM_LIMIT)


def _rms(x, g):
    return x * lax.rsqrt(jnp.mean(x * x, axis=-1, keepdims=True) + EPS) * g


BF16_SUBLANES = 16


def _rider_specs(weights, n_steps, step_of):
    counts, in_specs, out_specs, out_shapes = [], [], [], []
    for w in weights:
        rows, cols = w.shape
        nb = max(n for n in range(1, n_steps + 1)
                 if rows % n == 0 and (rows // n) % BF16_SUBLANES == 0)
        idx = lambda *g, nb=nb: (jnp.minimum(step_of(*g), nb - 1), 0)
        counts.append(nb)
        in_specs.append(pl.BlockSpec((rows // nb, cols), idx))
        out_specs.append(pl.BlockSpec((rows // nb, cols), idx))
        out_shapes.append(jax.ShapeDtypeStruct(w.shape, BF16))
    return counts, in_specs, out_specs, out_shapes


def _rider_cast(step, counts, src_refs, dst_refs):
    for nb, src, dst in zip(counts, src_refs, dst_refs):
        @pl.when(step < nb)
        def _():
            dst[...] = src[...].astype(dst.dtype)


def _norm_kernel(n_prompt_blocks, xp_ref, xs_ref, g_ref, o_ref):
    i = pl.program_id(0)

    @pl.when(i < n_prompt_blocks)
    def _():
        o_ref[...] = _rms(xp_ref[...], g_ref[...]).astype(BF16)

    @pl.when(i >= n_prompt_blocks)
    def _():
        o_ref[...] = _rms(xs_ref[...], g_ref[...]).astype(BF16)


def _norm(xp, xs, g, tr=1024):
    mp, ms = xp.shape[0], xs.shape[0]
    npb, nsb = mp // tr, ms // tr
    return pl.pallas_call(
        functools.partial(_norm_kernel, npb),
        grid=(npb + nsb,),
        in_specs=[
            pl.BlockSpec((tr, D_MODEL), lambda i: (jnp.minimum(i, npb - 1), 0)),
            pl.BlockSpec((tr, D_MODEL), lambda i: (jnp.maximum(i - npb, 0), 0)),
            pl.BlockSpec((1, D_MODEL), lambda i: (0, 0)),
        ],
        out_specs=pl.BlockSpec((tr, D_MODEL), lambda i: (i, 0)),
        out_shape=jax.ShapeDtypeStruct((mp + ms, D_MODEL), BF16),
        compiler_params=_params(("arbitrary",)),
        name="norm",
    )(xp, xs, g)


def _rope(h, cos, sa, sb):
    return h * cos + pltpu.roll(h, ROT_HALF, 1) * sa + pltpu.roll(h, LANES - ROT_HALF, 1) * sb


GELU_C1 = 0.7978845608028654
GELU_C3 = 0.035677408136300125


def _gelu_tanh(x):
    return (0.5 * x) * (1.0 + jnp.tanh(x * (GELU_C1 + GELU_C3 * (x * x))))


def _proj_uv_kernel(counts, xn_ref, w_ref, *refs):
    n = len(counts)
    riders, uv_ref, cast = refs[:n], refs[n], refs[n + 1:]
    _rider_cast(pl.program_id(0) * pl.num_programs(1) + pl.program_id(1), counts, riders, cast)
    h = jnp.dot(xn_ref[...], w_ref[...], preferred_element_type=F32)
    uv_ref[...] = _gelu_tanh(h).astype(uv_ref.dtype)


def _proj_uv(xn, w_in, riders, tm=1024, tn=1024):
    m = xn.shape[0]
    nj = (2 * D_MODEL) // tn
    counts, r_in, r_out, r_shapes = _rider_specs(riders, (m // tm) * nj, lambda i, j: i * nj + j)
    return pl.pallas_call(
        functools.partial(_proj_uv_kernel, counts),
        grid=(m // tm, nj),
        in_specs=[
            pl.BlockSpec((tm, D_MODEL), lambda i, j: (i, 0)),
            pl.BlockSpec((D_MODEL, tn), lambda i, j: (0, j)),
            *r_in,
        ],
        out_specs=(pl.BlockSpec((tm, tn), lambda i, j: (i, j)), *r_out),
        out_shape=(jax.ShapeDtypeStruct((m, 2 * D_MODEL), BF16), *r_shapes),
        compiler_params=_params(("arbitrary", "arbitrary")),
        name="proj_uv",
    )(xn, w_in, *riders)


def _proj_qkv_kernel(xn_ref, wq_ref, wkv_ref, cos_ref, sa_ref, sb_ref, q_ref, k_ref, va_ref):
    j = pl.program_id(1)
    xn = xn_ref[...]

    def rope_cols(h, scale):
        cos, sa, sb = cos_ref[...], sa_ref[...], sb_ref[...]
        parts = [_rope(h[:, c:c + LANES], cos, sa, sb) * scale for c in range(0, h.shape[1], LANES)]
        return jnp.concatenate(parts, axis=1)

    @pl.when(j == 0)
    def _():
        h = jnp.dot(xn, wkv_ref[...], preferred_element_type=F32)
        k_ref[...] = rope_cols(h[:, :KV_WIDTH], 1.0)
        va_ref[...] = h[:, KV_WIDTH:]

    @pl.when(j > 0)
    def _():
        h = jnp.dot(xn, wq_ref[...], preferred_element_type=F32)
        q_ref[...] = rope_cols(h, Q_SCALE).astype(q_ref.dtype)


def _proj_qkv(xn, w_in, cos_t, sa_t, sb_t, n_prompt_rows, seq, tm=1024, tn=Q_WIDTH):
    m = xn.shape[0]
    nq = Q_WIDTH // tn
    q_block0 = (2 * D_MODEL) // tn
    kv_block = (2 * D_MODEL + Q_WIDTH) // (2 * KV_WIDTH)
    npb = n_prompt_rows // tm
    blocks_per_seq = seq // tm

    def tab_idx(i, j):
        return (jnp.where(i < npb, i % blocks_per_seq, blocks_per_seq), 0)

    tab_spec = pl.BlockSpec((tm, LANES), tab_idx)
    qcol = lambda j: jnp.maximum(j - 1, 0)
    return pl.pallas_call(
        _proj_qkv_kernel,
        grid=(m // tm, nq + 1),
        in_specs=[
            pl.BlockSpec((tm, D_MODEL), lambda i, j: (i, 0)),
            pl.BlockSpec((D_MODEL, tn), lambda i, j: (0, q_block0 + qcol(j))),
            pl.BlockSpec((D_MODEL, 2 * KV_WIDTH), lambda i, j: (0, kv_block)),
            tab_spec, tab_spec, tab_spec,
        ],
        out_specs=(
            pl.BlockSpec((tm, tn), lambda i, j: (i, qcol(j))),
            pl.BlockSpec((tm, KV_WIDTH), lambda i, j: (i, 0)),
            pl.BlockSpec((tm, KV_WIDTH), lambda i, j: (i, 0)),
        ),
        out_shape=(
            jax.ShapeDtypeStruct((m, Q_WIDTH), BF16),
            jax.ShapeDtypeStruct((m, KV_WIDTH), F32),
            jax.ShapeDtypeStruct((m, KV_WIDTH), F32),
        ),
        compiler_params=_params(("arbitrary", "arbitrary")),
        name="proj_qkv",
    )(xn, w_in, w_in, cos_t, sa_t, sb_t)


def _proj_gate_kernel(counts, xn_ref, w_ref, b_ref, *refs):
    n = len(counts)
    riders, g_ref, cast = refs[:n], refs[n], refs[n + 1:]
    h = jnp.dot(xn_ref[...], w_ref[...].astype(BF16), preferred_element_type=F32)
    g = jax.nn.sigmoid(h + b_ref[...]).astype(g_ref.dtype)
    tb = g_ref.shape[2]
    for c in range(g_ref.shape[0]):
        g_ref[c] = g[:, c * tb:(c + 1) * tb]
    _rider_cast(pl.program_id(0) * pl.num_programs(1) + pl.program_id(1), counts, riders, cast)


def _proj_gate(xn, w_gate, b_gate, riders, tm=1536, tn=1024, tb=512):
    m = xn.shape[0]
    n = w_gate.shape[1]
    nj = n // tn
    counts, r_in, r_out, r_shapes = _rider_specs(riders, (m // tm) * nj, lambda i, j: i * nj + j)
    return pl.pallas_call(
        functools.partial(_proj_gate_kernel, counts),
        grid=(m // tm, nj),
        in_specs=[
            pl.BlockSpec((tm, D_MODEL), lambda i, j: (i, 0)),
            pl.BlockSpec((D_MODEL, tn), lambda i, j: (0, j)),
            pl.BlockSpec((1, tn), lambda i, j: (0, j)),
            *r_in,
        ],
        out_specs=(pl.BlockSpec((tn // tb, tm, tb), lambda i, j: (j, i, 0)), *r_out),
        out_shape=(jax.ShapeDtypeStruct((n // tb, m, tb), BF16), *r_shapes),
        compiler_params=_params(("arbitrary", "arbitrary")),
        name="proj_gate",
    )(xn, w_gate, b_gate, *riders)


def _gmlp_kernel(n_prompt_blocks, tr, u_ref, v_ref, lng_ref, lnb_ref, ws_ref, bst_ref,
                 a_ref, vn_ref):
    i = pl.program_id(0)
    v = v_ref[...].astype(F32)
    mu = jnp.mean(v, axis=-1, keepdims=True)
    vc = v - mu
    var = jnp.mean(vc * vc, axis=-1, keepdims=True)
    vn = vc * lax.rsqrt(var + EPS) * lng_ref[...] + lnb_ref[...]
    vnb = vn.astype(BF16)

    def mix(length):
        r = lax.broadcasted_iota(jnp.int32, (length, length), 0) // CHUNK
        c = lax.broadcasted_iota(jnp.int32, (length, length), 1) // CHUNK
        for g in range(GMLP_GROUPS):
            w = jnp.where(c <= r, ws_ref[g, :length, :length], 0.0).astype(BF16)
            bias = bst_ref[:length, g:g + 1]
            cols = slice(g * GMLP_GROUP_DIM, (g + 1) * GMLP_GROUP_DIM)
            for b in range(tr // length):
                rows = slice(b * length, (b + 1) * length)
                s = jnp.dot(w, vnb[rows, cols], preferred_element_type=F32) + bias
                a_ref[rows, cols] = (u_ref[rows, cols].astype(F32) * s).astype(a_ref.dtype)

    @pl.when(i < n_prompt_blocks)
    def _():
        mix(GMLP_CHUNK)

    @pl.when(i >= n_prompt_blocks)
    def _():
        vn_ref[...] = vn
        mix(CHUNK)


def _gmlp(uv, ln_g, ln_b, ws, bs_t, n_prompt_rows, tr=1024):
    m = uv.shape[0]
    npb = n_prompt_rows // tr
    return pl.pallas_call(
        functools.partial(_gmlp_kernel, npb, tr),
        grid=(m // tr,),
        in_specs=[
            pl.BlockSpec((tr, D_MODEL), lambda i: (i, 0)),
            pl.BlockSpec((tr, D_MODEL), lambda i: (i, 1)),
            pl.BlockSpec((1, D_MODEL), lambda i: (0, 0)),
            pl.BlockSpec((1, D_MODEL), lambda i: (0, 0)),
            pl.BlockSpec((GMLP_GROUPS, GMLP_CHUNK, GMLP_CHUNK), lambda i: (0, 0, 0)),
            pl.BlockSpec((GMLP_CHUNK, GMLP_GROUPS), lambda i: (0, 0)),
        ],
        out_specs=(
            pl.BlockSpec((tr, D_MODEL), lambda i: (i, 0)),
            pl.BlockSpec((tr, D_MODEL), lambda i: (jnp.maximum(i - npb, 0), 0)),
        ),
        out_shape=(
            jax.ShapeDtypeStruct((m, D_MODEL), BF16),
            jax.ShapeDtypeStruct((m - n_prompt_rows, D_MODEL), F32),
        ),
        compiler_params=_params(("arbitrary",)),
        name="gmlp",
    )(uv, uv, ln_g, ln_b, ws, bs_t)


def _dup_head(pair, pair_swapped, low_half, odd):
    if odd:
        return jnp.where(low_half, pair_swapped, pair).astype(BF16)
    return jnp.where(low_half, pair, pair_swapped).astype(BF16)


def _attend(q_ref, kwin, vwin, mask, sink_ref, o_ref):
    rows, keys = q_ref.shape[0], kwin.shape[0]
    ones = jnp.ones((keys, LANES), BF16)
    low_q = lax.broadcasted_iota(jnp.int32, (rows, LANES), 1) < HEAD_DIM
    low_k = lax.broadcasted_iota(jnp.int32, (keys, LANES), 1) < HEAD_DIM
    heads_per_pair = LANES // HEAD_DIM
    for c in range(N_KV_HEADS // heads_per_pair):
        kpair = kwin[:, c * LANES:(c + 1) * LANES]
        vpair = vwin[:, c * LANES:(c + 1) * LANES]
        kswap = pltpu.roll(kpair, HEAD_DIM, 1)
        vswap = pltpu.roll(vpair, HEAD_DIM, 1)
        for odd in range(heads_per_pair):
            hk = c * heads_per_pair + odd
            kdup = _dup_head(kpair, kswap, low_k, odd)
            vaug = jnp.concatenate([_dup_head(vpair, vswap, low_k, odd), ones], axis=1)
            parts = []
            for r in range(Q_REP):
                h = hk * Q_REP + r
                qpair = q_ref[:, (h // 2) * LANES:(h // 2 + 1) * LANES]
                keep = low_q if h % 2 == 0 else jnp.logical_not(low_q)
                parts.append(jnp.where(keep, qpair, jnp.zeros_like(qpair)))
            s_all = lax.dot_general(jnp.concatenate(parts, axis=0), kdup, (((1,), (1,)), ((), ())),
                                    preferred_element_type=F32)
            ps, sinks, mxs = [], [], []
            for r in range(Q_REP):
                s = s_all[r * rows:(r + 1) * rows]
                if mask is not None:
                    s = mask(s)
                sink = sink_ref[hk * Q_REP + r] * LOG2E
                mx = jnp.maximum(jnp.max(s, axis=-1, keepdims=True), sink)
                ps.append(jnp.exp2(s - mx).astype(BF16))
                sinks.append(sink)
                mxs.append(mx)
            o_all = jnp.dot(jnp.concatenate(ps, axis=0), vaug, preferred_element_type=F32)
            for r in range(0, Q_REP, 2):
                h = hk * Q_REP + r
                even, oddh = o_all[r * rows:(r + 1) * rows], o_all[(r + 1) * rows:(r + 2) * rows]
                esink = jnp.exp2(jnp.where(low_q, sinks[r], sinks[r + 1])
                                 - jnp.where(low_q, mxs[r], mxs[r + 1]))
                den = jnp.where(low_q, even[:, LANES:], oddh[:, LANES:]) + esink
                num = jnp.where(low_q, even[:, :LANES], oddh[:, :LANES])
                o_ref[:, (h // 2) * LANES:(h // 2 + 1) * LANES] = (num / den).astype(o_ref.dtype)


def _attn_prompt_kernel(tq, counts, sink_ref, q_ref, kp_ref, kc_ref, vp_ref, vc_ref, *refs):
    n = len(counts)
    riders, o_ref, cast = refs[:n], refs[n], refs[n + 1:]
    t = pl.program_id(1)
    _rider_cast(pl.program_id(0) * pl.num_programs(1) + t, counts, riders, cast)
    kwin = jnp.concatenate([kp_ref[...], kc_ref[...]], axis=0)
    vwin = jnp.concatenate([vp_ref[...], vc_ref[...]], axis=0)
    row = lax.broadcasted_iota(jnp.int32, (tq, tq), 0)
    col = lax.broadcasted_iota(jnp.int32, (tq, tq), 1)
    valid_prev = ((row < CHUNK) | (col >= CHUNK)) & (t > 0)
    valid_cur_top = lax.broadcasted_iota(jnp.int32, (CHUNK, tq), 1) < CHUNK

    def mask(s):
        prev = jnp.where(valid_prev, s[:, :tq], NEG)
        cur_top = jnp.where(valid_cur_top, s[:CHUNK, tq:], NEG)
        cur = jnp.concatenate([cur_top, s[CHUNK:, tq:]], axis=0)
        return jnp.concatenate([prev, cur], axis=1)

    _attend(q_ref, kwin, vwin, mask, sink_ref, o_ref)


def _attn_prompt(sinks, q, k, va, batch, seq, riders, tq=128):
    m = q.shape[0]
    nt = seq // tq
    assert tq == WINDOW
    cur = lambda b, t: (b * nt + t, 0)
    prev = lambda b, t: (b * nt + jnp.maximum(t - 1, 0), 0)
    counts, r_in, r_out, r_shapes = _rider_specs(riders, batch * nt, lambda b, t: b * nt + t)
    return pl.pallas_call(
        functools.partial(_attn_prompt_kernel, tq, counts),
        grid=(batch, nt),
        in_specs=[
            pl.BlockSpec(memory_space=pltpu.SMEM),
            pl.BlockSpec((tq, Q_WIDTH), cur),
            pl.BlockSpec((tq, KV_WIDTH), prev),
            pl.BlockSpec((tq, KV_WIDTH), cur),
            pl.BlockSpec((tq, KV_WIDTH), prev),
            pl.BlockSpec((tq, KV_WIDTH), cur),
            *r_in,
        ],
        out_specs=(pl.BlockSpec((tq, Q_WIDTH), cur), *r_out),
        out_shape=(jax.ShapeDtypeStruct((m, Q_WIDTH), BF16), *r_shapes),
        compiler_params=_params(("arbitrary", "arbitrary")),
        name="attn_prompt",
    )(sinks, q, k, k, va, va, *riders)


def _attn_sample_kernel(per_step, dec_seq, sink_ref, q_ref, ck_ref, kn_ref, cv_ref, vn_ref,
                        o_in_ref, o_ref):
    del o_in_ref
    kn, vn = kn_ref[...], vn_ref[...]
    for b in range(per_step):
        rows = pl.ds(b * dec_seq, dec_seq)
        new = slice(b * dec_seq, (b + 1) * dec_seq)
        kwin = jnp.concatenate([ck_ref[b], kn[new]], axis=0)
        vwin = jnp.concatenate([cv_ref[b], vn[new]], axis=0)
        _attend(q_ref.at[rows], kwin, vwin, None, sink_ref, o_ref.at[rows])


def _attn_sample(sinks, q, k, va, cache_k, cache_v, o, n_prompt_rows, dec_batch, dec_seq, per_step=4):
    rows = per_step * dec_seq
    first = n_prompt_rows // rows
    new = lambda s: (first + s, 0)
    cached = lambda s: (s, 0, 0)
    cache_len = cache_k.shape[1]
    return pl.pallas_call(
        functools.partial(_attn_sample_kernel, per_step, dec_seq),
        grid=(dec_batch // per_step,),
        in_specs=[
            pl.BlockSpec(memory_space=pltpu.SMEM),
            pl.BlockSpec((rows, Q_WIDTH), new),
            pl.BlockSpec((per_step, cache_len, KV_WIDTH), cached),
            pl.BlockSpec((rows, KV_WIDTH), new),
            pl.BlockSpec((per_step, cache_len, KV_WIDTH), cached),
            pl.BlockSpec((rows, KV_WIDTH), new),
            pl.BlockSpec(memory_space=pl.ANY),
        ],
        out_specs=pl.BlockSpec((rows, Q_WIDTH), new),
        out_shape=jax.ShapeDtypeStruct(o.shape, o.dtype),
        input_output_aliases={6: 0},
        compiler_params=_params(("arbitrary",)),
        name="attn_sample",
    )(sinks, q, cache_k, k, cache_v, va, o)


def _branch_kernel(counts, a_ref, o_ref, wa_ref, wb_ref, ga_ref, gb_ref, *refs):
    n = len(counts)
    riders, t_ref, cast = refs[:n], refs[n], refs[n + 1:]
    _rider_cast(pl.program_id(0) * pl.num_programs(1) + pl.program_id(1), counts, riders, cast)
    ya = jnp.dot(a_ref[...], wa_ref[...], preferred_element_type=F32)
    yb = jnp.dot(o_ref[...], wb_ref[...], preferred_element_type=F32)
    t = ga_ref[...].astype(F32) * ya + gb_ref[...].astype(F32) * yb
    t_ref[...] = t.astype(t_ref.dtype)


def _branch(a, o, wa, wb, g, riders, tm=1536):
    m = a.shape[0]
    tn = g.shape[2]
    nj = D_MODEL // tn
    counts, r_in, r_out, r_shapes = _rider_specs(riders, (m // tm) * nj, lambda i, j: i * nj + j)
    return pl.pallas_call(
        functools.partial(_branch_kernel, counts),
        grid=(m // tm, nj),
        in_specs=[
            pl.BlockSpec((tm, D_MODEL), lambda i, j: (i, 0)),
            pl.BlockSpec((tm, Q_WIDTH), lambda i, j: (i, 0)),
            pl.BlockSpec((D_MODEL, tn), lambda i, j: (0, j)),
            pl.BlockSpec((Q_WIDTH, tn), lambda i, j: (0, j)),
            pl.BlockSpec((None, tm, tn), lambda i, j: (j, i, 0)),
            pl.BlockSpec((None, tm, tn), lambda i, j: (j + nj, i, 0)),
            *r_in,
        ],
        out_specs=(pl.BlockSpec((tm, tn), lambda i, j: (i, j)), *r_out),
        out_shape=(jax.ShapeDtypeStruct((m, D_MODEL), BF16), *r_shapes),
        compiler_params=_params(("arbitrary", "arbitrary")),
        name="branch",
    )(a, o, wa, wb, g, g, *riders)


def _out_kernel(n_prompt_blocks, t_ref, w_ref, xp_ref, xs_ref, x1_ref):
    i = pl.program_id(0)
    y = jnp.dot(t_ref[...], w_ref[...], preferred_element_type=F32)

    @pl.when(i < n_prompt_blocks)
    def _():
        x1_ref[...] = xp_ref[...] + y

    @pl.when(i >= n_prompt_blocks)
    def _():
        x1_ref[...] = xs_ref[...] + y


def _out(t, w_out, xp, xs, tm=512, tn=D_MODEL):
    m = t.shape[0]
    npb = xp.shape[0] // tm
    return pl.pallas_call(
        functools.partial(_out_kernel, npb),
        grid=(m // tm, D_MODEL // tn),
        in_specs=[
            pl.BlockSpec((tm, D_MODEL), lambda i, j: (i, 0)),
            pl.BlockSpec((D_MODEL, tn), lambda i, j: (0, j)),
            pl.BlockSpec((tm, tn), lambda i, j: (jnp.minimum(i, npb - 1), j)),
            pl.BlockSpec((tm, tn), lambda i, j: (jnp.maximum(i - npb, 0), j)),
        ],
        out_specs=pl.BlockSpec((tm, tn), lambda i, j: (i, j)),
        out_shape=jax.ShapeDtypeStruct((m, D_MODEL), F32),
        compiler_params=_params(("arbitrary", "arbitrary")),
        name="out_proj",
    )(t, w_out, xp, xs)


def _ffn_kernel(x1_ref, ng_ref, wg_ref, wu_ref, wd_ref, fg_ref, y_ref, h_ref):
    f = pl.program_id(1)

    @pl.when(f == 0)
    def _():
        x1 = x1_ref[...]
        h_ref[...] = _rms(x1, ng_ref[...]).astype(h_ref.dtype)
        y_ref[...] = x1

    h = h_ref[...]
    gate = jnp.dot(h, wg_ref[...], preferred_element_type=F32)
    up = jnp.dot(h, wu_ref[...], preferred_element_type=F32)
    hid = (jax.nn.silu(gate) * up).astype(BF16)
    y_ref[...] += jnp.dot(hid, wd_ref[...], preferred_element_type=F32)

    @pl.when(f == pl.num_programs(1) - 1)
    def _():
        y_ref[...] = _rms(y_ref[...], fg_ref[...])


def _ffn(x1, row_block_offset, n_row_blocks, norm_g, wg, wu, wd, final_g, tm=512, tf=512):
    d_ff = wg.shape[1]
    return pl.pallas_call(
        _ffn_kernel,
        grid=(n_row_blocks, d_ff // tf),
        in_specs=[
            pl.BlockSpec((tm, D_MODEL), lambda i, f: (i + row_block_offset, 0)),
            pl.BlockSpec((1, D_MODEL), lambda i, f: (0, 0)),
            pl.BlockSpec((D_MODEL, tf), lambda i, f: (0, f)),
            pl.BlockSpec((D_MODEL, tf), lambda i, f: (0, f)),
            pl.BlockSpec((tf, D_MODEL), lambda i, f: (f, 0)),
            pl.BlockSpec((1, D_MODEL), lambda i, f: (0, 0)),
        ],
        out_specs=pl.BlockSpec((tm, D_MODEL), lambda i, f: (i, 0)),
        out_shape=jax.ShapeDtypeStruct((n_row_blocks * tm, D_MODEL), F32),
        scratch_shapes=[pltpu.VMEM((tm, D_MODEL), BF16)],
        compiler_params=_params(("arbitrary", "arbitrary")),
        name="ffn",
    )(x1, norm_g, wg, wu, wd, final_g)


def _rope_tables(seq, past_len, dec_batch, dec_seq):
    pos = jnp.concatenate([jnp.arange(seq, dtype=F32),
                           jnp.tile(past_len + jnp.arange(dec_seq, dtype=F32), dec_batch)])
    inv_freq = jnp.float32(ROPE_THETA) ** (-(jnp.arange(ROT_HALF, dtype=F32) * 2.0 / ROT_DIM))
    ang = pos[:, None] * inv_freq[None, :]
    cos, sin = jnp.cos(ang), jnp.sin(ang)
    n = pos.shape[0]
    pad = jnp.zeros((n, HEAD_DIM - ROT_DIM), F32)
    zero = jnp.zeros((n, ROT_HALF), F32)
    cos_h = jnp.concatenate([cos, cos, pad + 1.0], axis=1)
    sa_h = jnp.concatenate([zero, sin, pad], axis=1)
    sb_h = jnp.concatenate([-sin, zero, pad], axis=1)
    rep = LANES // HEAD_DIM
    return jnp.tile(cos_h, (1, rep)), jnp.tile(sa_h, (1, rep)), jnp.tile(sb_h, (1, rep))


def kernel(x_prompt, x_sample, cache_swa_k, cache_swa_v, norm_mix_g, w_in, gmlp_ln_g, gmlp_ln_b,
           gmlp_ws, gmlp_bs, attn_sinks, w_gate, b_gate, w_branch_a, w_branch_b, w_out,
           norm_ffn_g, w_ffn_gate, w_ffn_up, w_ffn_down, final_norm_g):
    batch, seq, _ = x_prompt.shape
    dec_batch, dec_seq, _ = x_sample.shape
    depth = w_in.shape[0]
    assert depth == 1 and cache_swa_k.shape[2] == WINDOW and dec_seq == CHUNK
    mp, ms = batch * seq, dec_batch * dec_seq

    xp = x_prompt.reshape(mp, D_MODEL)
    xs = x_sample.reshape(ms, D_MODEL)
    cos_t, sa_t, sb_t = _rope_tables(seq, PAST_LEN, dec_batch, dec_seq)
    row = lambda p: p.reshape(1, -1)

    l = 0
    xn = _norm(xp, xs, row(norm_mix_g[l]))
    g, w_in_b = _proj_gate(xn, w_gate[l], row(b_gate[l]), [w_in[l]])
    q, k, va = _proj_qkv(xn, w_in_b, cos_t, sa_t, sb_t, mp, seq)
    uv, wg, wu = _proj_uv(xn, w_in_b, [w_ffn_gate[l], w_ffn_up[l]])
    a, vn_s = _gmlp(uv, row(gmlp_ln_g[l]), row(gmlp_ln_b[l]), gmlp_ws[l], gmlp_bs[l].T, mp)
    o, wa_b, wb_b, wo_b = _attn_prompt(
        attn_sinks[l], q, k, va, batch, seq, [w_branch_a[l], w_branch_b[l], w_out[l]])
    o = _attn_sample(attn_sinks[l], q, k, va,
                     cache_swa_k[l].reshape(dec_batch, WINDOW, KV_WIDTH),
                     cache_swa_v[l].reshape(dec_batch, WINDOW, KV_WIDTH),
                     o, mp, dec_batch, dec_seq)
    t, wd = _branch(a, o, wa_b, wb_b, g, [w_ffn_down[l]])
    x1 = _out(t, wo_b, xp, xs)
    ffn_tm = 1024
    ffn = functools.partial(_ffn, norm_g=row(norm_ffn_g[l]), wg=wg, wu=wu, wd=wd,
                            final_g=row(final_norm_g), tm=ffn_tm)
    y_prompt = ffn(x1, 0, mp // ffn_tm)
    y_sample = ffn(x1, mp // ffn_tm, ms // ffn_tm)

    keep = min(WINDOW, seq)
    tail = lambda z: jnp.stack([z[(b + 1) * seq - keep:(b + 1) * seq] for b in range(batch)]).reshape(
        batch, keep, N_KV_HEADS, HEAD_DIM)
    kp, vp = tail(k), tail(va)
    return (
        y_prompt.reshape(batch, seq, D_MODEL),
        y_sample.reshape(dec_batch, dec_seq, D_MODEL),
        kp[None],
        vp[None],
        k[mp:].reshape(1, dec_batch, dec_seq, N_KV_HEADS, HEAD_DIM),
        va[mp:].reshape(1, dec_batch, dec_seq, N_KV_HEADS, HEAD_DIM),
        vn_s.reshape(1, dec_batch, dec_seq, D_MODEL),
    )
```

```python
import functools

import jax
import jax.numpy as jnp
from jax import lax
from jax.experimental import pallas as pl
from jax.experimental.pallas import tpu as pltpu

D_MODEL = 2048
CHUNK = 64
GMLP_CHUNK = 128
GMLP_GROUPS = 8
GMLP_GROUP_DIM = D_MODEL // GMLP_GROUPS
N_HEADS = 32
N_KV_HEADS = 4
HEAD_DIM = 64
Q_REP = N_HEADS // N_KV_HEADS
WINDOW = 128
PAST_LEN = 2048
ROPE_THETA = 500000.0
ROT_DIM = HEAD_DIM // 4
ROT_HALF = ROT_DIM // 2
Q_WIDTH = N_HEADS * HEAD_DIM
KV_WIDTH = N_KV_HEADS * HEAD_DIM
EPS = 1e-6
NEG = -1e30
LANES = 128
LOG2E = 1.4426950408889634
Q_SCALE = HEAD_DIM ** -0.5 * LOG2E

F32 = jnp.float32
BF16 = jnp.bfloat16

VMEM_LIMIT = 56 * 1024 * 1024


def _params(semantics):
    return pltpu.CompilerParams(dimension_semantics=semantics, vmem_limit_bytes=VMEM_LIMIT)


def _rms(x, g):
    return x * lax.rsqrt(jnp.mean(x * x, axis=-1, keepdims=True) + EPS) * g


BF16_SUBLANES = 16


def _rider_specs(weights, n_steps, step_of):
    counts, in_specs, out_specs, out_shapes = [], [], [], []
    for w in weights:
        rows, cols = w.shape
        nb = max(n for n in range(1, n_steps + 1)
                 if rows % n == 0 and (rows // n) % BF16_SUBLANES == 0)
        idx = lambda *g, nb=nb: (jnp.minimum(step_of(*g), nb - 1), 0)
        counts.append(nb)
        in_specs.append(pl.BlockSpec((rows // nb, cols), idx))
        out_specs.append(pl.BlockSpec((rows // nb, cols), idx))
        out_shapes.append(jax.ShapeDtypeStruct(w.shape, BF16))
    return counts, in_specs, out_specs, out_shapes


def _rider_cast(step, counts, src_refs, dst_refs):
    for nb, src, dst in zip(counts, src_refs, dst_refs):
        @pl.when(step < nb)
        def _():
            dst[...] = src[...].astype(dst.dtype)


def _norm_kernel(n_prompt_blocks, xp_ref, xs_ref, g_ref, o_ref):
    i = pl.program_id(0)

    @pl.when(i < n_prompt_blocks)
    def _():
        o_ref[...] = _rms(xp_ref[...], g_ref[...]).astype(BF16)

    @pl.when(i >= n_prompt_blocks)
    def _():
        o_ref[...] = _rms(xs_ref[...], g_ref[...]).astype(BF16)


def _norm(xp, xs, g, tr=1024):
    mp, ms = xp.shape[0], xs.shape[0]
    npb, nsb = mp // tr, ms // tr
    return pl.pallas_call(
        functools.partial(_norm_kernel, npb),
        grid=(npb + nsb,),
        in_specs=[
            pl.BlockSpec((tr, D_MODEL), lambda i: (jnp.minimum(i, npb - 1), 0)),
            pl.BlockSpec((tr, D_MODEL), lambda i: (jnp.maximum(i - npb, 0), 0)),
            pl.BlockSpec((1, D_MODEL), lambda i: (0, 0)),
        ],
        out_specs=pl.BlockSpec((tr, D_MODEL), lambda i: (i, 0)),
        out_shape=jax.ShapeDtypeStruct((mp + ms, D_MODEL), BF16),
        compiler_params=_params(("arbitrary",)),
        name="norm",
    )(xp, xs, g)


def _rope(h, cos, sa, sb):
    return h * cos + pltpu.roll(h, ROT_HALF, 1) * sa + pltpu.roll(h, LANES - ROT_HALF, 1) * sb


GELU_C1 = 0.7978845608028654
GELU_C3 = 0.035677408136300125


def _gelu_tanh(x):
    return (0.5 * x) * (1.0 + jnp.tanh(x * (GELU_C1 + GELU_C3 * (x * x))))


def _proj_uv_kernel(counts, xn_ref, w_ref, *refs):
    n = len(counts)
    riders, uv_ref, cast = refs[:n], refs[n], refs[n + 1:]
    _rider_cast(pl.program_id(0) * pl.num_programs(1) + pl.program_id(1), counts, riders, cast)
    h = jnp.dot(xn_ref[...], w_ref[...], preferred_element_type=F32)
    uv_ref[...] = _gelu_tanh(h).astype(uv_ref.dtype)


def _proj_uv(xn, w_in, riders, tm=1024, tn=1024):
    m = xn.shape[0]
    nj = (2 * D_MODEL) // tn
    counts, r_in, r_out, r_shapes = _rider_specs(riders, (m // tm) * nj, lambda i, j: i * nj + j)
    return pl.pallas_call(
        functools.partial(_proj_uv_kernel, counts),
        grid=(m // tm, nj),
        in_specs=[
            pl.BlockSpec((tm, D_MODEL), lambda i, j: (i, 0)),
            pl.BlockSpec((D_MODEL, tn), lambda i, j: (0, j)),
            *r_in,
        ],
        out_specs=(pl.BlockSpec((tm, tn), lambda i, j: (i, j)), *r_out),
        out_shape=(jax.ShapeDtypeStruct((m, 2 * D_MODEL), BF16), *r_shapes),
        compiler_params=_params(("arbitrary", "arbitrary")),
        name="proj_uv",
    )(xn, w_in, *riders)


def _proj_qkv_kernel(xn_ref, wq_ref, wkv_ref, cos_ref, sa_ref, sb_ref, q_ref, k_ref, va_ref):
    j = pl.program_id(1)
    xn = xn_ref[...]

    def rope_cols(h, scale):
        cos, sa, sb = cos_ref[...], sa_ref[...], sb_ref[...]
        parts = [_rope(h[:, c:c + LANES], cos, sa, sb) * scale for c in range(0, h.shape[1], LANES)]
        return jnp.concatenate(parts, axis=1)

    @pl.when(j == 0)
    def _():
        h = jnp.dot(xn, wkv_ref[...], preferred_element_type=F32)
        k_ref[...] = rope_cols(h[:, :KV_WIDTH], 1.0)
        va_ref[...] = h[:, KV_WIDTH:]

    @pl.when(j > 0)
    def _():
        h = jnp.dot(xn, wq_ref[...], preferred_element_type=F32)
        q_ref[...] = rope_cols(h, Q_SCALE).astype(q_ref.dtype)


def _proj_qkv(xn, w_in, cos_t, sa_t, sb_t, n_prompt_rows, seq, tm=1024, tn=Q_WIDTH):
    m = xn.shape[0]
    nq = Q_WIDTH // tn
    q_block0 = (2 * D_MODEL) // tn
    kv_block = (2 * D_MODEL + Q_WIDTH) // (2 * KV_WIDTH)
    npb = n_prompt_rows // tm
    blocks_per_seq = seq // tm

    def tab_idx(i, j):
        return (jnp.where(i < npb, i % blocks_per_seq, blocks_per_seq), 0)

    tab_spec = pl.BlockSpec((tm, LANES), tab_idx)
    qcol = lambda j: jnp.maximum(j - 1, 0)
    return pl.pallas_call(
        _proj_qkv_kernel,
        grid=(m // tm, nq + 1),
        in_specs=[
            pl.BlockSpec((tm, D_MODEL), lambda i, j: (i, 0)),
            pl.BlockSpec((D_MODEL, tn), lambda i, j: (0, q_block0 + qcol(j))),
            pl.BlockSpec((D_MODEL, 2 * KV_WIDTH), lambda i, j: (0, kv_block)),
            tab_spec, tab_spec, tab_spec,
        ],
        out_specs=(
            pl.BlockSpec((tm, tn), lambda i, j: (i, qcol(j))),
            pl.BlockSpec((tm, KV_WIDTH), lambda i, j: (i, 0)),
            pl.BlockSpec((tm, KV_WIDTH), lambda i, j: (i, 0)),
        ),
        out_shape=(
            jax.ShapeDtypeStruct((m, Q_WIDTH), BF16),
            jax.ShapeDtypeStruct((m, KV_WIDTH), F32),
            jax.ShapeDtypeStruct((m, KV_WIDTH), F32),
        ),
        compiler_params=_params(("arbitrary", "arbitrary")),
        name="proj_qkv",
    )(xn, w_in, w_in, cos_t, sa_t, sb_t)


def _proj_gate_kernel(counts, xn_ref, w_ref, b_ref, *refs):
    n = len(counts)
    riders, g_ref, cast = refs[:n], refs[n], refs[n + 1:]
    h = jnp.dot(xn_ref[...], w_ref[...].astype(BF16), preferred_element_type=F32)
    g_ref[...] = jax.nn.sigmoid(h + b_ref[...]).astype(g_ref.dtype)
    _rider_cast(pl.program_id(0) * pl.num_programs(1) + pl.program_id(1), counts, riders, cast)


def _proj_gate(xn, w_gate, b_gate, riders, tm=1536, tn=1024):
    m = xn.shape[0]
    n = w_gate.shape[1]
    nj = n // tn
    counts, r_in, r_out, r_shapes = _rider_specs(riders, (m // tm) * nj, lambda i, j: i * nj + j)
    return pl.pallas_call(
        functools.partial(_proj_gate_kernel, counts),
        grid=(m // tm, nj),
        in_specs=[
            pl.BlockSpec((tm, D_MODEL), lambda i, j: (i, 0)),
            pl.BlockSpec((D_MODEL, tn), lambda i, j: (0, j)),
            pl.BlockSpec((1, tn), lambda i, j: (0, j)),
            *r_in,
        ],
        out_specs=(pl.BlockSpec((tm, tn), lambda i, j: (i, j)), *r_out),
        out_shape=(jax.ShapeDtypeStruct((m, n), BF16), *r_shapes),
        compiler_params=_params(("arbitrary", "arbitrary")),
        name="proj_gate",
    )(xn, w_gate, b_gate, *riders)


def _gmlp_kernel(n_prompt_blocks, tr, u_ref, v_ref, lng_ref, lnb_ref, ws_ref, bst_ref,
                 a_ref, vn_ref):
    i = pl.program_id(0)
    v = v_ref[...].astype(F32)
    mu = jnp.mean(v, axis=-1, keepdims=True)
    vc = v - mu
    var = jnp.mean(vc * vc, axis=-1, keepdims=True)
    vn = vc * lax.rsqrt(var + EPS) * lng_ref[...] + lnb_ref[...]
    vnb = vn.astype(BF16)

    def mix(length):
        r = lax.broadcasted_iota(jnp.int32, (length, length), 0) // CHUNK
        c = lax.broadcasted_iota(jnp.int32, (length, length), 1) // CHUNK
        for g in range(GMLP_GROUPS):
            w = jnp.where(c <= r, ws_ref[g, :length, :length], 0.0).astype(BF16)
            bias = bst_ref[:length, g:g + 1]
            cols = slice(g * GMLP_GROUP_DIM, (g + 1) * GMLP_GROUP_DIM)
            for b in range(tr // length):
                rows = slice(b * length, (b + 1) * length)
                s = jnp.dot(w, vnb[rows, cols], preferred_element_type=F32) + bias
                a_ref[rows, cols] = (u_ref[rows, cols].astype(F32) * s).astype(a_ref.dtype)

    @pl.when(i < n_prompt_blocks)
    def _():
        mix(GMLP_CHUNK)

    @pl.when(i >= n_prompt_blocks)
    def _():
        vn_ref[...] = vn
        mix(CHUNK)


def _gmlp(uv, ln_g, ln_b, ws, bs_t, n_prompt_rows, tr=1024):
    m = uv.shape[0]
    npb = n_prompt_rows // tr
    return pl.pallas_call(
        functools.partial(_gmlp_kernel, npb, tr),
        grid=(m // tr,),
        in_specs=[
            pl.BlockSpec((tr, D_MODEL), lambda i: (i, 0)),
            pl.BlockSpec((tr, D_MODEL), lambda i: (i, 1)),
            pl.BlockSpec((1, D_MODEL), lambda i: (0, 0)),
            pl.BlockSpec((1, D_MODEL), lambda i: (0, 0)),
            pl.BlockSpec((GMLP_GROUPS, GMLP_CHUNK, GMLP_CHUNK), lambda i: (0, 0, 0)),
            pl.BlockSpec((GMLP_CHUNK, GMLP_GROUPS), lambda i: (0, 0)),
        ],
        out_specs=(
            pl.BlockSpec((tr, D_MODEL), lambda i: (i, 0)),
            pl.BlockSpec((tr, D_MODEL), lambda i: (jnp.maximum(i - npb, 0), 0)),
        ),
        out_shape=(
            jax.ShapeDtypeStruct((m, D_MODEL), BF16),
            jax.ShapeDtypeStruct((m - n_prompt_rows, D_MODEL), F32),
        ),
        compiler_params=_params(("arbitrary",)),
        name="gmlp",
    )(uv, uv, ln_g, ln_b, ws, bs_t)


def _dup_head(pair, pair_swapped, low_half, odd):
    if odd:
        return jnp.where(low_half, pair_swapped, pair).astype(BF16)
    return jnp.where(low_half, pair, pair_swapped).astype(BF16)


def _attend(q_ref, kwin, vwin, mask, sink_ref, o_ref):
    rows, keys = q_ref.shape[0], kwin.shape[0]
    ones = jnp.ones((keys, LANES), BF16)
    low_q = lax.broadcasted_iota(jnp.int32, (rows, LANES), 1) < HEAD_DIM
    low_k = lax.broadcasted_iota(jnp.int32, (keys, LANES), 1) < HEAD_DIM
    heads_per_pair = LANES // HEAD_DIM
    for c in range(N_KV_HEADS // heads_per_pair):
        kpair = kwin[:, c * LANES:(c + 1) * LANES]
        vpair = vwin[:, c * LANES:(c + 1) * LANES]
        kswap = pltpu.roll(kpair, HEAD_DIM, 1)
        vswap = pltpu.roll(vpair, HEAD_DIM, 1)
        for odd in range(heads_per_pair):
            hk = c * heads_per_pair + odd
            kdup = _dup_head(kpair, kswap, low_k, odd)
            vaug = jnp.concatenate([_dup_head(vpair, vswap, low_k, odd), ones], axis=1)
            parts = []
            for r in range(Q_REP):
                h = hk * Q_REP + r
                qpair = q_ref[:, (h // 2) * LANES:(h // 2 + 1) * LANES]
                keep = low_q if h % 2 == 0 else jnp.logical_not(low_q)
                parts.append(jnp.where(keep, qpair, jnp.zeros_like(qpair)))
            s_all = lax.dot_general(jnp.concatenate(parts, axis=0), kdup, (((1,), (1,)), ((), ())),
                                    preferred_element_type=F32)
            ps, sinks, mxs = [], [], []
            for r in range(Q_REP):
                s = s_all[r * rows:(r + 1) * rows]
                if mask is not None:
                    s = mask(s)
                sink = sink_ref[hk * Q_REP + r] * LOG2E
                mx = jnp.maximum(jnp.max(s, axis=-1, keepdims=True), sink)
                ps.append(jnp.exp2(s - mx).astype(BF16))
                sinks.append(sink)
                mxs.append(mx)
            o_all = jnp.dot(jnp.concatenate(ps, axis=0), vaug, preferred_element_type=F32)
            for r in range(0, Q_REP, 2):
                h = hk * Q_REP + r
                even, oddh = o_all[r * rows:(r + 1) * rows], o_all[(r + 1) * rows:(r + 2) * rows]
                esink = jnp.exp2(jnp.where(low_q, sinks[r], sinks[r + 1])
                                 - jnp.where(low_q, mxs[r], mxs[r + 1]))
                den = jnp.where(low_q, even[:, LANES:], oddh[:, LANES:]) + esink
                num = jnp.where(low_q, even[:, :LANES], oddh[:, :LANES])
                o_ref[:, (h // 2) * LANES:(h // 2 + 1) * LANES] = (num / den).astype(o_ref.dtype)


def _attn_kernel(tq, n_prompt_tiles, tiles_per_seq, dec_seq, counts, sink_ref, q_ref, kp_ref, kc_ref,
                 vp_ref, vc_ref, ck_ref, cv_ref, *refs):
    n = len(counts)
    riders, o_ref, cast = refs[:n], refs[n], refs[n + 1:]
    s = pl.program_id(0)
    _rider_cast(s, counts, riders, cast)

    @pl.when(s < n_prompt_tiles)
    def _():
        t = s % tiles_per_seq
        kwin = jnp.concatenate([kp_ref[...], kc_ref[...]], axis=0)
        vwin = jnp.concatenate([vp_ref[...], vc_ref[...]], axis=0)
        row = lax.broadcasted_iota(jnp.int32, (tq, tq), 0)
        col = lax.broadcasted_iota(jnp.int32, (tq, tq), 1)
        valid_prev = ((row < CHUNK) | (col >= CHUNK)) & (t > 0)
        valid_cur_top = lax.broadcasted_iota(jnp.int32, (CHUNK, tq), 1) < CHUNK

        def mask(sc):
            prev = jnp.where(valid_prev, sc[:, :tq], NEG)
            cur_top = jnp.where(valid_cur_top, sc[:CHUNK, tq:], NEG)
            cur = jnp.concatenate([cur_top, sc[CHUNK:, tq:]], axis=0)
            return jnp.concatenate([prev, cur], axis=1)

        _attend(q_ref, kwin, vwin, mask, sink_ref, o_ref)

    @pl.when(s >= n_prompt_tiles)
    def _():
        kn, vn = kc_ref[...], vc_ref[...]
        for b in range(tq // dec_seq):
            rows = pl.ds(b * dec_seq, dec_seq)
            new = slice(b * dec_seq, (b + 1) * dec_seq)
            kwin = jnp.concatenate([ck_ref[b], kn[new]], axis=0)
            vwin = jnp.concatenate([cv_ref[b], vn[new]], axis=0)
            _attend(q_ref.at[rows], kwin, vwin, None, sink_ref, o_ref.at[rows])


def _attn(sinks, q, k, va, cache_k, cache_v, batch, seq, dec_batch, dec_seq, riders, tq=128):
    m = q.shape[0]
    assert tq == WINDOW and tq % dec_seq == 0
    tiles_per_seq = seq // tq
    n_prompt_tiles = batch * tiles_per_seq
    per_step = tq // dec_seq
    n_steps = n_prompt_tiles + dec_batch // per_step
    cur = lambda s: (s, 0)
    prev = lambda s: (jnp.where(s % tiles_per_seq == 0, s, s - 1), 0)
    cached = lambda s: (jnp.maximum(s - n_prompt_tiles, 0), 0, 0)
    cache_len = cache_k.shape[1]
    counts, r_in, r_out, r_shapes = _rider_specs(riders, n_steps, lambda s: s)
    return pl.pallas_call(
        functools.partial(_attn_kernel, tq, n_prompt_tiles, tiles_per_seq, dec_seq, counts),
        grid=(n_steps,),
        in_specs=[
            pl.BlockSpec(memory_space=pltpu.SMEM),
            pl.BlockSpec((tq, Q_WIDTH), cur),
            pl.BlockSpec((tq, KV_WIDTH), prev),
            pl.BlockSpec((tq, KV_WIDTH), cur),
            pl.BlockSpec((tq, KV_WIDTH), prev),
            pl.BlockSpec((tq, KV_WIDTH), cur),
            pl.BlockSpec((per_step, cache_len, KV_WIDTH), cached),
            pl.BlockSpec((per_step, cache_len, KV_WIDTH), cached),
            *r_in,
        ],
        out_specs=(pl.BlockSpec((tq, Q_WIDTH), cur), *r_out),
        out_shape=(jax.ShapeDtypeStruct((m, Q_WIDTH), BF16), *r_shapes),
        compiler_params=_params(("arbitrary",)),
        name="attn",
    )(sinks, q, k, k, va, va, cache_k, cache_v, *riders)


def _branch_kernel(counts, a_ref, o_ref, wa_ref, wb_ref, ga_ref, gb_ref, *refs):
    n = len(counts)
    riders, t_ref, cast = refs[:n], refs[n], refs[n + 1:]
    _rider_cast(pl.program_id(0) * pl.num_programs(1) + pl.program_id(1), counts, riders, cast)
    ya = jnp.dot(a_ref[...], wa_ref[...], preferred_element_type=F32)
    yb = jnp.dot(o_ref[...], wb_ref[...], preferred_element_type=F32)
    t = ga_ref[...].astype(F32) * ya + gb_ref[...].astype(F32) * yb
    t_ref[...] = t.astype(t_ref.dtype)


def _branch(a, o, wa, wb, g, riders, tm=1536, tn=512):
    m = a.shape[0]
    nj = D_MODEL // tn
    counts, r_in, r_out, r_shapes = _rider_specs(riders, (m // tm) * nj, lambda i, j: i * nj + j)
    return pl.pallas_call(
        functools.partial(_branch_kernel, counts),
        grid=(m // tm, nj),
        in_specs=[
            pl.BlockSpec((tm, D_MODEL), lambda i, j: (i, 0)),
            pl.BlockSpec((tm, Q_WIDTH), lambda i, j: (i, 0)),
            pl.BlockSpec((D_MODEL, tn), lambda i, j: (0, j)),
            pl.BlockSpec((Q_WIDTH, tn), lambda i, j: (0, j)),
            pl.BlockSpec((tm, tn), lambda i, j: (i, j)),
            pl.BlockSpec((tm, tn), lambda i, j: (i, j + nj)),
            *r_in,
        ],
        out_specs=(pl.BlockSpec((tm, tn), lambda i, j: (i, j)), *r_out),
        out_shape=(jax.ShapeDtypeStruct((m, D_MODEL), BF16), *r_shapes),
        compiler_params=_params(("arbitrary", "arbitrary")),
        name="branch",
    )(a, o, wa, wb, g, g, *riders)


def _out_kernel(n_prompt_blocks, t_ref, w_ref, xp_ref, xs_ref, x1_ref):
    i = pl.program_id(0)
    y = jnp.dot(t_ref[...], w_ref[...], preferred_element_type=F32)

    @pl.when(i < n_prompt_blocks)
    def _():
        x1_ref[...] = xp_ref[...] + y

    @pl.when(i >= n_prompt_blocks)
    def _():
        x1_ref[...] = xs_ref[...] + y


def _out(t, w_out, xp, xs, tm=512, tn=D_MODEL):
    m = t.shape[0]
    npb = xp.shape[0] // tm
    return pl.pallas_call(
        functools.partial(_out_kernel, npb),
        grid=(m // tm, D_MODEL // tn),
        in_specs=[
            pl.BlockSpec((tm, D_MODEL), lambda i, j: (i, 0)),
            pl.BlockSpec((D_MODEL, tn), lambda i, j: (0, j)),
            pl.BlockSpec((tm, tn), lambda i, j: (jnp.minimum(i, npb - 1), j)),
            pl.BlockSpec((tm, tn), lambda i, j: (jnp.maximum(i - npb, 0), j)),
        ],
        out_specs=pl.BlockSpec((tm, tn), lambda i, j: (i, j)),
        out_shape=jax.ShapeDtypeStruct((m, D_MODEL), F32),
        compiler_params=_params(("arbitrary", "arbitrary")),
        name="out_proj",
    )(t, w_out, xp, xs)


def _ffn_kernel(x1_ref, ng_ref, wg_ref, wu_ref, wd_ref, fg_ref, y_ref, h_ref):
    f = pl.program_id(1)

    @pl.when(f == 0)
    def _():
        x1 = x1_ref[...]
        h_ref[...] = _rms(x1, ng_ref[...]).astype(h_ref.dtype)
        y_ref[...] = x1

    h = h_ref[...]
    gate = jnp.dot(h, wg_ref[...], preferred_element_type=F32)
    up = jnp.dot(h, wu_ref[...], preferred_element_type=F32)
    hid = (jax.nn.silu(gate) * up).astype(BF16)
    y_ref[...] += jnp.dot(hid, wd_ref[...], preferred_element_type=F32)

    @pl.when(f == pl.num_programs(1) - 1)
    def _():
        y_ref[...] = _rms(y_ref[...], fg_ref[...])


def _ffn(x1, row_block_offset, n_row_blocks, norm_g, wg, wu, wd, final_g, tm=512, tf=512):
    d_ff = wg.shape[1]
    return pl.pallas_call(
        _ffn_kernel,
        grid=(n_row_blocks, d_ff // tf),
        in_specs=[
            pl.BlockSpec((tm, D_MODEL), lambda i, f: (i + row_block_offset, 0)),
            pl.BlockSpec((1, D_MODEL), lambda i, f: (0, 0)),
            pl.BlockSpec((D_MODEL, tf), lambda i, f: (0, f)),
            pl.BlockSpec((D_MODEL, tf), lambda i, f: (0, f)),
            pl.BlockSpec((tf, D_MODEL), lambda i, f: (f, 0)),
            pl.BlockSpec((1, D_MODEL), lambda i, f: (0, 0)),
        ],
        out_specs=pl.BlockSpec((tm, D_MODEL), lambda i, f: (i, 0)),
        out_shape=jax.ShapeDtypeStruct((n_row_blocks * tm, D_MODEL), F32),
        scratch_shapes=[pltpu.VMEM((tm, D_MODEL), BF16)],
        compiler_params=_params(("arbitrary", "arbitrary")),
        name="ffn",
    )(x1, norm_g, wg, wu, wd, final_g)


def _rope_tables(seq, past_len, dec_batch, dec_seq):
    pos = jnp.concatenate([jnp.arange(seq, dtype=F32),
                           jnp.tile(past_len + jnp.arange(dec_seq, dtype=F32), dec_batch)])
    inv_freq = jnp.float32(ROPE_THETA) ** (-(jnp.arange(ROT_HALF, dtype=F32) * 2.0 / ROT_DIM))
    ang = pos[:, None] * inv_freq[None, :]
    cos, sin = jnp.cos(ang), jnp.sin(ang)
    n = pos.shape[0]
    pad = jnp.zeros((n, HEAD_DIM - ROT_DIM), F32)
    zero = jnp.zeros((n, ROT_HALF), F32)
    cos_h = jnp.concatenate([cos, cos, pad + 1.0], axis=1)
    sa_h = jnp.concatenate([zero, sin, pad], axis=1)
    sb_h = jnp.concatenate([-sin, zero, pad], axis=1)
    rep = LANES // HEAD_DIM
    return jnp.tile(cos_h, (1, rep)), jnp.tile(sa_h, (1, rep)), jnp.tile(sb_h, (1, rep))


def kernel(x_prompt, x_sample, cache_swa_k, cache_swa_v, norm_mix_g, w_in, gmlp_ln_g, gmlp_ln_b,
           gmlp_ws, gmlp_bs, attn_sinks, w_gate, b_gate, w_branch_a, w_branch_b, w_out,
           norm_ffn_g, w_ffn_gate, w_ffn_up, w_ffn_down, final_norm_g):
    batch, seq, _ = x_prompt.shape
    dec_batch, dec_seq, _ = x_sample.shape
    depth = w_in.shape[0]
    assert depth == 1 and cache_swa_k.shape[2] == WINDOW and dec_seq == CHUNK
    mp, ms = batch * seq, dec_batch * dec_seq

    xp = x_prompt.reshape(mp, D_MODEL)
    xs = x_sample.reshape(ms, D_MODEL)
    cos_t, sa_t, sb_t = _rope_tables(seq, PAST_LEN, dec_batch, dec_seq)
    row = lambda p: p.reshape(1, -1)

    l = 0
    xn = _norm(xp, xs, row(norm_mix_g[l]))
    g, w_in_b = _proj_gate(xn, w_gate[l], row(b_gate[l]), [w_in[l]])
    q, k, va = _proj_qkv(xn, w_in_b, cos_t, sa_t, sb_t, mp, seq)
    uv, wg, wu = _proj_uv(xn, w_in_b, [w_ffn_gate[l], w_ffn_up[l]])
    a, vn_s = _gmlp(uv, row(gmlp_ln_g[l]), row(gmlp_ln_b[l]), gmlp_ws[l], gmlp_bs[l].T, mp)
    o, wa_b, wb_b, wo_b = _attn(
        attn_sinks[l], q, k, va,
        cache_swa_k[l].reshape(dec_batch, WINDOW, KV_WIDTH),
        cache_swa_v[l].reshape(dec_batch, WINDOW, KV_WIDTH),
        batch, seq, dec_batch, dec_seq, [w_branch_a[l], w_branch_b[l], w_out[l]])
    t, wd = _branch(a, o, wa_b, wb_b, g, [w_ffn_down[l]])
    x1 = _out(t, wo_b, xp, xs)
    ffn_tm = 1024
    ffn = functools.partial(_ffn, norm_g=row(norm_ffn_g[l]), wg=wg, wu=wu, wd=wd,
                            final_g=row(final_norm_g), tm=ffn_tm)
    y_prompt = ffn(x1, 0, mp // ffn_tm)
    y_sample = ffn(x1, mp // ffn_tm, ms // ffn_tm)

    keep = min(WINDOW, seq)
    tail = lambda z: jnp.stack([z[(b + 1) * seq - keep:(b + 1) * seq] for b in range(batch)]).reshape(
        batch, keep, N_KV_HEADS, HEAD_DIM)
    kp, vp = tail(k), tail(va)
    return (
        y_prompt.reshape(batch, seq, D_MODEL),
        y_sample.reshape(dec_batch, dec_seq, D_MODEL),
        kp[None],
        vp[None],
        k[mp:].reshape(1, dec_batch, dec_seq, N_KV_HEADS, HEAD_DIM),
        va[mp:].reshape(1, dec_batch, dec_seq, N_KV_HEADS, HEAD_DIM),
        vn_s.reshape(1, dec_batch, dec_seq, D_MODEL),
    )
```

```python
import functools

import jax
import jax.numpy as jnp
from jax import lax
from jax.experimental import pallas as pl
from jax.experimental.pallas import tpu as pltpu

D_MODEL = 2048
CHUNK = 64
GMLP_CHUNK = 128
GMLP_GROUPS = 8
GMLP_GROUP_DIM = D_MODEL // GMLP_GROUPS
N_HEADS = 32
N_KV_HEADS = 4
HEAD_DIM = 64
Q_REP = N_HEADS // N_KV_HEADS
WINDOW = 128
PAST_LEN = 2048
ROPE_THETA = 500000.0
ROT_DIM = HEAD_DIM // 4
ROT_HALF = ROT_DIM // 2
Q_WIDTH = N_HEADS * HEAD_DIM
KV_WIDTH = N_KV_HEADS * HEAD_DIM
EPS = 1e-6
NEG = -1e30
LANES = 128
LOG2E = 1.4426950408889634
Q_SCALE = HEAD_DIM ** -0.5 * LOG2E

F32 = jnp.float32
BF16 = jnp.bfloat16

VMEM_LIMIT = 56 * 1024 * 1024


def _params(semantics):
    return pltpu.CompilerParams(dimension_semantics=semantics, vmem_limit_bytes=VMEM_LIMIT)


def _rms(x, g):
    return x * lax.rsqrt(jnp.mean(x * x, axis=-1, keepdims=True) + EPS) * g


BF16_SUBLANES = 16


def _rider_specs(weights, n_steps, step_of):
    counts, in_specs, out_specs, out_shapes = [], [], [], []
    for w in weights:
        rows, cols = w.shape
        nb = max(n for n in range(1, n_steps + 1)
                 if rows % n == 0 and (rows // n) % BF16_SUBLANES == 0)
        idx = lambda *g, nb=nb: (jnp.minimum(step_of(*g), nb - 1), 0)
        counts.append(nb)
        in_specs.append(pl.BlockSpec((rows // nb, cols), idx))
        out_specs.append(pl.BlockSpec((rows // nb, cols), idx))
        out_shapes.append(jax.ShapeDtypeStruct(w.shape, BF16))
    return counts, in_specs, out_specs, out_shapes


def _rider_cast(step, counts, src_refs, dst_refs):
    for nb, src, dst in zip(counts, src_refs, dst_refs):
        @pl.when(step < nb)
        def _():
            dst[...] = src[...].astype(dst.dtype)


def _norm_kernel(n_prompt_blocks, xp_ref, xs_ref, g_ref, o_ref):
    i = pl.program_id(0)

    @pl.when(i < n_prompt_blocks)
    def _():
        o_ref[...] = _rms(xp_ref[...], g_ref[...]).astype(BF16)

    @pl.when(i >= n_prompt_blocks)
    def _():
        o_ref[...] = _rms(xs_ref[...], g_ref[...]).astype(BF16)


def _norm(xp, xs, g, tr=1024):
    mp, ms = xp.shape[0], xs.shape[0]
    npb, nsb = mp // tr, ms // tr
    return pl.pallas_call(
        functools.partial(_norm_kernel, npb),
        grid=(npb + nsb,),
        in_specs=[
            pl.BlockSpec((tr, D_MODEL), lambda i: (jnp.minimum(i, npb - 1), 0)),
            pl.BlockSpec((tr, D_MODEL), lambda i: (jnp.maximum(i - npb, 0), 0)),
            pl.BlockSpec((1, D_MODEL), lambda i: (0, 0)),
        ],
        out_specs=pl.BlockSpec((tr, D_MODEL), lambda i: (i, 0)),
        out_shape=jax.ShapeDtypeStruct((mp + ms, D_MODEL), BF16),
        compiler_params=_params(("arbitrary",)),
        name="norm",
    )(xp, xs, g)


def _rope(h, cos, sa, sb):
    return h * cos + pltpu.roll(h, ROT_HALF, 1) * sa + pltpu.roll(h, LANES - ROT_HALF, 1) * sb


GELU_C1 = 0.7978845608028654
GELU_C3 = 0.035677408136300125


def _gelu_tanh(x):
    return (0.5 * x) * (1.0 + jnp.tanh(x * (GELU_C1 + GELU_C3 * (x * x))))


def _proj_uv_kernel(counts, xn_ref, w_ref, *refs):
    n = len(counts)
    riders, uv_ref, cast = refs[:n], refs[n], refs[n + 1:]
    _rider_cast(pl.program_id(0) * pl.num_programs(1) + pl.program_id(1), counts, riders, cast)
    h = jnp.dot(xn_ref[...], w_ref[...], preferred_element_type=F32)
    uv_ref[...] = _gelu_tanh(h).astype(uv_ref.dtype)


def _proj_uv(xn, w_in, riders, tm=1024, tn=1024):
    m = xn.shape[0]
    nj = (2 * D_MODEL) // tn
    counts, r_in, r_out, r_shapes = _rider_specs(riders, (m // tm) * nj, lambda i, j: i * nj + j)
    return pl.pallas_call(
        functools.partial(_proj_uv_kernel, counts),
        grid=(m // tm, nj),
        in_specs=[
            pl.BlockSpec((tm, D_MODEL), lambda i, j: (i, 0)),
            pl.BlockSpec((D_MODEL, tn), lambda i, j: (0, j)),
            *r_in,
        ],
        out_specs=(pl.BlockSpec((tm, tn), lambda i, j: (i, j)), *r_out),
        out_shape=(jax.ShapeDtypeStruct((m, 2 * D_MODEL), BF16), *r_shapes),
        compiler_params=_params(("arbitrary", "arbitrary")),
        name="proj_uv",
    )(xn, w_in, *riders)


def _proj_qkv_kernel(xn_ref, wq_ref, wkv_ref, cos_ref, sa_ref, sb_ref, q_ref, k_ref, va_ref):
    j = pl.program_id(1)
    xn = xn_ref[...]

    def rope_cols(h, scale):
        cos, sa, sb = cos_ref[...], sa_ref[...], sb_ref[...]
        parts = [_rope(h[:, c:c + LANES], cos, sa, sb) * scale for c in range(0, h.shape[1], LANES)]
        return jnp.concatenate(parts, axis=1)

    @pl.when(j == 0)
    def _():
        h = jnp.dot(xn, wkv_ref[...], preferred_element_type=F32)
        k_ref[...] = rope_cols(h[:, :KV_WIDTH], 1.0)
        va_ref[...] = h[:, KV_WIDTH:]

    @pl.when(j > 0)
    def _():
        h = jnp.dot(xn, wq_ref[...], preferred_element_type=F32)
        q_ref[...] = rope_cols(h, Q_SCALE).astype(q_ref.dtype)


def _proj_qkv(xn, w_in, cos_t, sa_t, sb_t, n_prompt_rows, seq, tm=1024, tn=Q_WIDTH):
    m = xn.shape[0]
    nq = Q_WIDTH // tn
    q_block0 = (2 * D_MODEL) // tn
    kv_block = (2 * D_MODEL + Q_WIDTH) // (2 * KV_WIDTH)
    npb = n_prompt_rows // tm
    blocks_per_seq = seq // tm

    def tab_idx(i, j):
        return (jnp.where(i < npb, i % blocks_per_seq, blocks_per_seq), 0)

    tab_spec = pl.BlockSpec((tm, LANES), tab_idx)
    qcol = lambda j: jnp.maximum(j - 1, 0)
    return pl.pallas_call(
        _proj_qkv_kernel,
        grid=(m // tm, nq + 1),
        in_specs=[
            pl.BlockSpec((tm, D_MODEL), lambda i, j: (i, 0)),
            pl.BlockSpec((D_MODEL, tn), lambda i, j: (0, q_block0 + qcol(j))),
            pl.BlockSpec((D_MODEL, 2 * KV_WIDTH), lambda i, j: (0, kv_block)),
            tab_spec, tab_spec, tab_spec,
        ],
        out_specs=(
            pl.BlockSpec((tm, tn), lambda i, j: (i, qcol(j))),
            pl.BlockSpec((tm, KV_WIDTH), lambda i, j: (i, 0)),
            pl.BlockSpec((tm, KV_WIDTH), lambda i, j: (i, 0)),
        ),
        out_shape=(
            jax.ShapeDtypeStruct((m, Q_WIDTH), BF16),
            jax.ShapeDtypeStruct((m, KV_WIDTH), F32),
            jax.ShapeDtypeStruct((m, KV_WIDTH), F32),
        ),
        compiler_params=_params(("arbitrary", "arbitrary")),
        name="proj_qkv",
    )(xn, w_in, w_in, cos_t, sa_t, sb_t)


def _proj_gate_kernel(counts, xn_ref, w_ref, b_ref, *refs):
    n = len(counts)
    riders, g_ref, cast = refs[:n], refs[n], refs[n + 1:]
    h = jnp.dot(xn_ref[...], w_ref[...].astype(BF16), preferred_element_type=F32)
    g_ref[...] = jax.nn.sigmoid(h + b_ref[...]).astype(g_ref.dtype)
    _rider_cast(pl.program_id(0) * pl.num_programs(1) + pl.program_id(1), counts, riders, cast)


def _proj_gate(xn, w_gate, b_gate, riders, tm=1536, tn=1024):
    m = xn.shape[0]
    n = w_gate.shape[1]
    nj = n // tn
    counts, r_in, r_out, r_shapes = _rider_specs(riders, (m // tm) * nj, lambda i, j: i * nj + j)
    return pl.pallas_call(
        functools.partial(_proj_gate_kernel, counts),
        grid=(m // tm, nj),
        in_specs=[
            pl.BlockSpec((tm, D_MODEL), lambda i, j: (i, 0)),
            pl.BlockSpec((D_MODEL, tn), lambda i, j: (0, j)),
            pl.BlockSpec((1, tn), lambda i, j: (0, j)),
            *r_in,
        ],
        out_specs=(pl.BlockSpec((tm, tn), lambda i, j: (i, j)), *r_out),
        out_shape=(jax.ShapeDtypeStruct((m, n), BF16), *r_shapes),
        compiler_params=_params(("arbitrary", "arbitrary")),
        name="proj_gate",
    )(xn, w_gate, b_gate, *riders)


def _gmlp_kernel(n_prompt_blocks, tr, u_ref, v_ref, lng_ref, lnb_ref, ws_ref, bst_ref,
                 a_ref, vn_ref):
    i = pl.program_id(0)
    v = v_ref[...].astype(F32)
    mu = jnp.mean(v, axis=-1, keepdims=True)
    vc = v - mu
    var = jnp.mean(vc * vc, axis=-1, keepdims=True)
    vn = vc * lax.rsqrt(var + EPS) * lng_ref[...] + lnb_ref[...]
    vnb = vn.astype(BF16)

    def mix(length):
        r = lax.broadcasted_iota(jnp.int32, (length, length), 0) // CHUNK
        c = lax.broadcasted_iota(jnp.int32, (length, length), 1) // CHUNK
        for g in range(GMLP_GROUPS):
            w = jnp.where(c <= r, ws_ref[g, :length, :length], 0.0).astype(BF16)
            bias = bst_ref[:length, g:g + 1]
            cols = slice(g * GMLP_GROUP_DIM, (g + 1) * GMLP_GROUP_DIM)
            for b in range(tr // length):
                rows = slice(b * length, (b + 1) * length)
                s = jnp.dot(w, vnb[rows, cols], preferred_element_type=F32) + bias
                a_ref[rows, cols] = (u_ref[rows, cols].astype(F32) * s).astype(a_ref.dtype)

    @pl.when(i < n_prompt_blocks)
    def _():
        mix(GMLP_CHUNK)

    @pl.when(i >= n_prompt_blocks)
    def _():
        vn_ref[...] = vn
        mix(CHUNK)


def _gmlp(uv, ln_g, ln_b, ws, bs_t, n_prompt_rows, tr=1024):
    m = uv.shape[0]
    npb = n_prompt_rows // tr
    return pl.pallas_call(
        functools.partial(_gmlp_kernel, npb, tr),
        grid=(m // tr,),
        in_specs=[
            pl.BlockSpec((tr, D_MODEL), lambda i: (i, 0)),
            pl.BlockSpec((tr, D_MODEL), lambda i: (i, 1)),
            pl.BlockSpec((1, D_MODEL), lambda i: (0, 0)),
            pl.BlockSpec((1, D_MODEL), lambda i: (0, 0)),
            pl.BlockSpec((GMLP_GROUPS, GMLP_CHUNK, GMLP_CHUNK), lambda i: (0, 0, 0)),
            pl.BlockSpec((GMLP_CHUNK, GMLP_GROUPS), lambda i: (0, 0)),
        ],
        out_specs=(
            pl.BlockSpec((tr, D_MODEL), lambda i: (i, 0)),
            pl.BlockSpec((tr, D_MODEL), lambda i: (jnp.maximum(i - npb, 0), 0)),
        ),
        out_shape=(
            jax.ShapeDtypeStruct((m, D_MODEL), BF16),
            jax.ShapeDtypeStruct((m - n_prompt_rows, D_MODEL), F32),
        ),
        compiler_params=_params(("arbitrary",)),
        name="gmlp",
    )(uv, uv, ln_g, ln_b, ws, bs_t)


def _dup_head(pair, pair_swapped, low_half, odd):
    if odd:
        return jnp.where(low_half, pair_swapped, pair).astype(BF16)
    return jnp.where(low_half, pair, pair_swapped).astype(BF16)


def _attend(q_ref, kwin, vwin, mask, sink_ref, o_ref):
    rows, keys = q_ref.shape[0], kwin.shape[0]
    ones = jnp.ones((keys, LANES), BF16)
    low_q = lax.broadcasted_iota(jnp.int32, (rows, LANES), 1) < HEAD_DIM
    low_k = lax.broadcasted_iota(jnp.int32, (keys, LANES), 1) < HEAD_DIM
    heads_per_pair = LANES // HEAD_DIM
    for c in range(N_KV_HEADS // heads_per_pair):
        kpair = kwin[:, c * LANES:(c + 1) * LANES]
        vpair = vwin[:, c * LANES:(c + 1) * LANES]
        kswap = pltpu.roll(kpair, HEAD_DIM, 1)
        vswap = pltpu.roll(vpair, HEAD_DIM, 1)
        for odd in range(heads_per_pair):
            hk = c * heads_per_pair + odd
            kdup = _dup_head(kpair, kswap, low_k, odd)
            vaug = jnp.concatenate([_dup_head(vpair, vswap, low_k, odd), ones], axis=1)
            parts = []
            for r in range(Q_REP):
                h = hk * Q_REP + r
                qpair = q_ref[:, (h // 2) * LANES:(h // 2 + 1) * LANES]
                keep = low_q if h % 2 == 0 else jnp.logical_not(low_q)
                parts.append(jnp.where(keep, qpair, jnp.zeros_like(qpair)))
            s_all = lax.dot_general(jnp.concatenate(parts, axis=0), kdup, (((1,), (1,)), ((), ())),
                                    preferred_element_type=F32)
            ps, sinks, mxs = [], [], []
            for r in range(Q_REP):
                s = s_all[r * rows:(r + 1) * rows]
                if mask is not None:
                    s = mask(s)
                sink = sink_ref[hk * Q_REP + r] * LOG2E
                mx = jnp.maximum(jnp.max(s, axis=-1, keepdims=True), sink)
                ps.append(jnp.exp2(s - mx).astype(BF16))
                sinks.append(sink)
                mxs.append(mx)
            o_all = jnp.dot(jnp.concatenate(ps, axis=0), vaug, preferred_element_type=F32)
            for r in range(0, Q_REP, 2):
                h = hk * Q_REP + r
                even, oddh = o_all[r * rows:(r + 1) * rows], o_all[(r + 1) * rows:(r + 2) * rows]
                esink = jnp.exp2(jnp.where(low_q, sinks[r], sinks[r + 1])
                                 - jnp.where(low_q, mxs[r], mxs[r + 1]))
                den = jnp.where(low_q, even[:, LANES:], oddh[:, LANES:]) + esink
                num = jnp.where(low_q, even[:, :LANES], oddh[:, :LANES])
                o_ref[:, (h // 2) * LANES:(h // 2 + 1) * LANES] = (num / den).astype(o_ref.dtype)


def _band_mask(t, tq):
    row = lax.broadcasted_iota(jnp.int32, (tq, tq), 0)
    col = lax.broadcasted_iota(jnp.int32, (tq, tq), 1)
    valid_prev = ((row < CHUNK) | (col >= CHUNK)) & (t > 0)
    valid_cur_top = lax.broadcasted_iota(jnp.int32, (CHUNK, tq), 1) < CHUNK

    def mask(sc):
        prev = jnp.where(valid_prev, sc[:, :tq], NEG)
        cur_top = jnp.where(valid_cur_top, sc[:CHUNK, tq:], NEG)
        cur = jnp.concatenate([cur_top, sc[CHUNK:, tq:]], axis=0)
        return jnp.concatenate([prev, cur], axis=1)

    return mask


def _attn_kernel(tq, pair_steps, tiles_per_seq, dec_seq, counts, sink_ref, q_ref, kp_ref, kc_ref,
                 vp_ref, vc_ref, ck_ref, cv_ref, *refs):
    n = len(counts)
    riders, o_ref, cast = refs[:n], refs[n], refs[n + 1:]
    s = pl.program_id(0)
    _rider_cast(s, counts, riders, cast)
    kc, vc = kc_ref[...], vc_ref[...]

    @pl.when(s < pair_steps)
    def _():
        t0 = (2 * s) % tiles_per_seq
        first, second = pl.ds(0, tq), pl.ds(tq, tq)
        _attend(q_ref.at[first], jnp.concatenate([kp_ref[...], kc[:tq]], axis=0),
                jnp.concatenate([vp_ref[...], vc[:tq]], axis=0), _band_mask(t0, tq), sink_ref,
                o_ref.at[first])
        _attend(q_ref.at[second], kc, vc, _band_mask(t0 + 1, tq), sink_ref, o_ref.at[second])

    @pl.when(s >= pair_steps)
    def _():
        for b in range(2 * tq // dec_seq):
            rows = pl.ds(b * dec_seq, dec_seq)
            new = slice(b * dec_seq, (b + 1) * dec_seq)
            kwin = jnp.concatenate([ck_ref[b], kc[new]], axis=0)
            vwin = jnp.concatenate([cv_ref[b], vc[new]], axis=0)
            _attend(q_ref.at[rows], kwin, vwin, None, sink_ref, o_ref.at[rows])


def _attn(sinks, q, k, va, cache_k, cache_v, batch, seq, dec_batch, dec_seq, riders, tq=128):
    m = q.shape[0]
    assert tq == WINDOW and tq % dec_seq == 0
    tiles_per_seq = seq // tq
    assert tiles_per_seq % 2 == 0
    pair_steps = batch * tiles_per_seq // 2
    per_step = 2 * tq // dec_seq
    n_steps = pair_steps + dec_batch // per_step
    cur = lambda s: (s, 0)
    prev = lambda s: (jnp.where((2 * s) % tiles_per_seq == 0, 2 * s, 2 * s - 1), 0)
    cached = lambda s: (jnp.maximum(s - pair_steps, 0), 0, 0)
    cache_len = cache_k.shape[1]
    counts, r_in, r_out, r_shapes = _rider_specs(riders, n_steps, lambda s: s)
    return pl.pallas_call(
        functools.partial(_attn_kernel, tq, pair_steps, tiles_per_seq, dec_seq, counts),
        grid=(n_steps,),
        in_specs=[
            pl.BlockSpec(memory_space=pltpu.SMEM),
            pl.BlockSpec((2 * tq, Q_WIDTH), cur),
            pl.BlockSpec((tq, KV_WIDTH), prev),
            pl.BlockSpec((2 * tq, KV_WIDTH), cur),
            pl.BlockSpec((tq, KV_WIDTH), prev),
            pl.BlockSpec((2 * tq, KV_WIDTH), cur),
            pl.BlockSpec((per_step, cache_len, KV_WIDTH), cached),
            pl.BlockSpec((per_step, cache_len, KV_WIDTH), cached),
            *r_in,
        ],
        out_specs=(pl.BlockSpec((2 * tq, Q_WIDTH), cur), *r_out),
        out_shape=(jax.ShapeDtypeStruct((m, Q_WIDTH), BF16), *r_shapes),
        compiler_params=_params(("arbitrary",)),
        name="attn",
    )(sinks, q, k, k, va, va, cache_k, cache_v, *riders)


def _branch_kernel(counts, a_ref, o_ref, wa_ref, wb_ref, ga_ref, gb_ref, *refs):
    n = len(counts)
    riders, t_ref, cast = refs[:n], refs[n], refs[n + 1:]
    _rider_cast(pl.program_id(0) * pl.num_programs(1) + pl.program_id(1), counts, riders, cast)
    ya = jnp.dot(a_ref[...], wa_ref[...], preferred_element_type=F32)
    yb = jnp.dot(o_ref[...], wb_ref[...], preferred_element_type=F32)
    t = ga_ref[...].astype(F32) * ya + gb_ref[...].astype(F32) * yb
    t_ref[...] = t.astype(t_ref.dtype)


def _branch(a, o, wa, wb, g, riders, tm=1536, tn=512):
    m = a.shape[0]
    nj = D_MODEL // tn
    counts, r_in, r_out, r_shapes = _rider_specs(riders, (m // tm) * nj, lambda i, j: i * nj + j)
    return pl.pallas_call(
        functools.partial(_branch_kernel, counts),
        grid=(m // tm, nj),
        in_specs=[
            pl.BlockSpec((tm, D_MODEL), lambda i, j: (i, 0)),
            pl.BlockSpec((tm, Q_WIDTH), lambda i, j: (i, 0)),
            pl.BlockSpec((D_MODEL, tn), lambda i, j: (0, j)),
            pl.BlockSpec((Q_WIDTH, tn), lambda i, j: (0, j)),
            pl.BlockSpec((tm, tn), lambda i, j: (i, j)),
            pl.BlockSpec((tm, tn), lambda i, j: (i, j + nj)),
            *r_in,
        ],
        out_specs=(pl.BlockSpec((tm, tn), lambda i, j: (i, j)), *r_out),
        out_shape=(jax.ShapeDtypeStruct((m, D_MODEL), BF16), *r_shapes),
        compiler_params=_params(("arbitrary", "arbitrary")),
        name="branch",
    )(a, o, wa, wb, g, g, *riders)


def _out_kernel(n_prompt_blocks, t_ref, w_ref, xp_ref, xs_ref, x1_ref):
    i = pl.program_id(0)
    y = jnp.dot(t_ref[...], w_ref[...], preferred_element_type=F32)

    @pl.when(i < n_prompt_blocks)
    def _():
        x1_ref[...] = xp_ref[...] + y

    @pl.when(i >= n_prompt_blocks)
    def _():
        x1_ref[...] = xs_ref[...] + y


def _out(t, w_out, xp, xs, tm=512, tn=D_MODEL):
    m = t.shape[0]
    npb = xp.shape[0] // tm
    return pl.pallas_call(
        functools.partial(_out_kernel, npb),
        grid=(m // tm, D_MODEL // tn),
        in_specs=[
            pl.BlockSpec((tm, D_MODEL), lambda i, j: (i, 0)),
            pl.BlockSpec((D_MODEL, tn), lambda i, j: (0, j)),
            pl.BlockSpec((tm, tn), lambda i, j: (jnp.minimum(i, npb - 1), j)),
            pl.BlockSpec((tm, tn), lambda i, j: (jnp.maximum(i - npb, 0), j)),
        ],
        out_specs=pl.BlockSpec((tm, tn), lambda i, j: (i, j)),
        out_shape=jax.ShapeDtypeStruct((m, D_MODEL), F32),
        compiler_params=_params(("arbitrary", "arbitrary")),
        name="out_proj",
    )(t, w_out, xp, xs)


def _ffn_kernel(x1_ref, ng_ref, wg_ref, wu_ref, wd_ref, fg_ref, y_ref, h_ref):
    f = pl.program_id(1)

    @pl.when(f == 0)
    def _():
        x1 = x1_ref[...]
        h_ref[...] = _rms(x1, ng_ref[...]).astype(h_ref.dtype)
        y_ref[...] = x1

    h = h_ref[...]
    gate = jnp.dot(h, wg_ref[...], preferred_element_type=F32)
    up = jnp.dot(h, wu_ref[...], preferred_element_type=F32)
    hid = (jax.nn.silu(gate) * up).astype(BF16)
    y_ref[...] += jnp.dot(hid, wd_ref[...], preferred_element_type=F32)

    @pl.when(f == pl.num_programs(1) - 1)
    def _():
        y_ref[...] = _rms(y_ref[...], fg_ref[...])


def _ffn(x1, row_block_offset, n_row_blocks, norm_g, wg, wu, wd, final_g, tm=512, tf=512):
    d_ff = wg.shape[1]
    return pl.pallas_call(
        _ffn_kernel,
        grid=(n_row_blocks, d_ff // tf),
        in_specs=[
            pl.BlockSpec((tm, D_MODEL), lambda i, f: (i + row_block_offset, 0)),
            pl.BlockSpec((1, D_MODEL), lambda i, f: (0, 0)),
            pl.BlockSpec((D_MODEL, tf), lambda i, f: (0, f)),
            pl.BlockSpec((D_MODEL, tf), lambda i, f: (0, f)),
            pl.BlockSpec((tf, D_MODEL), lambda i, f: (f, 0)),
            pl.BlockSpec((1, D_MODEL), lambda i, f: (0, 0)),
        ],
        out_specs=pl.BlockSpec((tm, D_MODEL), lambda i, f: (i, 0)),
        out_shape=jax.ShapeDtypeStruct((n_row_blocks * tm, D_MODEL), F32),
        scratch_shapes=[pltpu.VMEM((tm, D_MODEL), BF16)],
        compiler_params=_params(("arbitrary", "arbitrary")),
        name="ffn",
    )(x1, norm_g, wg, wu, wd, final_g)


def _rope_tables(seq, past_len, dec_batch, dec_seq):
    pos = jnp.concatenate([jnp.arange(seq, dtype=F32),
                           jnp.tile(past_len + jnp.arange(dec_seq, dtype=F32), dec_batch)])
    inv_freq = jnp.float32(ROPE_THETA) ** (-(jnp.arange(ROT_HALF, dtype=F32) * 2.0 / ROT_DIM))
    ang = pos[:, None] * inv_freq[None, :]
    cos, sin = jnp.cos(ang), jnp.sin(ang)
    n = pos.shape[0]
    pad = jnp.zeros((n, HEAD_DIM - ROT_DIM), F32)
    zero = jnp.zeros((n, ROT_HALF), F32)
    cos_h = jnp.concatenate([cos, cos, pad + 1.0], axis=1)
    sa_h = jnp.concatenate([zero, sin, pad], axis=1)
    sb_h = jnp.concatenate([-sin, zero, pad], axis=1)
    rep = LANES // HEAD_DIM
    return jnp.tile(cos_h, (1, rep)), jnp.tile(sa_h, (1, rep)), jnp.tile(sb_h, (1, rep))


def kernel(x_prompt, x_sample, cache_swa_k, cache_swa_v, norm_mix_g, w_in, gmlp_ln_g, gmlp_ln_b,
           gmlp_ws, gmlp_bs, attn_sinks, w_gate, b_gate, w_branch_a, w_branch_b, w_out,
           norm_ffn_g, w_ffn_gate, w_ffn_up, w_ffn_down, final_norm_g):
    batch, seq, _ = x_prompt.shape
    dec_batch, dec_seq, _ = x_sample.shape
    depth = w_in.shape[0]
    assert depth == 1 and cache_swa_k.shape[2] == WINDOW and dec_seq == CHUNK
    mp, ms = batch * seq, dec_batch * dec_seq

    xp = x_prompt.reshape(mp, D_MODEL)
    xs = x_sample.reshape(ms, D_MODEL)
    cos_t, sa_t, sb_t = _rope_tables(seq, PAST_LEN, dec_batch, dec_seq)
    row = lambda p: p.reshape(1, -1)

    l = 0
    xn = _norm(xp, xs, row(norm_mix_g[l]))
    g, w_in_b = _proj_gate(xn, w_gate[l], row(b_gate[l]), [w_in[l]])
    q, k, va = _proj_qkv(xn, w_in_b, cos_t, sa_t, sb_t, mp, seq)
    uv, wg, wu = _proj_uv(xn, w_in_b, [w_ffn_gate[l], w_ffn_up[l]])
    a, vn_s = _gmlp(uv, row(gmlp_ln_g[l]), row(gmlp_ln_b[l]), gmlp_ws[l], gmlp_bs[l].T, mp)
    o, wa_b, wb_b, wo_b = _attn(
        attn_sinks[l], q, k, va,
        cache_swa_k[l].reshape(dec_batch, WINDOW, KV_WIDTH),
        cache_swa_v[l].reshape(dec_batch, WINDOW, KV_WIDTH),
        batch, seq, dec_batch, dec_seq, [w_branch_a[l], w_branch_b[l], w_out[l]])
    t, wd = _branch(a, o, wa_b, wb_b, g, [w_ffn_down[l]])
    x1 = _out(t, wo_b, xp, xs)
    ffn_tm = 1024
    ffn = functools.partial(_ffn, norm_g=row(norm_ffn_g[l]), wg=wg, wu=wu, wd=wd,
                            final_g=row(final_norm_g), tm=ffn_tm)
    y_prompt = ffn(x1, 0, mp // ffn_tm)
    y_sample = ffn(x1, mp // ffn_tm, ms // ffn_tm)

    keep = min(WINDOW, seq)
    tail = lambda z: jnp.stack([z[(b + 1) * seq - keep:(b + 1) * seq] for b in range(batch)]).reshape(
        batch, keep, N_KV_HEADS, HEAD_DIM)
    kp, vp = tail(k), tail(va)
    return (
        y_prompt.reshape(batch, seq, D_MODEL),
        y_sample.reshape(dec_batch, dec_seq, D_MODEL),
        kp[None],
        vp[None],
        k[mp:].reshape(1, dec_batch, dec_seq, N_KV_HEADS, HEAD_DIM),
        va[mp:].reshape(1, dec_batch, dec_seq, N_KV_HEADS, HEAD_DIM),
        vn_s.reshape(1, dec_batch, dec_seq, D_MODEL),
    )
```

```python
import functools

import jax
import jax.numpy as jnp
from jax import lax
from jax.experimental import pallas as pl
from jax.experimental.pallas import tpu as pltpu

D_MODEL = 2048
CHUNK = 64
GMLP_CHUNK = 128
GMLP_GROUPS = 8
GMLP_GROUP_DIM = D_MODEL // GMLP_GROUPS
N_HEADS = 32
N_KV_HEADS = 4
HEAD_DIM = 64
Q_REP = N_HEADS // N_KV_HEADS
WINDOW = 128
PAST_LEN = 2048
ROPE_THETA = 500000.0
ROT_DIM = HEAD_DIM // 4
ROT_HALF = ROT_DIM // 2
Q_WIDTH = N_HEADS * HEAD_DIM
KV_WIDTH = N_KV_HEADS * HEAD_DIM
EPS = 1e-6
NEG = -1e30
LANES = 128
LOG2E = 1.4426950408889634
Q_SCALE = HEAD_DIM ** -0.5 * LOG2E

F32 = jnp.float32
BF16 = jnp.bfloat16

VMEM_LIMIT = 56 * 1024 * 1024


def _params(semantics):
    return pltpu.CompilerParams(dimension_semantics=semantics, vmem_limit_bytes=VMEM_LIMIT)


def _rms(x, g):
    return x * lax.rsqrt(jnp.mean(x * x, axis=-1, keepdims=True) + EPS) * g


BF16_SUBLANES = 16


def _rider_specs(weights, n_steps, step_of):
    counts, in_specs, out_specs, out_shapes = [], [], [], []
    for w in weights:
        rows, cols = w.shape
        nb = max(n for n in range(1, n_steps + 1)
                 if rows % n == 0 and (rows // n) % BF16_SUBLANES == 0)
        idx = lambda *g, nb=nb: (jnp.minimum(step_of(*g), nb - 1), 0)
        counts.append(nb)
        in_specs.append(pl.BlockSpec((rows // nb, cols), idx))
        out_specs.append(pl.BlockSpec((rows // nb, cols), idx))
        out_shapes.append(jax.ShapeDtypeStruct(w.shape, BF16))
    return counts, in_specs, out_specs, out_shapes


def _rider_cast(step, counts, src_refs, dst_refs):
    for nb, src, dst in zip(counts, src_refs, dst_refs):
        @pl.when(step < nb)
        def _():
            dst[...] = src[...].astype(dst.dtype)


def _norm_kernel(n_prompt_blocks, xp_ref, xs_ref, g_ref, o_ref):
    i = pl.program_id(0)

    @pl.when(i < n_prompt_blocks)
    def _():
        o_ref[...] = _rms(xp_ref[...], g_ref[...]).astype(BF16)

    @pl.when(i >= n_prompt_blocks)
    def _():
        o_ref[...] = _rms(xs_ref[...], g_ref[...]).astype(BF16)


def _norm(xp, xs, g, tr=1024):
    mp, ms = xp.shape[0], xs.shape[0]
    npb, nsb = mp // tr, ms // tr
    return pl.pallas_call(
        functools.partial(_norm_kernel, npb),
        grid=(npb + nsb,),
        in_specs=[
            pl.BlockSpec((tr, D_MODEL), lambda i: (jnp.minimum(i, npb - 1), 0)),
            pl.BlockSpec((tr, D_MODEL), lambda i: (jnp.maximum(i - npb, 0), 0)),
            pl.BlockSpec((1, D_MODEL), lambda i: (0, 0)),
        ],
        out_specs=pl.BlockSpec((tr, D_MODEL), lambda i: (i, 0)),
        out_shape=jax.ShapeDtypeStruct((mp + ms, D_MODEL), BF16),
        compiler_params=_params(("arbitrary",)),
        name="norm",
    )(xp, xs, g)


def _rope(h, cos, sa, sb):
    return h * cos + pltpu.roll(h, ROT_HALF, 1) * sa + pltpu.roll(h, LANES - ROT_HALF, 1) * sb


GELU_C1 = 0.7978845608028654
GELU_C3 = 0.035677408136300125


def _gelu_tanh(x):
    return (0.5 * x) * (1.0 + jnp.tanh(x * (GELU_C1 + GELU_C3 * (x * x))))


def _proj_uv_kernel(counts, xn_ref, w_ref, *refs):
    n = len(counts)
    riders, uv_ref, cast = refs[:n], refs[n], refs[n + 1:]
    _rider_cast(pl.program_id(0) * pl.num_programs(1) + pl.program_id(1), counts, riders, cast)
    h = jnp.dot(xn_ref[...], w_ref[...], preferred_element_type=F32)
    uv_ref[...] = _gelu_tanh(h).astype(uv_ref.dtype)


def _proj_uv(xn, w_in, riders, tm=1024, tn=1024):
    m = xn.shape[0]
    nj = (2 * D_MODEL) // tn
    counts, r_in, r_out, r_shapes = _rider_specs(riders, (m // tm) * nj, lambda i, j: i * nj + j)
    return pl.pallas_call(
        functools.partial(_proj_uv_kernel, counts),
        grid=(m // tm, nj),
        in_specs=[
            pl.BlockSpec((tm, D_MODEL), lambda i, j: (i, 0)),
            pl.BlockSpec((D_MODEL, tn), lambda i, j: (0, j)),
            *r_in,
        ],
        out_specs=(pl.BlockSpec((tm, tn), lambda i, j: (i, j)), *r_out),
        out_shape=(jax.ShapeDtypeStruct((m, 2 * D_MODEL), BF16), *r_shapes),
        compiler_params=_params(("arbitrary", "arbitrary")),
        name="proj_uv",
    )(xn, w_in, *riders)


def _proj_qkv_kernel(xn_ref, wq_ref, wkv_ref, cos_ref, sa_ref, sb_ref, q_ref, k_ref, va_ref):
    j = pl.program_id(1)
    xn = xn_ref[...]

    def rope_cols(h, scale):
        cos, sa, sb = cos_ref[...], sa_ref[...], sb_ref[...]
        parts = [_rope(h[:, c:c + LANES], cos, sa, sb) * scale for c in range(0, h.shape[1], LANES)]
        return jnp.concatenate(parts, axis=1)

    @pl.when(j == 0)
    def _():
        h = jnp.dot(xn, wkv_ref[...], preferred_element_type=F32)
        k_ref[...] = rope_cols(h[:, :KV_WIDTH], 1.0)
        va_ref[...] = h[:, KV_WIDTH:]

    @pl.when(j > 0)
    def _():
        h = jnp.dot(xn, wq_ref[...], preferred_element_type=F32)
        q_ref[...] = rope_cols(h, Q_SCALE).astype(q_ref.dtype)


def _proj_qkv(xn, w_in, cos_t, sa_t, sb_t, n_prompt_rows, seq, tm=1024, tn=Q_WIDTH):
    m = xn.shape[0]
    nq = Q_WIDTH // tn
    q_block0 = (2 * D_MODEL) // tn
    kv_block = (2 * D_MODEL + Q_WIDTH) // (2 * KV_WIDTH)
    npb = n_prompt_rows // tm
    blocks_per_seq = seq // tm

    def tab_idx(i, j):
        return (jnp.where(i < npb, i % blocks_per_seq, blocks_per_seq), 0)

    tab_spec = pl.BlockSpec((tm, LANES), tab_idx)
    qcol = lambda j: jnp.maximum(j - 1, 0)
    return pl.pallas_call(
        _proj_qkv_kernel,
        grid=(m // tm, nq + 1),
        in_specs=[
            pl.BlockSpec((tm, D_MODEL), lambda i, j: (i, 0)),
            pl.BlockSpec((D_MODEL, tn), lambda i, j: (0, q_block0 + qcol(j))),
            pl.BlockSpec((D_MODEL, 2 * KV_WIDTH), lambda i, j: (0, kv_block)),
            tab_spec, tab_spec, tab_spec,
        ],
        out_specs=(
            pl.BlockSpec((tm, tn), lambda i, j: (i, qcol(j))),
            pl.BlockSpec((tm, KV_WIDTH), lambda i, j: (i, 0)),
            pl.BlockSpec((tm, KV_WIDTH), lambda i, j: (i, 0)),
        ),
        out_shape=(
            jax.ShapeDtypeStruct((m, Q_WIDTH), BF16),
            jax.ShapeDtypeStruct((m, KV_WIDTH), F32),
            jax.ShapeDtypeStruct((m, KV_WIDTH), F32),
        ),
        compiler_params=_params(("arbitrary", "arbitrary")),
        name="proj_qkv",
    )(xn, w_in, w_in, cos_t, sa_t, sb_t)


def _proj_gate_kernel(counts, xn_ref, w_ref, b_ref, *refs):
    n = len(counts)
    riders, g_ref, cast = refs[:n], refs[n], refs[n + 1:]
    h = jnp.dot(xn_ref[...], w_ref[...].astype(BF16), preferred_element_type=F32)
    g_ref[...] = jax.nn.sigmoid(h + b_ref[...]).astype(g_ref.dtype)
    _rider_cast(pl.program_id(0) * pl.num_programs(1) + pl.program_id(1), counts, riders, cast)


def _proj_gate(xn, w_gate, b_gate, riders, tm=1536, tn=1024):
    m = xn.shape[0]
    n = w_gate.shape[1]
    nj = n // tn
    counts, r_in, r_out, r_shapes = _rider_specs(riders, (m // tm) * nj, lambda i, j: i * nj + j)
    return pl.pallas_call(
        functools.partial(_proj_gate_kernel, counts),
        grid=(m // tm, nj),
        in_specs=[
            pl.BlockSpec((tm, D_MODEL), lambda i, j: (i, 0)),
            pl.BlockSpec((D_MODEL, tn), lambda i, j: (0, j)),
            pl.BlockSpec((1, tn), lambda i, j: (0, j)),
            *r_in,
        ],
        out_specs=(pl.BlockSpec((tm, tn), lambda i, j: (i, j)), *r_out),
        out_shape=(jax.ShapeDtypeStruct((m, n), BF16), *r_shapes),
        compiler_params=_params(("arbitrary", "arbitrary")),
        name="proj_gate",
    )(xn, w_gate, b_gate, *riders)


def _gmlp_kernel(n_prompt_blocks, tr, u_ref, v_ref, lng_ref, lnb_ref, ws_ref, bst_ref,
                 a_ref, vn_ref):
    i = pl.program_id(0)
    v = v_ref[...].astype(F32)
    mu = jnp.mean(v, axis=-1, keepdims=True)
    vc = v - mu
    var = jnp.mean(vc * vc, axis=-1, keepdims=True)
    vn = vc * lax.rsqrt(var + EPS) * lng_ref[...] + lnb_ref[...]
    vnb = vn.astype(BF16)

    def mix(length):
        r = lax.broadcasted_iota(jnp.int32, (length, length), 0) // CHUNK
        c = lax.broadcasted_iota(jnp.int32, (length, length), 1) // CHUNK
        for g in range(GMLP_GROUPS):
            w = jnp.where(c <= r, ws_ref[g, :length, :length], 0.0).astype(BF16)
            bias = bst_ref[:length, g:g + 1]
            cols = slice(g * GMLP_GROUP_DIM, (g + 1) * GMLP_GROUP_DIM)
            for b in range(tr // length):
                rows = slice(b * length, (b + 1) * length)
                s = jnp.dot(w, vnb[rows, cols], preferred_element_type=F32) + bias
                a_ref[rows, cols] = (u_ref[rows, cols].astype(F32) * s).astype(a_ref.dtype)

    @pl.when(i < n_prompt_blocks)
    def _():
        mix(GMLP_CHUNK)

    @pl.when(i >= n_prompt_blocks)
    def _():
        vn_ref[...] = vn
        mix(CHUNK)


def _gmlp(uv, ln_g, ln_b, ws, bs_t, n_prompt_rows, tr=1024):
    m = uv.shape[0]
    npb = n_prompt_rows // tr
    return pl.pallas_call(
        functools.partial(_gmlp_kernel, npb, tr),
        grid=(m // tr,),
        in_specs=[
            pl.BlockSpec((tr, D_MODEL), lambda i: (i, 0)),
            pl.BlockSpec((tr, D_MODEL), lambda i: (i, 1)),
            pl.BlockSpec((1, D_MODEL), lambda i: (0, 0)),
            pl.BlockSpec((1, D_MODEL), lambda i: (0, 0)),
            pl.BlockSpec((GMLP_GROUPS, GMLP_CHUNK, GMLP_CHUNK), lambda i: (0, 0, 0)),
            pl.BlockSpec((GMLP_CHUNK, GMLP_GROUPS), lambda i: (0, 0)),
        ],
        out_specs=(
            pl.BlockSpec((tr, D_MODEL), lambda i: (i, 0)),
            pl.BlockSpec((tr, D_MODEL), lambda i: (jnp.maximum(i - npb, 0), 0)),
        ),
        out_shape=(
            jax.ShapeDtypeStruct((m, D_MODEL), BF16),
            jax.ShapeDtypeStruct((m - n_prompt_rows, D_MODEL), F32),
        ),
        compiler_params=_params(("arbitrary",)),
        name="gmlp",
    )(uv, uv, ln_g, ln_b, ws, bs_t)


def _dup_head(pair, pair_swapped, low_half, odd):
    if odd:
        return jnp.where(low_half, pair_swapped, pair).astype(BF16)
    return jnp.where(low_half, pair, pair_swapped).astype(BF16)


def _attend(q_ref, kwin, vwin, mask, sink_ref, o_ref):
    rows, keys = q_ref.shape[0], kwin.shape[0]
    ones = jnp.ones((keys, LANES), BF16)
    low_q = lax.broadcasted_iota(jnp.int32, (rows, LANES), 1) < HEAD_DIM
    low_k = lax.broadcasted_iota(jnp.int32, (keys, LANES), 1) < HEAD_DIM
    heads_per_pair = LANES // HEAD_DIM
    for c in range(N_KV_HEADS // heads_per_pair):
        kpair = kwin[:, c * LANES:(c + 1) * LANES]
        vpair = vwin[:, c * LANES:(c + 1) * LANES]
        kswap = pltpu.roll(kpair, HEAD_DIM, 1)
        vswap = pltpu.roll(vpair, HEAD_DIM, 1)
        for odd in range(heads_per_pair):
            hk = c * heads_per_pair + odd
            kdup = _dup_head(kpair, kswap, low_k, odd)
            vaug = jnp.concatenate([_dup_head(vpair, vswap, low_k, odd), ones], axis=1)
            parts = []
            for r in range(Q_REP):
                h = hk * Q_REP + r
                qpair = q_ref[:, (h // 2) * LANES:(h // 2 + 1) * LANES]
                keep = low_q if h % 2 == 0 else jnp.logical_not(low_q)
                parts.append(jnp.where(keep, qpair, jnp.zeros_like(qpair)))
            s_all = lax.dot_general(jnp.concatenate(parts, axis=0), kdup, (((1,), (1,)), ((), ())),
                                    preferred_element_type=F32)
            ps, sinks, mxs = [], [], []
            for r in range(Q_REP):
                s = s_all[r * rows:(r + 1) * rows]
                if mask is not None:
                    s = mask(s)
                sink = sink_ref[hk * Q_REP + r] * LOG2E
                mx = jnp.maximum(jnp.max(s, axis=-1, keepdims=True), sink)
                ps.append(jnp.exp2(s - mx).astype(BF16))
                sinks.append(sink)
                mxs.append(mx)
            o_all = jnp.dot(jnp.concatenate(ps, axis=0), vaug, preferred_element_type=F32)
            for r in range(0, Q_REP, 2):
                h = hk * Q_REP + r
                even, oddh = o_all[r * rows:(r + 1) * rows], o_all[(r + 1) * rows:(r + 2) * rows]
                esink = jnp.exp2(jnp.where(low_q, sinks[r], sinks[r + 1])
                                 - jnp.where(low_q, mxs[r], mxs[r + 1]))
                den = jnp.where(low_q, even[:, LANES:], oddh[:, LANES:]) + esink
                num = jnp.where(low_q, even[:, :LANES], oddh[:, :LANES])
                o_ref[:, (h // 2) * LANES:(h // 2 + 1) * LANES] = (num / den).astype(o_ref.dtype)


def _band_mask(t, tq):
    row = lax.broadcasted_iota(jnp.int32, (tq, tq), 0)
    col = lax.broadcasted_iota(jnp.int32, (tq, tq), 1)
    valid_prev = ((row < CHUNK) | (col >= CHUNK)) & (t > 0)
    valid_cur_top = lax.broadcasted_iota(jnp.int32, (CHUNK, tq), 1) < CHUNK

    def mask(sc):
        prev = jnp.where(valid_prev, sc[:, :tq], NEG)
        cur_top = jnp.where(valid_cur_top, sc[:CHUNK, tq:], NEG)
        cur = jnp.concatenate([cur_top, sc[CHUNK:, tq:]], axis=0)
        return jnp.concatenate([prev, cur], axis=1)

    return mask


def _attn_kernel(tq, nt, prompt_steps, tiles_per_seq, dec_seq, counts, sink_ref, q_ref, kp_ref, kc_ref,
                 vp_ref, vc_ref, ck_ref, cv_ref, *refs):
    n = len(counts)
    riders, o_ref, cast = refs[:n], refs[n], refs[n + 1:]
    s = pl.program_id(0)
    _rider_cast(s, counts, riders, cast)
    kc, vc = kc_ref[...], vc_ref[...]

    @pl.when(s < prompt_steps)
    def _():
        t0 = (nt * s) % tiles_per_seq
        for i in range(nt):
            rows = pl.ds(i * tq, tq)
            if i == 0:
                kwin = jnp.concatenate([kp_ref[...], kc[:tq]], axis=0)
                vwin = jnp.concatenate([vp_ref[...], vc[:tq]], axis=0)
            else:
                kwin, vwin = kc[(i - 1) * tq:(i + 1) * tq], vc[(i - 1) * tq:(i + 1) * tq]
            _attend(q_ref.at[rows], kwin, vwin, _band_mask(t0 + i, tq), sink_ref, o_ref.at[rows])

    @pl.when(s >= prompt_steps)
    def _():
        for b in range(nt * tq // dec_seq):
            rows = pl.ds(b * dec_seq, dec_seq)
            new = slice(b * dec_seq, (b + 1) * dec_seq)
            kwin = jnp.concatenate([ck_ref[b], kc[new]], axis=0)
            vwin = jnp.concatenate([cv_ref[b], vc[new]], axis=0)
            _attend(q_ref.at[rows], kwin, vwin, None, sink_ref, o_ref.at[rows])


def _attn(sinks, q, k, va, cache_k, cache_v, batch, seq, dec_batch, dec_seq, riders, tq=128, nt=8):
    m = q.shape[0]
    assert tq == WINDOW and tq % dec_seq == 0
    tiles_per_seq = seq // tq
    assert tiles_per_seq % nt == 0
    prompt_steps = batch * tiles_per_seq // nt
    per_step = nt * tq // dec_seq
    assert dec_batch % per_step == 0
    n_steps = prompt_steps + dec_batch // per_step
    cur = lambda s: (s, 0)
    prev = lambda s: (jnp.where((nt * s) % tiles_per_seq == 0, nt * s, nt * s - 1), 0)
    cached = lambda s: (jnp.maximum(s - prompt_steps, 0), 0, 0)
    cache_len = cache_k.shape[1]
    counts, r_in, r_out, r_shapes = _rider_specs(riders, n_steps, lambda s: s)
    return pl.pallas_call(
        functools.partial(_attn_kernel, tq, nt, prompt_steps, tiles_per_seq, dec_seq, counts),
        grid=(n_steps,),
        in_specs=[
            pl.BlockSpec(memory_space=pltpu.SMEM),
            pl.BlockSpec((nt * tq, Q_WIDTH), cur),
            pl.BlockSpec((tq, KV_WIDTH), prev),
            pl.BlockSpec((nt * tq, KV_WIDTH), cur),
            pl.BlockSpec((tq, KV_WIDTH), prev),
            pl.BlockSpec((nt * tq, KV_WIDTH), cur),
            pl.BlockSpec((per_step, cache_len, KV_WIDTH), cached),
            pl.BlockSpec((per_step, cache_len, KV_WIDTH), cached),
            *r_in,
        ],
        out_specs=(pl.BlockSpec((nt * tq, Q_WIDTH), cur), *r_out),
        out_shape=(jax.ShapeDtypeStruct((m, Q_WIDTH), BF16), *r_shapes),
        compiler_params=_params(("arbitrary",)),
        name="attn",
    )(sinks, q, k, k, va, va, cache_k, cache_v, *riders)


def _branch_kernel(counts, a_ref, o_ref, wa_ref, wb_ref, ga_ref, gb_ref, *refs):
    n = len(counts)
    riders, t_ref, cast = refs[:n], refs[n], refs[n + 1:]
    _rider_cast(pl.program_id(0) * pl.num_programs(1) + pl.program_id(1), counts, riders, cast)
    ya = jnp.dot(a_ref[...], wa_ref[...], preferred_element_type=F32)
    yb = jnp.dot(o_ref[...], wb_ref[...], preferred_element_type=F32)
    t = ga_ref[...].astype(F32) * ya + gb_ref[...].astype(F32) * yb
    t_ref[...] = t.astype(t_ref.dtype)


def _branch(a, o, wa, wb, g, riders, tm=1536, tn=512):
    m = a.shape[0]
    nj = D_MODEL // tn
    counts, r_in, r_out, r_shapes = _rider_specs(riders, (m // tm) * nj, lambda i, j: i * nj + j)
    return pl.pallas_call(
        functools.partial(_branch_kernel, counts),
        grid=(m // tm, nj),
        in_specs=[
            pl.BlockSpec((tm, D_MODEL), lambda i, j: (i, 0)),
            pl.BlockSpec((tm, Q_WIDTH), lambda i, j: (i, 0)),
            pl.BlockSpec((D_MODEL, tn), lambda i, j: (0, j)),
            pl.BlockSpec((Q_WIDTH, tn), lambda i, j: (0, j)),
            pl.BlockSpec((tm, tn), lambda i, j: (i, j)),
            pl.BlockSpec((tm, tn), lambda i, j: (i, j + nj)),
            *r_in,
        ],
        out_specs=(pl.BlockSpec((tm, tn), lambda i, j: (i, j)), *r_out),
        out_shape=(jax.ShapeDtypeStruct((m, D_MODEL), BF16), *r_shapes),
        compiler_params=_params(("arbitrary", "arbitrary")),
        name="branch",
    )(a, o, wa, wb, g, g, *riders)


def _out_kernel(n_prompt_blocks, t_ref, w_ref, xp_ref, xs_ref, x1_ref):
    i = pl.program_id(0)
    y = jnp.dot(t_ref[...], w_ref[...], preferred_element_type=F32)

    @pl.when(i < n_prompt_blocks)
    def _():
        x1_ref[...] = xp_ref[...] + y

    @pl.when(i >= n_prompt_blocks)
    def _():
        x1_ref[...] = xs_ref[...] + y


def _out(t, w_out, xp, xs, tm=512, tn=D_MODEL):
    m = t.shape[0]
    npb = xp.shape[0] // tm
    return pl.pallas_call(
        functools.partial(_out_kernel, npb),
        grid=(m // tm, D_MODEL // tn),
        in_specs=[
            pl.BlockSpec((tm, D_MODEL), lambda i, j: (i, 0)),
            pl.BlockSpec((D_MODEL, tn), lambda i, j: (0, j)),
            pl.BlockSpec((tm, tn), lambda i, j: (jnp.minimum(i, npb - 1), j)),
            pl.BlockSpec((tm, tn), lambda i, j: (jnp.maximum(i - npb, 0), j)),
        ],
        out_specs=pl.BlockSpec((tm, tn), lambda i, j: (i, j)),
        out_shape=jax.ShapeDtypeStruct((m, D_MODEL), F32),
        compiler_params=_params(("arbitrary", "arbitrary")),
        name="out_proj",
    )(t, w_out, xp, xs)


def _ffn_kernel(x1_ref, ng_ref, wg_ref, wu_ref, wd_ref, fg_ref, y_ref, h_ref):
    f = pl.program_id(1)

    @pl.when(f == 0)
    def _():
        x1 = x1_ref[...]
        h_ref[...] = _rms(x1, ng_ref[...]).astype(h_ref.dtype)
        y_ref[...] = x1

    h = h_ref[...]
    gate = jnp.dot(h, wg_ref[...], preferred_element_type=F32)
    up = jnp.dot(h, wu_ref[...], preferred_element_type=F32)
    hid = (jax.nn.silu(gate) * up).astype(BF16)
    y_ref[...] += jnp.dot(hid, wd_ref[...], preferred_element_type=F32)

    @pl.when(f == pl.num_programs(1) - 1)
    def _():
        y_ref[...] = _rms(y_ref[...], fg_ref[...])


def _ffn(x1, row_block_offset, n_row_blocks, norm_g, wg, wu, wd, final_g, tm=512, tf=512):
    d_ff = wg.shape[1]
    return pl.pallas_call(
        _ffn_kernel,
        grid=(n_row_blocks, d_ff // tf),
        in_specs=[
            pl.BlockSpec((tm, D_MODEL), lambda i, f: (i + row_block_offset, 0)),
            pl.BlockSpec((1, D_MODEL), lambda i, f: (0, 0)),
            pl.BlockSpec((D_MODEL, tf), lambda i, f: (0, f)),
            pl.BlockSpec((D_MODEL, tf), lambda i, f: (0, f)),
            pl.BlockSpec((tf, D_MODEL), lambda i, f: (f, 0)),
            pl.BlockSpec((1, D_MODEL), lambda i, f: (0, 0)),
        ],
        out_specs=pl.BlockSpec((tm, D_MODEL), lambda i, f: (i, 0)),
        out_shape=jax.ShapeDtypeStruct((n_row_blocks * tm, D_MODEL), F32),
        scratch_shapes=[pltpu.VMEM((tm, D_MODEL), BF16)],
        compiler_params=_params(("arbitrary", "arbitrary")),
        name="ffn",
    )(x1, norm_g, wg, wu, wd, final_g)


def _rope_tables(seq, past_len, dec_batch, dec_seq):
    pos = jnp.concatenate([jnp.arange(seq, dtype=F32),
                           jnp.tile(past_len + jnp.arange(dec_seq, dtype=F32), dec_batch)])
    inv_freq = jnp.float32(ROPE_THETA) ** (-(jnp.arange(ROT_HALF, dtype=F32) * 2.0 / ROT_DIM))
    ang = pos[:, None] * inv_freq[None, :]
    cos, sin = jnp.cos(ang), jnp.sin(ang)
    n = pos.shape[0]
    pad = jnp.zeros((n, HEAD_DIM - ROT_DIM), F32)
    zero = jnp.zeros((n, ROT_HALF), F32)
    cos_h = jnp.concatenate([cos, cos, pad + 1.0], axis=1)
    sa_h = jnp.concatenate([zero, sin, pad], axis=1)
    sb_h = jnp.concatenate([-sin, zero, pad], axis=1)
    rep = LANES // HEAD_DIM
    return jnp.tile(cos_h, (1, rep)), jnp.tile(sa_h, (1, rep)), jnp.tile(sb_h, (1, rep))


def kernel(x_prompt, x_sample, cache_swa_k, cache_swa_v, norm_mix_g, w_in, gmlp_ln_g, gmlp_ln_b,
           gmlp_ws, gmlp_bs, attn_sinks, w_gate, b_gate, w_branch_a, w_branch_b, w_out,
           norm_ffn_g, w_ffn_gate, w_ffn_up, w_ffn_down, final_norm_g):
    batch, seq, _ = x_prompt.shape
    dec_batch, dec_seq, _ = x_sample.shape
    depth = w_in.shape[0]
    assert depth == 1 and cache_swa_k.shape[2] == WINDOW and dec_seq == CHUNK
    mp, ms = batch * seq, dec_batch * dec_seq

    xp = x_prompt.reshape(mp, D_MODEL)
    xs = x_sample.reshape(ms, D_MODEL)
    cos_t, sa_t, sb_t = _rope_tables(seq, PAST_LEN, dec_batch, dec_seq)
    row = lambda p: p.reshape(1, -1)

    l = 0
    xn = _norm(xp, xs, row(norm_mix_g[l]))
    g, w_in_b = _proj_gate(xn, w_gate[l], row(b_gate[l]), [w_in[l]])
    q, k, va = _proj_qkv(xn, w_in_b, cos_t, sa_t, sb_t, mp, seq)
    uv, wg, wu = _proj_uv(xn, w_in_b, [w_ffn_gate[l], w_ffn_up[l]])
    a, vn_s = _gmlp(uv, row(gmlp_ln_g[l]), row(gmlp_ln_b[l]), gmlp_ws[l], gmlp_bs[l].T, mp)
    o, wa_b, wb_b, wo_b = _attn(
        attn_sinks[l], q, k, va,
        cache_swa_k[l].reshape(dec_batch, WINDOW, KV_WIDTH),
        cache_swa_v[l].reshape(dec_batch, WINDOW, KV_WIDTH),
        batch, seq, dec_batch, dec_seq, [w_branch_a[l], w_branch_b[l], w_out[l]])
    t, wd = _branch(a, o, wa_b, wb_b, g, [w_ffn_down[l]])
    x1 = _out(t, wo_b, xp, xs)
    ffn_tm = 1024
    ffn = functools.partial(_ffn, norm_g=row(norm_ffn_g[l]), wg=wg, wu=wu, wd=wd,
                            final_g=row(final_norm_g), tm=ffn_tm)
    y_prompt = ffn(x1, 0, mp // ffn_tm)
    y_sample = ffn(x1, mp // ffn_tm, ms // ffn_tm)

    keep = min(WINDOW, seq)
    tail = lambda z: jnp.stack([z[(b + 1) * seq - keep:(b + 1) * seq] for b in range(batch)]).reshape(
        batch, keep, N_KV_HEADS, HEAD_DIM)
    kp, vp = tail(k), tail(va)
    return (
        y_prompt.reshape(batch, seq, D_MODEL),
        y_sample.reshape(dec_batch, dec_seq, D_MODEL),
        kp[None],
        vp[None],
        k[mp:].reshape(1, dec_batch, dec_seq, N_KV_HEADS, HEAD_DIM),
        va[mp:].reshape(1, dec_batch, dec_seq, N_KV_HEADS, HEAD_DIM),
        vn_s.reshape(1, dec_batch, dec_seq, D_MODEL),
    )
```

```python
import functools

import jax
import jax.numpy as jnp
from jax import lax
from jax.experimental import pallas as pl
from jax.experimental.pallas import tpu as pltpu

D_MODEL = 2048
CHUNK = 64
GMLP_CHUNK = 128
GMLP_GROUPS = 8
GMLP_GROUP_DIM = D_MODEL // GMLP_GROUPS
N_HEADS = 32
N_KV_HEADS = 4
HEAD_DIM = 64
Q_REP = N_HEADS // N_KV_HEADS
WINDOW = 128
PAST_LEN = 2048
ROPE_THETA = 500000.0
ROT_DIM = HEAD_DIM // 4
ROT_HALF = ROT_DIM // 2
Q_WIDTH = N_HEADS * HEAD_DIM
KV_WIDTH = N_KV_HEADS * HEAD_DIM
EPS = 1e-6
NEG = -1e30
LANES = 128
LOG2E = 1.4426950408889634
Q_SCALE = HEAD_DIM ** -0.5 * LOG2E

F32 = jnp.float32
BF16 = jnp.bfloat16

VMEM_LIMIT = 56 * 1024 * 1024


def _params(semantics):
    return pltpu.CompilerParams(dimension_semantics=semantics, vmem_limit_bytes=VMEM_LIMIT)


def _rms(x, g):
    return x * lax.rsqrt(jnp.mean(x * x, axis=-1, keepdims=True) + EPS) * g


BF16_SUBLANES = 16


def _rider_specs(weights, n_steps, step_of):
    counts, in_specs, out_specs, out_shapes = [], [], [], []
    for w in weights:
        rows, cols = w.shape
        nb = max(n for n in range(1, n_steps + 1)
                 if rows % n == 0 and (rows // n) % BF16_SUBLANES == 0)
        idx = lambda *g, nb=nb: (jnp.minimum(step_of(*g), nb - 1), 0)
        counts.append(nb)
        in_specs.append(pl.BlockSpec((rows // nb, cols), idx))
        out_specs.append(pl.BlockSpec((rows // nb, cols), idx))
        out_shapes.append(jax.ShapeDtypeStruct(w.shape, BF16))
    return counts, in_specs, out_specs, out_shapes


def _rider_cast(step, counts, src_refs, dst_refs):
    for nb, src, dst in zip(counts, src_refs, dst_refs):
        @pl.when(step < nb)
        def _():
            dst[...] = src[...].astype(dst.dtype)


def _norm_kernel(n_prompt_blocks, xp_ref, xs_ref, g_ref, o_ref):
    i = pl.program_id(0)

    @pl.when(i < n_prompt_blocks)
    def _():
        o_ref[...] = _rms(xp_ref[...], g_ref[...]).astype(BF16)

    @pl.when(i >= n_prompt_blocks)
    def _():
        o_ref[...] = _rms(xs_ref[...], g_ref[...]).astype(BF16)


def _norm(xp, xs, g, tr=1024):
    mp, ms = xp.shape[0], xs.shape[0]
    npb, nsb = mp // tr, ms // tr
    return pl.pallas_call(
        functools.partial(_norm_kernel, npb),
        grid=(npb + nsb,),
        in_specs=[
            pl.BlockSpec((tr, D_MODEL), lambda i: (jnp.minimum(i, npb - 1), 0)),
            pl.BlockSpec((tr, D_MODEL), lambda i: (jnp.maximum(i - npb, 0), 0)),
            pl.BlockSpec((1, D_MODEL), lambda i: (0, 0)),
        ],
        out_specs=pl.BlockSpec((tr, D_MODEL), lambda i: (i, 0)),
        out_shape=jax.ShapeDtypeStruct((mp + ms, D_MODEL), BF16),
        compiler_params=_params(("arbitrary",)),
        name="norm",
    )(xp, xs, g)


def _rope(h, cos, sa, sb):
    return h * cos + pltpu.roll(h, ROT_HALF, 1) * sa + pltpu.roll(h, LANES - ROT_HALF, 1) * sb


GELU_C1 = 0.7978845608028654
GELU_C3 = 0.035677408136300125


def _gelu_tanh(x):
    return (0.5 * x) * (1.0 + jnp.tanh(x * (GELU_C1 + GELU_C3 * (x * x))))


def _proj_uv_kernel(counts, xn_ref, w_ref, *refs):
    n = len(counts)
    riders, uv_ref, cast = refs[:n], refs[n], refs[n + 1:]
    _rider_cast(pl.program_id(0) * pl.num_programs(1) + pl.program_id(1), counts, riders, cast)
    h = jnp.dot(xn_ref[...], w_ref[...], preferred_element_type=F32)
    uv_ref[...] = _gelu_tanh(h).astype(uv_ref.dtype)


def _proj_uv(xn, w_in, riders, tm=1024, tn=1024):
    m = xn.shape[0]
    nj = (2 * D_MODEL) // tn
    counts, r_in, r_out, r_shapes = _rider_specs(riders, (m // tm) * nj, lambda i, j: i * nj + j)
    return pl.pallas_call(
        functools.partial(_proj_uv_kernel, counts),
        grid=(m // tm, nj),
        in_specs=[
            pl.BlockSpec((tm, D_MODEL), lambda i, j: (i, 0)),
            pl.BlockSpec((D_MODEL, tn), lambda i, j: (0, j)),
            *r_in,
        ],
        out_specs=(pl.BlockSpec((tm, tn), lambda i, j: (i, j)), *r_out),
        out_shape=(jax.ShapeDtypeStruct((m, 2 * D_MODEL), BF16), *r_shapes),
        compiler_params=_params(("arbitrary", "arbitrary")),
        name="proj_uv",
    )(xn, w_in, *riders)


def _proj_qkv_kernel(xn_ref, wq_ref, wkv_ref, cos_ref, sa_ref, sb_ref, q_ref, k_ref, va_ref):
    j = pl.program_id(1)
    xn = xn_ref[...]

    def rope_cols(h, scale):
        cos, sa, sb = cos_ref[...], sa_ref[...], sb_ref[...]
        parts = [_rope(h[:, c:c + LANES], cos, sa, sb) * scale for c in range(0, h.shape[1], LANES)]
        return jnp.concatenate(parts, axis=1)

    @pl.when(j == 0)
    def _():
        h = jnp.dot(xn, wkv_ref[...], preferred_element_type=F32)
        k_ref[...] = rope_cols(h[:, :KV_WIDTH], 1.0)
        va_ref[...] = h[:, KV_WIDTH:]

    @pl.when(j > 0)
    def _():
        h = jnp.dot(xn, wq_ref[...], preferred_element_type=F32)
        q_ref[...] = rope_cols(h, Q_SCALE).astype(q_ref.dtype)


def _proj_qkv(xn, w_in, cos_t, sa_t, sb_t, n_prompt_rows, seq, tm=1024, tn=Q_WIDTH):
    m = xn.shape[0]
    nq = Q_WIDTH // tn
    q_block0 = (2 * D_MODEL) // tn
    kv_block = (2 * D_MODEL + Q_WIDTH) // (2 * KV_WIDTH)
    npb = n_prompt_rows // tm
    blocks_per_seq = seq // tm

    def tab_idx(i, j):
        return (jnp.where(i < npb, i % blocks_per_seq, blocks_per_seq), 0)

    tab_spec = pl.BlockSpec((tm, LANES), tab_idx)
    qcol = lambda j: jnp.maximum(j - 1, 0)
    return pl.pallas_call(
        _proj_qkv_kernel,
        grid=(m // tm, nq + 1),
        in_specs=[
            pl.BlockSpec((tm, D_MODEL), lambda i, j: (i, 0)),
            pl.BlockSpec((D_MODEL, tn), lambda i, j: (0, q_block0 + qcol(j))),
            pl.BlockSpec((D_MODEL, 2 * KV_WIDTH), lambda i, j: (0, kv_block)),
            tab_spec, tab_spec, tab_spec,
        ],
        out_specs=(
            pl.BlockSpec((tm, tn), lambda i, j: (i, qcol(j))),
            pl.BlockSpec((tm, KV_WIDTH), lambda i, j: (i, 0)),
            pl.BlockSpec((tm, KV_WIDTH), lambda i, j: (i, 0)),
        ),
        out_shape=(
            jax.ShapeDtypeStruct((m, Q_WIDTH), BF16),
            jax.ShapeDtypeStruct((m, KV_WIDTH), F32),
            jax.ShapeDtypeStruct((m, KV_WIDTH), F32),
        ),
        compiler_params=_params(("arbitrary", "arbitrary")),
        name="proj_qkv",
    )(xn, w_in, w_in, cos_t, sa_t, sb_t)


def _proj_gate_kernel(counts, xn_ref, w_ref, b_ref, *refs):
    n = len(counts)
    riders, g_ref, cast = refs[:n], refs[n], refs[n + 1:]
    h = jnp.dot(xn_ref[...], w_ref[...].astype(BF16), preferred_element_type=F32)
    g_ref[...] = jax.nn.sigmoid(h + b_ref[...]).astype(g_ref.dtype)
    _rider_cast(pl.program_id(0) * pl.num_programs(1) + pl.program_id(1), counts, riders, cast)


def _proj_gate(xn, w_gate, b_gate, riders, tm=1536, tn=1024):
    m = xn.shape[0]
    n = w_gate.shape[1]
    nj = n // tn
    counts, r_in, r_out, r_shapes = _rider_specs(riders, (m // tm) * nj, lambda i, j: i * nj + j)
    return pl.pallas_call(
        functools.partial(_proj_gate_kernel, counts),
        grid=(m // tm, nj),
        in_specs=[
            pl.BlockSpec((tm, D_MODEL), lambda i, j: (i, 0)),
            pl.BlockSpec((D_MODEL, tn), lambda i, j: (0, j)),
            pl.BlockSpec((1, tn), lambda i, j: (0, j)),
            *r_in,
        ],
        out_specs=(pl.BlockSpec((tm, tn), lambda i, j: (i, j)), *r_out),
        out_shape=(jax.ShapeDtypeStruct((m, n), BF16), *r_shapes),
        compiler_params=_params(("arbitrary", "arbitrary")),
        name="proj_gate",
    )(xn, w_gate, b_gate, *riders)


def _gmlp_kernel(n_prompt_blocks, tr, u_ref, v_ref, lng_ref, lnb_ref, ws_ref, bst_ref,
                 a_ref, vn_ref):
    i = pl.program_id(0)
    v = v_ref[...].astype(F32)
    mu = jnp.mean(v, axis=-1, keepdims=True)
    vc = v - mu
    var = jnp.mean(vc * vc, axis=-1, keepdims=True)
    vn = vc * lax.rsqrt(var + EPS) * lng_ref[...] + lnb_ref[...]
    vnb = vn.astype(BF16)

    def mix(length):
        r = lax.broadcasted_iota(jnp.int32, (length, length), 0) // CHUNK
        c = lax.broadcasted_iota(jnp.int32, (length, length), 1) // CHUNK
        for g in range(GMLP_GROUPS):
            w = jnp.where(c <= r, ws_ref[g, :length, :length], 0.0).astype(BF16)
            bias = bst_ref[:length, g:g + 1]
            cols = slice(g * GMLP_GROUP_DIM, (g + 1) * GMLP_GROUP_DIM)
            for b in range(tr // length):
                rows = slice(b * length, (b + 1) * length)
                s = jnp.dot(w, vnb[rows, cols], preferred_element_type=F32) + bias
                a_ref[rows, cols] = (u_ref[rows, cols].astype(F32) * s).astype(a_ref.dtype)

    @pl.when(i < n_prompt_blocks)
    def _():
        mix(GMLP_CHUNK)

    @pl.when(i >= n_prompt_blocks)
    def _():
        vn_ref[...] = vn
        mix(CHUNK)


def _gmlp(uv, ln_g, ln_b, ws, bs_t, n_prompt_rows, tr=1024):
    m = uv.shape[0]
    npb = n_prompt_rows // tr
    return pl.pallas_call(
        functools.partial(_gmlp_kernel, npb, tr),
        grid=(m // tr,),
        in_specs=[
            pl.BlockSpec((tr, D_MODEL), lambda i: (i, 0)),
            pl.BlockSpec((tr, D_MODEL), lambda i: (i, 1)),
            pl.BlockSpec((1, D_MODEL), lambda i: (0, 0)),
            pl.BlockSpec((1, D_MODEL), lambda i: (0, 0)),
            pl.BlockSpec((GMLP_GROUPS, GMLP_CHUNK, GMLP_CHUNK), lambda i: (0, 0, 0)),
            pl.BlockSpec((GMLP_CHUNK, GMLP_GROUPS), lambda i: (0, 0)),
        ],
        out_specs=(
            pl.BlockSpec((tr, D_MODEL), lambda i: (i, 0)),
            pl.BlockSpec((tr, D_MODEL), lambda i: (jnp.maximum(i - npb, 0), 0)),
        ),
        out_shape=(
            jax.ShapeDtypeStruct((m, D_MODEL), BF16),
            jax.ShapeDtypeStruct((m - n_prompt_rows, D_MODEL), F32),
        ),
        compiler_params=_params(("arbitrary",)),
        name="gmlp",
    )(uv, uv, ln_g, ln_b, ws, bs_t)


def _dup_head(pair, pair_swapped, low_half, odd):
    if odd:
        return jnp.where(low_half, pair_swapped, pair).astype(BF16)
    return jnp.where(low_half, pair, pair_swapped).astype(BF16)


def _attend(q_ref, kwin, vwin, mask, sink_ref, o_ref):
    rows, keys = q_ref.shape[0], kwin.shape[0]
    ones = jnp.ones((keys, LANES), BF16)
    low_q = lax.broadcasted_iota(jnp.int32, (rows, LANES), 1) < HEAD_DIM
    low_k = lax.broadcasted_iota(jnp.int32, (keys, LANES), 1) < HEAD_DIM
    heads_per_pair = LANES // HEAD_DIM
    for c in range(N_KV_HEADS // heads_per_pair):
        kpair = kwin[:, c * LANES:(c + 1) * LANES]
        vpair = vwin[:, c * LANES:(c + 1) * LANES]
        kswap = pltpu.roll(kpair, HEAD_DIM, 1)
        vswap = pltpu.roll(vpair, HEAD_DIM, 1)
        for odd in range(heads_per_pair):
            hk = c * heads_per_pair + odd
            kdup = _dup_head(kpair, kswap, low_k, odd)
            vaug = jnp.concatenate([_dup_head(vpair, vswap, low_k, odd), ones], axis=1)
            parts = []
            for r in range(Q_REP):
                h = hk * Q_REP + r
                qpair = q_ref[:, (h // 2) * LANES:(h // 2 + 1) * LANES]
                keep = low_q if h % 2 == 0 else jnp.logical_not(low_q)
                parts.append(jnp.where(keep, qpair, jnp.zeros_like(qpair)))
            s_all = lax.dot_general(jnp.concatenate(parts, axis=0), kdup, (((1,), (1,)), ((), ())),
                                    preferred_element_type=F32)
            ps, sinks, mxs = [], [], []
            for r in range(Q_REP):
                s = s_all[r * rows:(r + 1) * rows]
                if mask is not None:
                    s = mask(s)
                sink = sink_ref[hk * Q_REP + r] * LOG2E
                mx = jnp.maximum(jnp.max(s, axis=-1, keepdims=True), sink)
                ps.append(jnp.exp2(s - mx).astype(BF16))
                sinks.append(sink)
                mxs.append(mx)
            o_all = jnp.dot(jnp.concatenate(ps, axis=0), vaug, preferred_element_type=F32)
            for r in range(0, Q_REP, 2):
                h = hk * Q_REP + r
                even, oddh = o_all[r * rows:(r + 1) * rows], o_all[(r + 1) * rows:(r + 2) * rows]
                esink = jnp.exp2(jnp.where(low_q, sinks[r], sinks[r + 1])
                                 - jnp.where(low_q, mxs[r], mxs[r + 1]))
                den = jnp.where(low_q, even[:, LANES:], oddh[:, LANES:]) + esink
                num = jnp.where(low_q, even[:, :LANES], oddh[:, :LANES])
                o_ref[:, (h // 2) * LANES:(h // 2 + 1) * LANES] = (num / den).astype(o_ref.dtype)


def _band_mask(t, tq):
    row = lax.broadcasted_iota(jnp.int32, (tq, tq), 0)
    col = lax.broadcasted_iota(jnp.int32, (tq, tq), 1)
    valid_prev = ((row < CHUNK) | (col >= CHUNK)) & (t > 0)
    valid_cur_top = lax.broadcasted_iota(jnp.int32, (CHUNK, tq), 1) < CHUNK

    def mask(sc):
        prev = jnp.where(valid_prev, sc[:, :tq], NEG)
        cur_top = jnp.where(valid_cur_top, sc[:CHUNK, tq:], NEG)
        cur = jnp.concatenate([cur_top, sc[CHUNK:, tq:]], axis=0)
        return jnp.concatenate([prev, cur], axis=1)

    return mask


def _attn_kernel(tq, nt, prompt_steps, tiles_per_seq, dec_seq, counts, sink_ref, q_ref, kp_ref, kc_ref,
                 vp_ref, vc_ref, ck_ref, cv_ref, *refs):
    n = len(counts)
    riders, o_ref, cast = refs[:n], refs[n], refs[n + 1:]
    s = pl.program_id(0)
    _rider_cast(s, counts, riders, cast)
    kc, vc = kc_ref[...], vc_ref[...]

    @pl.when(s < prompt_steps)
    def _():
        t0 = (nt * s) % tiles_per_seq
        for i in range(nt):
            rows = pl.ds(i * tq, tq)
            if i == 0:
                kwin = jnp.concatenate([kp_ref[...], kc[:tq]], axis=0)
                vwin = jnp.concatenate([vp_ref[...], vc[:tq]], axis=0)
            else:
                kwin, vwin = kc[(i - 1) * tq:(i + 1) * tq], vc[(i - 1) * tq:(i + 1) * tq]
            _attend(q_ref.at[rows], kwin, vwin, _band_mask(t0 + i, tq), sink_ref, o_ref.at[rows])

    @pl.when(s >= prompt_steps)
    def _():
        for b in range(nt * tq // dec_seq):
            rows = pl.ds(b * dec_seq, dec_seq)
            new = slice(b * dec_seq, (b + 1) * dec_seq)
            kwin = jnp.concatenate([ck_ref[b], kc[new]], axis=0)
            vwin = jnp.concatenate([cv_ref[b], vc[new]], axis=0)
            _attend(q_ref.at[rows], kwin, vwin, None, sink_ref, o_ref.at[rows])


def _attn(sinks, q, k, va, cache_k, cache_v, batch, seq, dec_batch, dec_seq, riders, tq=128, nt=4):
    m = q.shape[0]
    assert tq == WINDOW and tq % dec_seq == 0
    tiles_per_seq = seq // tq
    assert tiles_per_seq % nt == 0
    prompt_steps = batch * tiles_per_seq // nt
    per_step = nt * tq // dec_seq
    assert dec_batch % per_step == 0
    n_steps = prompt_steps + dec_batch // per_step
    cur = lambda s: (s, 0)
    prev = lambda s: (jnp.where((nt * s) % tiles_per_seq == 0, nt * s, nt * s - 1), 0)
    cached = lambda s: (jnp.maximum(s - prompt_steps, 0), 0, 0)
    cache_len = cache_k.shape[1]
    counts, r_in, r_out, r_shapes = _rider_specs(riders, n_steps, lambda s: s)
    return pl.pallas_call(
        functools.partial(_attn_kernel, tq, nt, prompt_steps, tiles_per_seq, dec_seq, counts),
        grid=(n_steps,),
        in_specs=[
            pl.BlockSpec(memory_space=pltpu.SMEM),
            pl.BlockSpec((nt * tq, Q_WIDTH), cur),
            pl.BlockSpec((tq, KV_WIDTH), prev),
            pl.BlockSpec((nt * tq, KV_WIDTH), cur),
            pl.BlockSpec((tq, KV_WIDTH), prev),
            pl.BlockSpec((nt * tq, KV_WIDTH), cur),
            pl.BlockSpec((per_step, cache_len, KV_WIDTH), cached),
            pl.BlockSpec((per_step, cache_len, KV_WIDTH), cached),
            *r_in,
        ],
        out_specs=(pl.BlockSpec((nt * tq, Q_WIDTH), cur), *r_out),
        out_shape=(jax.ShapeDtypeStruct((m, Q_WIDTH), BF16), *r_shapes),
        compiler_params=_params(("arbitrary",)),
        name="attn",
    )(sinks, q, k, k, va, va, cache_k, cache_v, *riders)


def _branch_kernel(counts, a_ref, o_ref, wa_ref, wb_ref, ga_ref, gb_ref, *refs):
    n = len(counts)
    riders, t_ref, cast = refs[:n], refs[n], refs[n + 1:]
    _rider_cast(pl.program_id(0) * pl.num_programs(1) + pl.program_id(1), counts, riders, cast)
    ya = jnp.dot(a_ref[...], wa_ref[...], preferred_element_type=F32)
    yb = jnp.dot(o_ref[...], wb_ref[...], preferred_element_type=F32)
    t = ga_ref[...].astype(F32) * ya + gb_ref[...].astype(F32) * yb
    t_ref[...] = t.astype(t_ref.dtype)


def _branch(a, o, wa, wb, g, riders, tm=1536, tn=512):
    m = a.shape[0]
    nj = D_MODEL // tn
    counts, r_in, r_out, r_shapes = _rider_specs(riders, (m // tm) * nj, lambda i, j: i * nj + j)
    return pl.pallas_call(
        functools.partial(_branch_kernel, counts),
        grid=(m // tm, nj),
        in_specs=[
            pl.BlockSpec((tm, D_MODEL), lambda i, j: (i, 0)),
            pl.BlockSpec((tm, Q_WIDTH), lambda i, j: (i, 0)),
            pl.BlockSpec((D_MODEL, tn), lambda i, j: (0, j)),
            pl.BlockSpec((Q_WIDTH, tn), lambda i, j: (0, j)),
            pl.BlockSpec((tm, tn), lambda i, j: (i, j)),
            pl.BlockSpec((tm, tn), lambda i, j: (i, j + nj)),
            *r_in,
        ],
        out_specs=(pl.BlockSpec((tm, tn), lambda i, j: (i, j)), *r_out),
        out_shape=(jax.ShapeDtypeStruct((m, D_MODEL), BF16), *r_shapes),
        compiler_params=_params(("arbitrary", "arbitrary")),
        name="branch",
    )(a, o, wa, wb, g, g, *riders)


def _out_kernel(n_prompt_blocks, t_ref, w_ref, xp_ref, xs_ref, x1_ref):
    i = pl.program_id(0)
    y = jnp.dot(t_ref[...], w_ref[...], preferred_element_type=F32)

    @pl.when(i < n_prompt_blocks)
    def _():
        x1_ref[...] = xp_ref[...] + y

    @pl.when(i >= n_prompt_blocks)
    def _():
        x1_ref[...] = xs_ref[...] + y


def _out(t, w_out, xp, xs, tm=256, tn=D_MODEL):
    m = t.shape[0]
    npb = xp.shape[0] // tm
    return pl.pallas_call(
        functools.partial(_out_kernel, npb),
        grid=(m // tm, D_MODEL // tn),
        in_specs=[
            pl.BlockSpec((tm, D_MODEL), lambda i, j: (i, 0)),
            pl.BlockSpec((D_MODEL, tn), lambda i, j: (0, j)),
            pl.BlockSpec((tm, tn), lambda i, j: (jnp.minimum(i, npb - 1), j)),
            pl.BlockSpec((tm, tn), lambda i, j: (jnp.maximum(i - npb, 0), j)),
        ],
        out_specs=pl.BlockSpec((tm, tn), lambda i, j: (i, j)),
        out_shape=jax.ShapeDtypeStruct((m, D_MODEL), F32),
        compiler_params=_params(("arbitrary", "arbitrary")),
        name="out_proj",
    )(t, w_out, xp, xs)


def _ffn_kernel(x1_ref, ng_ref, wg_ref, wu_ref, wd_ref, fg_ref, y_ref, h_ref):
    f = pl.program_id(1)

    @pl.when(f == 0)
    def _():
        x1 = x1_ref[...]
        h_ref[...] = _rms(x1, ng_ref[...]).astype(h_ref.dtype)
        y_ref[...] = x1

    h = h_ref[...]
    gate = jnp.dot(h, wg_ref[...], preferred_element_type=F32)
    up = jnp.dot(h, wu_ref[...], preferred_element_type=F32)
    hid = (jax.nn.silu(gate) * up).astype(BF16)
    y_ref[...] += jnp.dot(hid, wd_ref[...], preferred_element_type=F32)

    @pl.when(f == pl.num_programs(1) - 1)
    def _():
        y_ref[...] = _rms(y_ref[...], fg_ref[...])


def _ffn(x1, row_block_offset, n_row_blocks, norm_g, wg, wu, wd, final_g, tm=512, tf=512):
    d_ff = wg.shape[1]
    return pl.pallas_call(
        _ffn_kernel,
        grid=(n_row_blocks, d_ff // tf),
        in_specs=[
            pl.BlockSpec((tm, D_MODEL), lambda i, f: (i + row_block_offset, 0)),
            pl.BlockSpec((1, D_MODEL), lambda i, f: (0, 0)),
            pl.BlockSpec((D_MODEL, tf), lambda i, f: (0, f)),
            pl.BlockSpec((D_MODEL, tf), lambda i, f: (0, f)),
            pl.BlockSpec((tf, D_MODEL), lambda i, f: (f, 0)),
            pl.BlockSpec((1, D_MODEL), lambda i, f: (0, 0)),
        ],
        out_specs=pl.BlockSpec((tm, D_MODEL), lambda i, f: (i, 0)),
        out_shape=jax.ShapeDtypeStruct((n_row_blocks * tm, D_MODEL), F32),
        scratch_shapes=[pltpu.VMEM((tm, D_MODEL), BF16)],
        compiler_params=_params(("arbitrary", "arbitrary")),
        name="ffn",
    )(x1, norm_g, wg, wu, wd, final_g)


def _rope_tables(seq, past_len, dec_batch, dec_seq):
    pos = jnp.concatenate([jnp.arange(seq, dtype=F32),
                           jnp.tile(past_len + jnp.arange(dec_seq, dtype=F32), dec_batch)])
    inv_freq = jnp.float32(ROPE_THETA) ** (-(jnp.arange(ROT_HALF, dtype=F32) * 2.0 / ROT_DIM))
    ang = pos[:, None] * inv_freq[None, :]
    cos, sin = jnp.cos(ang), jnp.sin(ang)
    n = pos.shape[0]
    pad = jnp.zeros((n, HEAD_DIM - ROT_DIM), F32)
    zero = jnp.zeros((n, ROT_HALF), F32)
    cos_h = jnp.concatenate([cos, cos, pad + 1.0], axis=1)
    sa_h = jnp.concatenate([zero, sin, pad], axis=1)
    sb_h = jnp.concatenate([-sin, zero, pad], axis=1)
    rep = LANES // HEAD_DIM
    return jnp.tile(cos_h, (1, rep)), jnp.tile(sa_h, (1, rep)), jnp.tile(sb_h, (1, rep))


def kernel(x_prompt, x_sample, cache_swa_k, cache_swa_v, norm_mix_g, w_in, gmlp_ln_g, gmlp_ln_b,
           gmlp_ws, gmlp_bs, attn_sinks, w_gate, b_gate, w_branch_a, w_branch_b, w_out,
           norm_ffn_g, w_ffn_gate, w_ffn_up, w_ffn_down, final_norm_g):
    batch, seq, _ = x_prompt.shape
    dec_batch, dec_seq, _ = x_sample.shape
    depth = w_in.shape[0]
    assert depth == 1 and cache_swa_k.shape[2] == WINDOW and dec_seq == CHUNK
    mp, ms = batch * seq, dec_batch * dec_seq

    xp = x_prompt.reshape(mp, D_MODEL)
    xs = x_sample.reshape(ms, D_MODEL)
    cos_t, sa_t, sb_t = _rope_tables(seq, PAST_LEN, dec_batch, dec_seq)
    row = lambda p: p.reshape(1, -1)

    l = 0
    xn = _norm(xp, xs, row(norm_mix_g[l]))
    g, w_in_b = _proj_gate(xn, w_gate[l], row(b_gate[l]), [w_in[l]])
    q, k, va = _proj_qkv(xn, w_in_b, cos_t, sa_t, sb_t, mp, seq)
    uv, wg, wu = _proj_uv(xn, w_in_b, [w_ffn_gate[l], w_ffn_up[l]])
    a, vn_s = _gmlp(uv, row(gmlp_ln_g[l]), row(gmlp_ln_b[l]), gmlp_ws[l], gmlp_bs[l].T, mp)
    o, wa_b, wb_b, wo_b = _attn(
        attn_sinks[l], q, k, va,
        cache_swa_k[l].reshape(dec_batch, WINDOW, KV_WIDTH),
        cache_swa_v[l].reshape(dec_batch, WINDOW, KV_WIDTH),
        batch, seq, dec_batch, dec_seq, [w_branch_a[l], w_branch_b[l], w_out[l]])
    t, wd = _branch(a, o, wa_b, wb_b, g, [w_ffn_down[l]])
    x1 = _out(t, wo_b, xp, xs)
    ffn_tm = 1024
    ffn = functools.partial(_ffn, norm_g=row(norm_ffn_g[l]), wg=wg, wu=wu, wd=wd,
                            final_g=row(final_norm_g), tm=ffn_tm)
    y_prompt = ffn(x1, 0, mp // ffn_tm)
    y_sample = ffn(x1, mp // ffn_tm, ms // ffn_tm)

    keep = min(WINDOW, seq)
    tail = lambda z: jnp.stack([z[(b + 1) * seq - keep:(b + 1) * seq] for b in range(batch)]).reshape(
        batch, keep, N_KV_HEADS, HEAD_DIM)
    kp, vp = tail(k), tail(va)
    return (
        y_prompt.reshape(batch, seq, D_MODEL),
        y_sample.reshape(dec_batch, dec_seq, D_MODEL),
        kp[None],
        vp[None],
        k[mp:].reshape(1, dec_batch, dec_seq, N_KV_HEADS, HEAD_DIM),
        va[mp:].reshape(1, dec_batch, dec_seq, N_KV_HEADS, HEAD_DIM),
        vn_s.reshape(1, dec_batch, dec_seq, D_MODEL),
    )
```
